```python
import math
import jax, jax.numpy as jnp
from jax import lax
import numpy as np

D_MODEL = 1024
BATCH = 8
SEQ = 8192
DEPTH = 2

N_MIXERS = 2
N_ATTN_LAYERS = (DEPTH + N_MIXERS - 1) // N_MIXERS
N_RG_LAYERS = DEPTH // N_MIXERS

N_HEADS = 16
HEAD_DIM = D_MODEL // N_HEADS
Q_BLOCK = 128

D_RNN = 1344
RG_BLOCKS = 16
RG_BLOCK_W = D_RNN // RG_BLOCKS
RG_CONV_W = 4
RG_C = 8.0

D_FF = 2816
FFN_CONV_W = 3

NORM_EPS = 1e-6

kernel_name = "hybrid_stickbreak_rglru_convffn"


def rms_norm(x, g):
    xf = x.astype(jnp.float32)
    inv = lax.rsqrt(jnp.mean(xf * xf, axis=-1, keepdims=True) + NORM_EPS)
    return (xf * inv * g.astype(jnp.float32)).astype(x.dtype)


def causal_depthwise_conv(x, w, b):
    k = w.shape[0]
    out = lax.conv_general_dilated(
        x, w[:, None, :].astype(x.dtype), window_strides=(1,),
        padding=[(k - 1, 0)], dimension_numbers=("NWC", "WIO", "NWC"),
        feature_group_count=x.shape[-1])
    return out + b.astype(x.dtype)


def stick_breaking_attention(q, k, v):
    seq = q.shape[2]
    scale = HEAD_DIM ** -0.5
    outs = []
    for qb in range(seq // Q_BLOCK):
        q0 = qb * Q_BLOCK
        kend = q0 + Q_BLOCK
        z = jnp.einsum("bhtd,bhsd->bhts", q[:, :, q0:kend], k[:, :, :kend]).astype(jnp.float32) * scale
        t_pos = q0 + jnp.arange(Q_BLOCK)[:, None]
        s_pos = jnp.arange(kend)[None, :]
        causal = s_pos < t_pos
        log_beta = jax.nn.log_sigmoid(z)
        log_1m_beta = jnp.where(causal, jax.nn.log_sigmoid(-z), 0.0)
        log_stick = lax.cumsum(log_1m_beta, axis=3, reverse=True) - log_1m_beta
        w = jnp.where(causal, jnp.exp(log_beta + log_stick), 0.0)
        outs.append(jnp.einsum("bhts,bhsd->bhtd", w.astype(v.dtype), v[:, :, :kend]))
    return jnp.concatenate(outs, axis=2)


def attention_mixer(h, w_qkv, w_o):
    b, s, _ = h.shape
    qkv = (h @ w_qkv).reshape(b, s, 3, N_HEADS, HEAD_DIM)
    qkv = jnp.transpose(qkv, (2, 0, 3, 1, 4))
    o = stick_breaking_attention(qkv[0], qkv[1], qkv[2])
    o = jnp.transpose(o, (0, 2, 1, 3)).reshape(b, s, D_MODEL)
    return o @ w_o


def block_diag_linear(x, w, bias):
    b, s, _ = x.shape
    xb = x.reshape(b, s, RG_BLOCKS, RG_BLOCK_W)
    y = jnp.einsum("bsnc,ncd->bsnd", xb, w).reshape(b, s, D_RNN)
    return y + bias


def rg_lru(x, w_a, b_a, w_x, b_x, lam):
    r = jax.nn.sigmoid(block_diag_linear(x, w_a, b_a).astype(jnp.float32))
    i = jax.nn.sigmoid(block_diag_linear(x, w_x, b_x).astype(jnp.float32))
    log_a = -RG_C * r * jax.nn.softplus(-lam.astype(jnp.float32))
    a = jnp.exp(log_a)
    mult = jnp.sqrt(-jnp.expm1(2.0 * log_a))
    u = mult * i * x.astype(jnp.float32)

    def combine(left, right):
        a_l, b_l = left
        a_r, b_r = right
        return a_l * a_r, a_r * b_l + b_r

    _, h = lax.associative_scan(combine, (a, u), axis=1)
    return h.astype(x.dtype)


def recurrent_mixer(h, w_in, conv_w, conv_b, w_a, b_a, w_x, b_x, lam, w_out):
    proj = h @ w_in
    gate_branch, rec_branch = jnp.split(proj, 2, axis=-1)
    gate = jax.nn.gelu(gate_branch, approximate=True)
    rec = causal_depthwise_conv(rec_branch, conv_w, conv_b)
    rec = rg_lru(rec, w_a, b_a, w_x, b_x, lam)
    return (gate * rec) @ w_out


def conv_ffn(h, w_up, conv_w, conv_b, w_down):
    u = causal_depthwise_conv(h @ w_up, conv_w, conv_b)
    g, val = jnp.split(u, 2, axis=-1)
    return (jax.nn.gelu(g, approximate=True) * val) @ w_down


def _fwd_setup_inputs(seed: int = 0) -> dict:
    key = jax.random.key(seed)
    ks = iter(jax.random.split(key, 32))

    def nrm(shape, fan_in):
        return jax.random.normal(next(ks), shape, jnp.float32) * (fan_in ** -0.5)

    def gain(shape):
        return 1.0 + 0.02 * jax.random.normal(next(ks), shape, jnp.float32)

    def small(shape):
        return 0.01 * jax.random.normal(next(ks), shape, jnp.float32)

    x = jax.random.normal(next(ks), (BATCH, SEQ, D_MODEL), jnp.float32)

    attn_w_qkv = nrm((N_ATTN_LAYERS, D_MODEL, 3 * D_MODEL), D_MODEL)
    attn_w_o = nrm((N_ATTN_LAYERS, D_MODEL, D_MODEL), D_MODEL)

    rg_w_in = nrm((N_RG_LAYERS, D_MODEL, 2 * D_RNN), D_MODEL)
    rg_conv_w = nrm((N_RG_LAYERS, RG_CONV_W, D_RNN), RG_CONV_W)
    rg_conv_b = small((N_RG_LAYERS, D_RNN))
    rg_w_a = nrm((N_RG_LAYERS, RG_BLOCKS, RG_BLOCK_W, RG_BLOCK_W), RG_BLOCK_W)
    rg_b_a = small((N_RG_LAYERS, D_RNN))
    rg_w_x = nrm((N_RG_LAYERS, RG_BLOCKS, RG_BLOCK_W, RG_BLOCK_W), RG_BLOCK_W)
    rg_b_x = small((N_RG_LAYERS, D_RNN))
    u = jax.random.uniform(next(ks), (N_RG_LAYERS, D_RNN), jnp.float32, 0.9, 0.999)
    a0 = u ** (1.0 / RG_C)
    rg_lambda = jnp.log(a0) - jnp.log1p(-a0)
    rg_w_out = nrm((N_RG_LAYERS, D_RNN, D_MODEL), D_RNN)

    ffn_w_up = nrm((DEPTH, D_MODEL, 2 * D_FF), D_MODEL)
    ffn_conv_w = nrm((DEPTH, FFN_CONV_W, 2 * D_FF), FFN_CONV_W)
    ffn_conv_b = small((DEPTH, 2 * D_FF))
    ffn_w_down = nrm((DEPTH, D_FF, D_MODEL), D_FF)

    mix_pre_g = gain((DEPTH, D_MODEL))
    mix_post_g = gain((DEPTH, D_MODEL))
    ffn_pre_g = gain((DEPTH, D_MODEL))
    ffn_post_g = gain((DEPTH, D_MODEL))

    return {
        "x": x,
        "attn_w_qkv": attn_w_qkv, "attn_w_o": attn_w_o,
        "rg_w_in": rg_w_in, "rg_conv_w": rg_conv_w, "rg_conv_b": rg_conv_b,
        "rg_w_a": rg_w_a, "rg_b_a": rg_b_a, "rg_w_x": rg_w_x, "rg_b_x": rg_b_x,
        "rg_lambda": rg_lambda, "rg_w_out": rg_w_out,
        "ffn_w_up": ffn_w_up, "ffn_conv_w": ffn_conv_w, "ffn_conv_b": ffn_conv_b,
        "ffn_w_down": ffn_w_down,
        "mix_pre_g": mix_pre_g, "mix_post_g": mix_post_g,
        "ffn_pre_g": ffn_pre_g, "ffn_post_g": ffn_post_g,
    }


def _fwd_reference(x, attn_w_qkv, attn_w_o, rg_w_in, rg_conv_w, rg_conv_b, rg_w_a, rg_b_a,
              rg_w_x, rg_b_x, rg_lambda, rg_w_out, ffn_w_up, ffn_conv_w, ffn_conv_b,
              ffn_w_down, mix_pre_g, mix_post_g, ffn_pre_g, ffn_post_g):
    for i in range(DEPTH):
        j = i // N_MIXERS
        h = rms_norm(x, mix_pre_g[i])
        if i % N_MIXERS == 0:
            h = attention_mixer(h, attn_w_qkv[j], attn_w_o[j])
        else:
            h = recurrent_mixer(h, rg_w_in[j], rg_conv_w[j], rg_conv_b[j], rg_w_a[j], rg_b_a[j],
                                rg_w_x[j], rg_b_x[j], rg_lambda[j], rg_w_out[j])
        x = x + rms_norm(h, mix_post_g[i])
        h = conv_ffn(rms_norm(x, ffn_pre_g[i]), ffn_w_up[i], ffn_conv_w[i], ffn_conv_b[i], ffn_w_down[i])
        x = x + rms_norm(h, ffn_post_g[i])
    return x


import jax as _jax
import jax.numpy as _jnp

TWIN_FORMAT = 'train_step'
FWD_PARAMS = ['x', 'attn_w_qkv', 'attn_w_o', 'rg_w_in', 'rg_conv_w', 'rg_conv_b', 'rg_w_a', 'rg_b_a', 'rg_w_x', 'rg_b_x', 'rg_lambda', 'rg_w_out', 'ffn_w_up', 'ffn_conv_w', 'ffn_conv_b', 'ffn_w_down', 'mix_pre_g', 'mix_post_g', 'ffn_pre_g', 'ffn_post_g']
TWIN_WEIGHTS = ['attn_w_qkv', 'attn_w_o', 'rg_w_in', 'rg_conv_w', 'rg_conv_b', 'rg_w_a', 'rg_b_a', 'rg_w_x', 'rg_b_x', 'rg_lambda', 'rg_w_out', 'ffn_w_up', 'ffn_conv_w', 'ffn_conv_b', 'ffn_w_down', 'mix_pre_g', 'mix_post_g', 'ffn_pre_g', 'ffn_post_g']
TWIN_DIFF_INPUT = 'x'
TWIN_INPUTS = ['x', 'attn_w_qkv', 'attn_w_o', 'rg_w_in', 'rg_conv_w', 'rg_conv_b', 'rg_w_a', 'rg_b_a', 'rg_w_x', 'rg_b_x', 'rg_lambda', 'rg_w_out', 'ffn_w_up', 'ffn_conv_w', 'ffn_conv_b', 'ffn_w_down', 'mix_pre_g', 'mix_post_g', 'ffn_pre_g', 'ffn_post_g', 'loss_target', 'm_attn_w_qkv', 'm_attn_w_o', 'm_rg_w_in', 'm_rg_conv_w', 'm_rg_conv_b', 'm_rg_w_a', 'm_rg_b_a', 'm_rg_w_x', 'm_rg_b_x', 'm_rg_lambda', 'm_rg_w_out', 'm_ffn_w_up', 'm_ffn_conv_w', 'm_ffn_conv_b', 'm_ffn_w_down', 'm_mix_pre_g', 'm_mix_post_g', 'm_ffn_pre_g', 'm_ffn_post_g', 'v_attn_w_qkv', 'v_attn_w_o', 'v_rg_w_in', 'v_rg_conv_w', 'v_rg_conv_b', 'v_rg_w_a', 'v_rg_b_a', 'v_rg_w_x', 'v_rg_b_x', 'v_rg_lambda', 'v_rg_w_out', 'v_ffn_w_up', 'v_ffn_conv_w', 'v_ffn_conv_b', 'v_ffn_w_down', 'v_mix_pre_g', 'v_mix_post_g', 'v_ffn_pre_g', 'v_ffn_post_g']
TWIN_OUTPUTS = ['loss', 'grad_x', 'grad_attn_w_qkv', 'grad_attn_w_o', 'grad_rg_w_in', 'grad_rg_conv_w', 'grad_rg_conv_b', 'grad_rg_w_a', 'grad_rg_b_a', 'grad_rg_w_x', 'grad_rg_b_x', 'grad_rg_lambda', 'grad_rg_w_out', 'grad_ffn_w_up', 'grad_ffn_conv_w', 'grad_ffn_conv_b', 'grad_ffn_w_down', 'grad_mix_pre_g', 'grad_mix_post_g', 'grad_ffn_pre_g', 'grad_ffn_post_g', 'delta_attn_w_qkv', 'delta_attn_w_o', 'delta_rg_w_in', 'delta_rg_conv_w', 'delta_rg_conv_b', 'delta_rg_w_a', 'delta_rg_b_a', 'delta_rg_w_x', 'delta_rg_b_x', 'delta_rg_lambda', 'delta_rg_w_out', 'delta_ffn_w_up', 'delta_ffn_conv_w', 'delta_ffn_conv_b', 'delta_ffn_w_down', 'delta_mix_pre_g', 'delta_mix_post_g', 'delta_ffn_pre_g', 'delta_ffn_post_g', 'new_m_attn_w_qkv', 'new_m_attn_w_o', 'new_m_rg_w_in', 'new_m_rg_conv_w', 'new_m_rg_conv_b', 'new_m_rg_w_a', 'new_m_rg_b_a', 'new_m_rg_w_x', 'new_m_rg_b_x', 'new_m_rg_lambda', 'new_m_rg_w_out', 'new_m_ffn_w_up', 'new_m_ffn_conv_w', 'new_m_ffn_conv_b', 'new_m_ffn_w_down', 'new_m_mix_pre_g', 'new_m_mix_post_g', 'new_m_ffn_pre_g', 'new_m_ffn_post_g', 'new_v_attn_w_qkv', 'new_v_attn_w_o', 'new_v_rg_w_in', 'new_v_rg_conv_w', 'new_v_rg_conv_b', 'new_v_rg_w_a', 'new_v_rg_b_a', 'new_v_rg_w_x', 'new_v_rg_b_x', 'new_v_rg_lambda', 'new_v_rg_w_out', 'new_v_ffn_w_up', 'new_v_ffn_conv_w', 'new_v_ffn_conv_b', 'new_v_ffn_w_down', 'new_v_mix_pre_g', 'new_v_mix_post_g', 'new_v_ffn_pre_g', 'new_v_ffn_post_g']
TWIN_LEAF_KINDS = {'loss': 'loss', 'grad_x': 'grad_x', 'grad_attn_w_qkv': 'grad_w', 'grad_attn_w_o': 'grad_w', 'grad_rg_w_in': 'grad_w', 'grad_rg_conv_w': 'grad_w', 'grad_rg_conv_b': 'grad_w', 'grad_rg_w_a': 'grad_w', 'grad_rg_b_a': 'grad_w', 'grad_rg_w_x': 'grad_w', 'grad_rg_b_x': 'grad_w', 'grad_rg_lambda': 'grad_w', 'grad_rg_w_out': 'grad_w', 'grad_ffn_w_up': 'grad_w', 'grad_ffn_conv_w': 'grad_w', 'grad_ffn_conv_b': 'grad_w', 'grad_ffn_w_down': 'grad_w', 'grad_mix_pre_g': 'grad_w', 'grad_mix_post_g': 'grad_w', 'grad_ffn_pre_g': 'grad_w', 'grad_ffn_post_g': 'grad_w', 'delta_attn_w_qkv': 'delta_w', 'delta_attn_w_o': 'delta_w', 'delta_rg_w_in': 'delta_w', 'delta_rg_conv_w': 'delta_w', 'delta_rg_conv_b': 'delta_w', 'delta_rg_w_a': 'delta_w', 'delta_rg_b_a': 'delta_w', 'delta_rg_w_x': 'delta_w', 'delta_rg_b_x': 'delta_w', 'delta_rg_lambda': 'delta_w', 'delta_rg_w_out': 'delta_w', 'delta_ffn_w_up': 'delta_w', 'delta_ffn_conv_w': 'delta_w', 'delta_ffn_conv_b': 'delta_w', 'delta_ffn_w_down': 'delta_w', 'delta_mix_pre_g': 'delta_w', 'delta_mix_post_g': 'delta_w', 'delta_ffn_pre_g': 'delta_w', 'delta_ffn_post_g': 'delta_w', 'new_m_attn_w_qkv': 'new_m', 'new_m_attn_w_o': 'new_m', 'new_m_rg_w_in': 'new_m', 'new_m_rg_conv_w': 'new_m', 'new_m_rg_conv_b': 'new_m', 'new_m_rg_w_a': 'new_m', 'new_m_rg_b_a': 'new_m', 'new_m_rg_w_x': 'new_m', 'new_m_rg_b_x': 'new_m', 'new_m_rg_lambda': 'new_m', 'new_m_rg_w_out': 'new_m', 'new_m_ffn_w_up': 'new_m', 'new_m_ffn_conv_w': 'new_m', 'new_m_ffn_conv_b': 'new_m', 'new_m_ffn_w_down': 'new_m', 'new_m_mix_pre_g': 'new_m', 'new_m_mix_post_g': 'new_m', 'new_m_ffn_pre_g': 'new_m', 'new_m_ffn_post_g': 'new_m', 'new_v_attn_w_qkv': 'new_v', 'new_v_attn_w_o': 'new_v', 'new_v_rg_w_in': 'new_v', 'new_v_rg_conv_w': 'new_v', 'new_v_rg_conv_b': 'new_v', 'new_v_rg_w_a': 'new_v', 'new_v_rg_b_a': 'new_v', 'new_v_rg_w_x': 'new_v', 'new_v_rg_b_x': 'new_v', 'new_v_rg_lambda': 'new_v', 'new_v_rg_w_out': 'new_v', 'new_v_ffn_w_up': 'new_v', 'new_v_ffn_conv_w': 'new_v', 'new_v_ffn_conv_b': 'new_v', 'new_v_ffn_w_down': 'new_v', 'new_v_mix_pre_g': 'new_v', 'new_v_mix_post_g': 'new_v', 'new_v_ffn_pre_g': 'new_v', 'new_v_ffn_post_g': 'new_v'}


def _forward(args):
    return _fwd_reference(*[args[k] for k in FWD_PARAMS])


def _output_shape():
    def fwd():
        inp = _fwd_setup_inputs(0)
        return _fwd_reference(*[inp[k] for k in FWD_PARAMS])
    out = _jax.eval_shape(fwd)
    return out.shape, out.dtype

N_MICROBATCH = 1
ADAM_LR = 0.001
ADAM_B1 = 0.9
ADAM_B2 = 0.999
ADAM_EPS = 1e-08
ADAM_WD = 0.01
ADAM_STEP = 10
PER_EXAMPLE_BATCH_AXIS = {'x': 0, 'loss_target': 0}
SHARED_INPUTS = []
_WEIGHT_DTYPES = {'attn_w_qkv': _jnp.float32, 'attn_w_o': _jnp.float32, 'rg_w_in': _jnp.float32, 'rg_conv_w': _jnp.float32, 'rg_conv_b': _jnp.float32, 'rg_w_a': _jnp.float32, 'rg_b_a': _jnp.float32, 'rg_w_x': _jnp.float32, 'rg_b_x': _jnp.float32, 'rg_lambda': _jnp.float32, 'rg_w_out': _jnp.float32, 'ffn_w_up': _jnp.float32, 'ffn_conv_w': _jnp.float32, 'ffn_conv_b': _jnp.float32, 'ffn_w_down': _jnp.float32, 'mix_pre_g': _jnp.float32, 'mix_post_g': _jnp.float32, 'ffn_pre_g': _jnp.float32, 'ffn_post_g': _jnp.float32}
MOMENT_SCALE = {'attn_w_qkv': 9.450654e-01, 'attn_w_o': 1.532180e+00, 'rg_w_in': 8.074688e-01, 'rg_conv_w': 9.990800e-01, 'rg_conv_b': 2.427788e+01, 'rg_w_a': 6.062332e-01, 'rg_b_a': 3.307855e-01, 'rg_w_x': 1.138904e+00, 'rg_b_x': 2.780751e-01, 'rg_lambda': 5.307385e-01, 'rg_w_out': 1.441289e+00, 'ffn_w_up': 5.660291e-01, 'ffn_conv_w': 6.279689e-01, 'ffn_conv_b': 3.452888e+00, 'ffn_w_down': 1.133722e+00, 'mix_pre_g': 1.496115e+00, 'mix_post_g': 6.409695e+01, 'ffn_pre_g': 1.468241e+00, 'ffn_post_g': 6.394861e+01}


def _to_microbatches(a, axis):
    t = _jnp.moveaxis(a, axis, 0)
    t = t.reshape((N_MICROBATCH, t.shape[0] // N_MICROBATCH) + t.shape[1:])
    return _jnp.moveaxis(t, 1, axis + 1)


def setup_inputs(seed: int = 0) -> dict:
    inp = _fwd_setup_inputs(seed)
    key = _jax.random.fold_in(_jax.random.key(seed), 7919)
    shape, _ = _output_shape()
    out = dict(inp)
    out["loss_target"] = _jax.random.normal(_jax.random.fold_in(key, 0), shape, _jnp.float32)
    for i, name in enumerate(TWIN_WEIGHTS):
        w = inp[name].astype(_jnp.float32)
        if MOMENT_SCALE is None:
            s = _jnp.sqrt(_jnp.mean(_jnp.square(w)) + 1e-30)
        else:
            s = MOMENT_SCALE[name]
        km, kv = _jax.random.split(_jax.random.fold_in(key, i + 1))
        out[name] = w
        out["m_" + name] = s * _jax.random.normal(km, w.shape, _jnp.float32)
        out["v_" + name] = (s * s) * _jax.random.uniform(kv, w.shape, _jnp.float32, 0.5, 1.5)
    if N_MICROBATCH > 1:
        for name, axis in PER_EXAMPLE_BATCH_AXIS.items():
            out[name] = _to_microbatches(out[name], axis)
    return {'x': out['x'], 'attn_w_qkv': out['attn_w_qkv'], 'attn_w_o': out['attn_w_o'], 'rg_w_in': out['rg_w_in'], 'rg_conv_w': out['rg_conv_w'], 'rg_conv_b': out['rg_conv_b'], 'rg_w_a': out['rg_w_a'], 'rg_b_a': out['rg_b_a'], 'rg_w_x': out['rg_w_x'], 'rg_b_x': out['rg_b_x'], 'rg_lambda': out['rg_lambda'], 'rg_w_out': out['rg_w_out'], 'ffn_w_up': out['ffn_w_up'], 'ffn_conv_w': out['ffn_conv_w'], 'ffn_conv_b': out['ffn_conv_b'], 'ffn_w_down': out['ffn_w_down'], 'mix_pre_g': out['mix_pre_g'], 'mix_post_g': out['mix_post_g'], 'ffn_pre_g': out['ffn_pre_g'], 'ffn_post_g': out['ffn_post_g'], 'loss_target': out['loss_target'], 'm_attn_w_qkv': out['m_attn_w_qkv'], 'm_attn_w_o': out['m_attn_w_o'], 'm_rg_w_in': out['m_rg_w_in'], 'm_rg_conv_w': out['m_rg_conv_w'], 'm_rg_conv_b': out['m_rg_conv_b'], 'm_rg_w_a': out['m_rg_w_a'], 'm_rg_b_a': out['m_rg_b_a'], 'm_rg_w_x': out['m_rg_w_x'], 'm_rg_b_x': out['m_rg_b_x'], 'm_rg_lambda': out['m_rg_lambda'], 'm_rg_w_out': out['m_rg_w_out'], 'm_ffn_w_up': out['m_ffn_w_up'], 'm_ffn_conv_w': out['m_ffn_conv_w'], 'm_ffn_conv_b': out['m_ffn_conv_b'], 'm_ffn_w_down': out['m_ffn_w_down'], 'm_mix_pre_g': out['m_mix_pre_g'], 'm_mix_post_g': out['m_mix_post_g'], 'm_ffn_pre_g': out['m_ffn_pre_g'], 'm_ffn_post_g': out['m_ffn_post_g'], 'v_attn_w_qkv': out['v_attn_w_qkv'], 'v_attn_w_o': out['v_attn_w_o'], 'v_rg_w_in': out['v_rg_w_in'], 'v_rg_conv_w': out['v_rg_conv_w'], 'v_rg_conv_b': out['v_rg_conv_b'], 'v_rg_w_a': out['v_rg_w_a'], 'v_rg_b_a': out['v_rg_b_a'], 'v_rg_w_x': out['v_rg_w_x'], 'v_rg_b_x': out['v_rg_b_x'], 'v_rg_lambda': out['v_rg_lambda'], 'v_rg_w_out': out['v_rg_w_out'], 'v_ffn_w_up': out['v_ffn_w_up'], 'v_ffn_conv_w': out['v_ffn_conv_w'], 'v_ffn_conv_b': out['v_ffn_conv_b'], 'v_ffn_w_down': out['v_ffn_w_down'], 'v_mix_pre_g': out['v_mix_pre_g'], 'v_mix_post_g': out['v_mix_post_g'], 'v_ffn_pre_g': out['v_ffn_pre_g'], 'v_ffn_post_g': out['v_ffn_post_g']}


def _loss(weights, diff, rest, loss_target):
    with _jax.named_scope("forward"):
        args = {**rest, TWIN_DIFF_INPUT: diff, **{k: w.astype(_WEIGHT_DTYPES[k]) for k, w in weights.items()}}
        y = _forward(args)
    with _jax.named_scope("loss_head"):
        err = _jnp.square(y.astype(_jnp.float32) - loss_target)
        return 0.5 * _jnp.sum(_jnp.mean(err, axis=-1)) if err.ndim else 0.5 * err


def _adamw(w, g, m, v):
    m = ADAM_B1 * m + (1.0 - ADAM_B1) * g
    v = ADAM_B2 * v + (1.0 - ADAM_B2) * _jnp.square(g)
    m_hat = m / (1.0 - ADAM_B1 ** ADAM_STEP)
    v_hat = v / (1.0 - ADAM_B2 ** ADAM_STEP)
    delta = -ADAM_LR * (m_hat / (_jnp.sqrt(v_hat) + ADAM_EPS) + ADAM_WD * w)
    return delta, m, v


def reference(x, attn_w_qkv, attn_w_o, rg_w_in, rg_conv_w, rg_conv_b, rg_w_a, rg_b_a, rg_w_x, rg_b_x, rg_lambda, rg_w_out, ffn_w_up, ffn_conv_w, ffn_conv_b, ffn_w_down, mix_pre_g, mix_post_g, ffn_pre_g, ffn_post_g, loss_target, m_attn_w_qkv, m_attn_w_o, m_rg_w_in, m_rg_conv_w, m_rg_conv_b, m_rg_w_a, m_rg_b_a, m_rg_w_x, m_rg_b_x, m_rg_lambda, m_rg_w_out, m_ffn_w_up, m_ffn_conv_w, m_ffn_conv_b, m_ffn_w_down, m_mix_pre_g, m_mix_post_g, m_ffn_pre_g, m_ffn_post_g, v_attn_w_qkv, v_attn_w_o, v_rg_w_in, v_rg_conv_w, v_rg_conv_b, v_rg_w_a, v_rg_b_a, v_rg_w_x, v_rg_b_x, v_rg_lambda, v_rg_w_out, v_ffn_w_up, v_ffn_conv_w, v_ffn_conv_b, v_ffn_w_down, v_mix_pre_g, v_mix_post_g, v_ffn_pre_g, v_ffn_post_g):
    given = dict(x=x, attn_w_qkv=attn_w_qkv, attn_w_o=attn_w_o, rg_w_in=rg_w_in, rg_conv_w=rg_conv_w, rg_conv_b=rg_conv_b, rg_w_a=rg_w_a, rg_b_a=rg_b_a, rg_w_x=rg_w_x, rg_b_x=rg_b_x, rg_lambda=rg_lambda, rg_w_out=rg_w_out, ffn_w_up=ffn_w_up, ffn_conv_w=ffn_conv_w, ffn_conv_b=ffn_conv_b, ffn_w_down=ffn_w_down, mix_pre_g=mix_pre_g, mix_post_g=mix_post_g, ffn_pre_g=ffn_pre_g, ffn_post_g=ffn_post_g, loss_target=loss_target, m_attn_w_qkv=m_attn_w_qkv, m_attn_w_o=m_attn_w_o, m_rg_w_in=m_rg_w_in, m_rg_conv_w=m_rg_conv_w, m_rg_conv_b=m_rg_conv_b, m_rg_w_a=m_rg_w_a, m_rg_b_a=m_rg_b_a, m_rg_w_x=m_rg_w_x, m_rg_b_x=m_rg_b_x, m_rg_lambda=m_rg_lambda, m_rg_w_out=m_rg_w_out, m_ffn_w_up=m_ffn_w_up, m_ffn_conv_w=m_ffn_conv_w, m_ffn_conv_b=m_ffn_conv_b, m_ffn_w_down=m_ffn_w_down, m_mix_pre_g=m_mix_pre_g, m_mix_post_g=m_mix_post_g, m_ffn_pre_g=m_ffn_pre_g, m_ffn_post_g=m_ffn_post_g, v_attn_w_qkv=v_attn_w_qkv, v_attn_w_o=v_attn_w_o, v_rg_w_in=v_rg_w_in, v_rg_conv_w=v_rg_conv_w, v_rg_conv_b=v_rg_conv_b, v_rg_w_a=v_rg_w_a, v_rg_b_a=v_rg_b_a, v_rg_w_x=v_rg_w_x, v_rg_b_x=v_rg_b_x, v_rg_lambda=v_rg_lambda, v_rg_w_out=v_rg_w_out, v_ffn_w_up=v_ffn_w_up, v_ffn_conv_w=v_ffn_conv_w, v_ffn_conv_b=v_ffn_conv_b, v_ffn_w_down=v_ffn_w_down, v_mix_pre_g=v_mix_pre_g, v_mix_post_g=v_mix_post_g, v_ffn_pre_g=v_ffn_pre_g, v_ffn_post_g=v_ffn_post_g)
    weights = {n: given[n] for n in TWIN_WEIGHTS}
    shared = {n: given[n] for n in SHARED_INPUTS}
    per_example = {n: given[n] for n in ['x']}
    grad_fn = _jax.value_and_grad(_loss, argnums=(0, 1))

    def one_microbatch(ex, loss_target):
        ex = dict(ex)
        diff = ex.pop(TWIN_DIFF_INPUT)
        return grad_fn(weights, diff, {**shared, **ex}, loss_target)

    if N_MICROBATCH == 1:
        loss, (grad_w, grad_x) = one_microbatch(per_example, given["loss_target"])
    else:
        def body(carry, xs):
            loss_sum, grad_sum = carry
            l_k, (gw_k, gx_k) = one_microbatch(xs[0], xs[1])
            with _jax.named_scope("update"):
                return (loss_sum + l_k, _jax.tree.map(_jnp.add, grad_sum, gw_k)), gx_k

        init = (_jnp.zeros((), _jnp.float32), _jax.tree.map(_jnp.zeros_like, weights))
        (loss, grad_w), grad_x = _jax.lax.scan(body, init, (per_example, given["loss_target"]))
    with _jax.named_scope("update"):
        delta_w, new_m, new_v = {}, {}, {}
        for n in TWIN_WEIGHTS:
            delta_w[n], new_m[n], new_v[n] = _adamw(weights[n], grad_w[n], given["m_" + n], given["v_" + n])
    return (loss, grad_x, *[grad_w[n] for n in TWIN_WEIGHTS], *[delta_w[n] for n in TWIN_WEIGHTS],
            *[new_m[n] for n in TWIN_WEIGHTS], *[new_v[n] for n in TWIN_WEIGHTS])
```

```python
import functools
import math

import jax
import jax.numpy as jnp
from jax import lax
from jax.experimental import pallas as pl
from jax.experimental.pallas import tpu as pltpu

D_MODEL = 1024
N_HEADS = 16
HEAD_DIM = 64
D_RNN = 1344
D_RNN_PAD = 1408
RG_BLOCKS = 16
RG_BLOCK_W = 84
D_FF = 2816
FFN_TILE = 256
NORM_EPS = 1e-6
RG_C = 8.0

ADAM_LR = 0.001
ADAM_B1 = 0.9
ADAM_B2 = 0.999
ADAM_EPS = 1e-08
ADAM_WD = 0.01
ADAM_STEP = 10

LANES = 128
SUBLANES = 8
VMEM_LIMIT = 56 * 1024 * 1024
FLAT_W = 1024
N_CHIPS = 4

F32 = jnp.float32
BF16 = jnp.bfloat16
MESH = pl.DeviceIdType.MESH


def _cp(*sem):
    return pltpu.CompilerParams(dimension_semantics=tuple(sem), vmem_limit_bytes=VMEM_LIMIT)


def _mm(a, b, mode, out_dtype, name, bm=1024, bn=1024, bk=1024, add=None):
    if mode == "nn":
        (M, K), (K2, N) = a.shape, b.shape
    elif mode == "nt":
        (M, K), (N, K2) = a.shape, b.shape
    else:
        (K, M), (K2, N) = a.shape, b.shape
    assert K == K2, (a.shape, b.shape, mode)
    bm, bn, bk = min(bm, M), min(bn, N), min(bk, K)
    assert M % bm == 0 and N % bn == 0 and K % bk == 0, (M, N, K, bm, bn, bk)
    nk = K // bk
    dims = {"nn": (((1,), (0,)), ((), ())), "nt": (((1,), (1,)), ((), ())), "tn": (((0,), (0,)), ((), ()))}[mode]
    has_add = add is not None

    def body(*refs):
        a_ref, b_ref = refs[0], refs[1]
        add_ref = refs[2] if has_add else None
        o_ref = refs[2 + has_add]
        p = lax.dot_general(a_ref[...].astype(BF16), b_ref[...].astype(BF16), dims, preferred_element_type=F32)
        if nk == 1:
            if has_add:
                p = p + add_ref[...]
            o_ref[...] = p.astype(o_ref.dtype)
        else:
            acc = refs[3 + has_add]
            k = pl.program_id(2)

            @pl.when(k == 0)
            def _():
                acc[...] = p

            @pl.when(k > 0)
            def _():
                acc[...] += p

            @pl.when(k == nk - 1)
            def _():
                r = acc[...]
                if has_add:
                    r = r + add_ref[...]
                o_ref[...] = r.astype(o_ref.dtype)

    if mode == "tn":
        a_spec = pl.BlockSpec((bk, bm), lambda i, j, k: (k, i))
    else:
        a_spec = pl.BlockSpec((bm, bk), lambda i, j, k: (i, k))
    if mode == "nt":
        b_spec = pl.BlockSpec((bn, bk), lambda i, j, k: (j, k))
    else:
        b_spec = pl.BlockSpec((bk, bn), lambda i, j, k: (k, j))
    o_spec = pl.BlockSpec((bm, bn), lambda i, j, k: (i, j))
    in_specs = [a_spec, b_spec] + ([o_spec] if has_add else [])
    args = (a, b) + ((add,) if has_add else ())
    return pl.pallas_call(
        body, name=name, grid=(M // bm, N // bn, nk),
        in_specs=in_specs, out_specs=o_spec,
        out_shape=jax.ShapeDtypeStruct((M, N), out_dtype),
        scratch_shapes=[pltpu.VMEM((bm, bn), F32)] if nk > 1 else [],
        compiler_params=_cp("parallel", "parallel", "arbitrary"),
    )(*args)


def _row_block(T, bm=512):
    bm = min(bm, T)
    assert T % bm == 0
    return bm


def _rmsnorm_fwd(x, g, name):
    T, D = x.shape
    bm = _row_block(T)

    def body(x_ref, g_ref, o_ref):
        xv = x_ref[...]
        inv = lax.rsqrt(jnp.mean(xv * xv, axis=-1, keepdims=True) + NORM_EPS)
        o_ref[...] = (xv * inv * g_ref[...]).astype(o_ref.dtype)

    return pl.pallas_call(
        body, name=name, grid=(T // bm,),
        in_specs=[pl.BlockSpec((bm, D), lambda i: (i, 0)), pl.BlockSpec((1, D), lambda i: (0, 0))],
        out_specs=pl.BlockSpec((bm, D), lambda i: (i, 0)),
        out_shape=jax.ShapeDtypeStruct((T, D), BF16),
        compiler_params=_cp("parallel"),
    )(x, g)


def _postnorm_residual(x, h, g, name):
    T, D = x.shape
    bm = _row_block(T)

    def body(x_ref, h_ref, g_ref, o_ref):
        hv = h_ref[...]
        inv = lax.rsqrt(jnp.mean(hv * hv, axis=-1, keepdims=True) + NORM_EPS)
        o_ref[...] = x_ref[...] + hv * inv * g_ref[...]

    return pl.pallas_call(
        body, name=name, grid=(T // bm,),
        in_specs=[pl.BlockSpec((bm, D), lambda i: (i, 0)), pl.BlockSpec((bm, D), lambda i: (i, 0)),
                  pl.BlockSpec((1, D), lambda i: (0, 0))],
        out_specs=pl.BlockSpec((bm, D), lambda i: (i, 0)),
        out_shape=jax.ShapeDtypeStruct((T, D), F32),
        compiler_params=_cp("parallel"),
    )(x, h, g)


def _postnorm_loss(x, h, g, target, name):
    T, D = x.shape
    bm = _row_block(T)

    def body(x_ref, h_ref, g_ref, t_ref, dy_ref, loss_ref):
        hv = h_ref[...]
        inv = lax.rsqrt(jnp.mean(hv * hv, axis=-1, keepdims=True) + NORM_EPS)
        err = x_ref[...] + hv * inv * g_ref[...] - t_ref[...]
        dy_ref[...] = err * (1.0 / D)
        part = jnp.sum(jnp.sum(err * err, axis=-1, keepdims=True), axis=0, keepdims=True) * (0.5 / D)

        @pl.when(pl.program_id(0) == 0)
        def _():
            loss_ref[...] = jnp.zeros_like(loss_ref)

        loss_ref[...] += jnp.broadcast_to(part, loss_ref.shape)

    row = pl.BlockSpec((bm, D), lambda i: (i, 0))
    return pl.pallas_call(
        body, name=name, grid=(T // bm,),
        in_specs=[row, row, pl.BlockSpec((1, D), lambda i: (0, 0)), row],
        out_specs=[row, pl.BlockSpec((1, LANES), lambda i: (0, 0))],
        out_shape=[jax.ShapeDtypeStruct((T, D), F32), jax.ShapeDtypeStruct((1, LANES), F32)],
        compiler_params=_cp("arbitrary"),
    )(x, h, g, target)


def _rmsnorm_bwd(dy, xin, g, name, res=None, out_dtype=F32):
    T, D = xin.shape
    bm = _row_block(T)
    has_res = res is not None

    def body(*refs):
        dy_ref, x_ref, g_ref = refs[:3]
        res_ref = refs[3] if has_res else None
        dx_ref, dg_ref = refs[3 + has_res], refs[4 + has_res]
        xv = x_ref[...]
        dyv = dy_ref[...].astype(F32)
        inv = lax.rsqrt(jnp.mean(xv * xv, axis=-1, keepdims=True) + NORM_EPS)
        xhat = xv * inv
        dyg = dyv * g_ref[...]
        c = jnp.mean(dyg * xhat, axis=-1, keepdims=True)
        dx = inv * (dyg - xhat * c)
        if has_res:
            dx = dx + res_ref[...]
        dx_ref[...] = dx.astype(dx_ref.dtype)

        @pl.when(pl.program_id(0) == 0)
        def _():
            dg_ref[...] = jnp.zeros_like(dg_ref)

        dg_ref[...] += jnp.sum(dyv * xhat, axis=0, keepdims=True)

    row = pl.BlockSpec((bm, D), lambda i: (i, 0))
    vec = pl.BlockSpec((1, D), lambda i: (0, 0))
    args = (dy, xin, g) + ((res,) if has_res else ())
    return pl.pallas_call(
        body, name=name, grid=(T // bm,),
        in_specs=[row, row, vec] + ([row] if has_res else []),
        out_specs=[row, vec],
        out_shape=[jax.ShapeDtypeStruct((T, D), out_dtype), jax.ShapeDtypeStruct((1, D), F32)],
        compiler_params=_cp("arbitrary"),
    )(*args)


_GELU_C = math.sqrt(2.0 / math.pi)


def _gelu(x):
    return 0.5 * x * (1.0 + jnp.tanh(_GELU_C * (x + 0.044715 * x * x * x)))


def _gelu_and_grad(x):
    t = jnp.tanh(_GELU_C * (x + 0.044715 * x * x * x))
    val = 0.5 * x * (1.0 + t)
    grad = 0.5 * (1.0 + t) + 0.5 * x * (1.0 - t * t) * _GELU_C * (1.0 + 3.0 * 0.044715 * x * x)
    return val, grad


def _sigmoid(x):
    return 1.0 / (1.0 + jnp.exp(-x))


def _shift_down(x, halo_prev, k):
    xe = jnp.concatenate([halo_prev, x], axis=0)
    return pltpu.roll(xe, k, 0)[SUBLANES:]


def _shift_up(x, halo_next, k):
    xe = jnp.concatenate([x, halo_next], axis=0)
    return pltpu.roll(xe, xe.shape[0] - k, 0)[:x.shape[0]]


def _halo_specs(T, bm, bc):
    r = bm // SUBLANES
    last = T // SUBLANES - 1
    prev = pl.BlockSpec((SUBLANES, bc), lambda j, i: (jnp.maximum(i * r - 1, 0), j))
    nxt = pl.BlockSpec((SUBLANES, bc), lambda j, i: (jnp.minimum((i + 1) * r, last), j))
    return prev, nxt


def _causal_conv(x, halo, w_ref, b_ref, width):
    out = x * w_ref[width - 1:width, :] + b_ref[...]
    for k in range(width - 1):
        out = out + _shift_down(x, halo, width - 1 - k) * w_ref[k:k + 1, :]
    return out


def _ffn_act_fwd(u_pre, conv_w, conv_b, name):
    T, C2 = u_pre.shape
    bm = _row_block(T, 256)
    bc = 2 * FFN_TILE
    nj = C2 // bc
    prev, _ = _halo_specs(T, bm, bc)

    def body(u_ref, h_ref, w_ref, b_ref, o_ref):
        i = pl.program_id(1)
        halo = jnp.where(i > 0, h_ref[...], 0.0)
        u = _causal_conv(u_ref[...], halo, w_ref, b_ref, 3)
        o_ref[...] = (_gelu(u[:, :FFN_TILE]) * u[:, FFN_TILE:]).astype(o_ref.dtype)

    return pl.pallas_call(
        body, name=name, grid=(nj, T // bm),
        in_specs=[pl.BlockSpec((bm, bc), lambda j, i: (i, j)), prev,
                  pl.BlockSpec((SUBLANES, bc), lambda j, i: (0, j)), pl.BlockSpec((1, bc), lambda j, i: (0, j))],
        out_specs=pl.BlockSpec((bm, FFN_TILE), lambda j, i: (i, j)),
        out_shape=jax.ShapeDtypeStruct((T, C2 // 2), BF16),
        compiler_params=_cp("parallel", "parallel"),
    )(u_pre, u_pre, conv_w, conv_b)


def _ffn_act_bwd(u_pre, conv_w, conv_b, d_act, name):
    T, C2 = u_pre.shape
    bm = _row_block(T, 256)
    bc = 2 * FFN_TILE
    nj = C2 // bc
    prev, _ = _halo_specs(T, bm, bc)

    def body(u_ref, h_ref, w_ref, b_ref, da_ref, du_ref, st_ref):
        i = pl.program_id(1)
        halo = jnp.where(i > 0, h_ref[...], 0.0)
        x = u_ref[...]
        xm1 = _shift_down(x, halo, 1)
        xm2 = _shift_down(x, halo, 2)
        u = xm2 * w_ref[0:1, :] + xm1 * w_ref[1:2, :] + x * w_ref[2:3, :] + b_ref[...]
        gv, gg = _gelu_and_grad(u[:, :FFN_TILE])
        da = da_ref[...]
        du = jnp.concatenate([da * u[:, FFN_TILE:] * gg, da * gv], axis=1)
        du_ref[...] = du
        rows = [jnp.sum(du * xm2, axis=0, keepdims=True), jnp.sum(du * xm1, axis=0, keepdims=True),
                jnp.sum(du * x, axis=0, keepdims=True), jnp.sum(du, axis=0, keepdims=True),
                jnp.zeros((4, bc), F32)]

        @pl.when(i == 0)
        def _():
            st_ref[...] = jnp.zeros_like(st_ref)

        st_ref[...] += jnp.concatenate(rows, axis=0)

    return pl.pallas_call(
        body, name=name, grid=(nj, T // bm),
        in_specs=[pl.BlockSpec((bm, bc), lambda j, i: (i, j)), prev,
                  pl.BlockSpec((SUBLANES, bc), lambda j, i: (0, j)), pl.BlockSpec((1, bc), lambda j, i: (0, j)),
                  pl.BlockSpec((bm, FFN_TILE), lambda j, i: (i, j))],
        out_specs=[pl.BlockSpec((bm, bc), lambda j, i: (i, j)), pl.BlockSpec((SUBLANES, bc), lambda j, i: (0, j))],
        out_shape=[jax.ShapeDtypeStruct((T, C2), F32), jax.ShapeDtypeStruct((SUBLANES, C2), F32)],
        compiler_params=_cp("parallel", "arbitrary"),
    )(u_pre, u_pre, conv_w, conv_b, d_act)


def _conv_transpose(dy, conv_w, width, name, bc):
    T, C = dy.shape
    bm = _row_block(T, 256)
    _, nxt = _halo_specs(T, bm, bc)
    nt = T // bm

    def body(y_ref, h_ref, w_ref, o_ref):
        i = pl.program_id(1)
        halo = jnp.where(i < nt - 1, h_ref[...], 0.0)
        y = y_ref[...]
        out = y * w_ref[width - 1:width, :]
        for k in range(width - 1):
            out = out + _shift_up(y, halo, width - 1 - k) * w_ref[k:k + 1, :]
        o_ref[...] = out.astype(o_ref.dtype)

    return pl.pallas_call(
        body, name=name, grid=(C // bc, nt),
        in_specs=[pl.BlockSpec((bm, bc), lambda j, i: (i, j)), nxt, pl.BlockSpec((SUBLANES, bc), lambda j, i: (0, j))],
        out_specs=pl.BlockSpec((bm, bc), lambda j, i: (i, j)),
        out_shape=jax.ShapeDtypeStruct((T, C), BF16),
        compiler_params=_cp("parallel", "parallel"),
    )(dy, dy, conv_w)


ATT_BK = 128


def _attn_consts():
    r = lax.broadcasted_iota(jnp.int32, (ATT_BK, ATT_BK), 0)
    c = lax.broadcasted_iota(jnp.int32, (ATT_BK, ATT_BK), 1)
    upper = jnp.where(r > c, 1.0, 0.0).astype(BF16)
    return jnp.concatenate([upper, upper], axis=0)


def _split_dot(x, uu):
    hi = x.astype(BF16)
    lo = (x - hi.astype(F32)).astype(BF16)
    return jnp.dot(jnp.concatenate([hi, lo], axis=1), uu, preferred_element_type=F32)


def _log_sigmoid(z):
    return jnp.minimum(z, 0.0) - jnp.log(1.0 + jnp.exp(-jnp.abs(z)))


def _attn_scores(qh, kblk, uu, causal):
    z = lax.dot_general(qh, kblk, (((1,), (1,)), ((), ())), preferred_element_type=F32)
    lb = _log_sigmoid(z)
    l1m = lb - z
    if causal is not None:
        l1m = jnp.where(causal, l1m, 0.0)
    cs = _split_dot(l1m, uu)
    return lb, l1m, cs


def _attn_fwd(qkv, name, tq=128):
    T = qkv.shape[0]
    tq = min(tq, T)
    nq = T // tq
    nsub = tq // ATT_BK
    nhp = N_HEADS // 2

    def body(q_ref, k_ref, v_ref, o_ref):
        qi = pl.program_id(1)
        lane = lax.broadcasted_iota(jnp.int32, (1, LANES), 1)
        first = lane < HEAD_DIM
        q2 = (q_ref[...].astype(F32) * HEAD_DIM ** -0.5).astype(BF16)
        zero = jnp.zeros_like(q2)
        qs = (jnp.where(first, q2, zero), jnp.where(first, zero, q2))
        uu = _attn_consts()
        row = lax.broadcasted_iota(jnp.int32, (tq, ATT_BK), 0)
        col = lax.broadcasted_iota(jnp.int32, (tq, ATT_BK), 1)

        def block(kb, carry, masked):
            off = pl.multiple_of(kb * ATT_BK, ATT_BK)
            kblk = k_ref[pl.ds(off, ATT_BK), :]
            vblk = v_ref[pl.ds(off, ATT_BK), :]
            causal = (kb * ATT_BK + col) < (qi * tq + row) if masked else None
            out = []
            for h in range(2):
                acc, run = carry[2 * h], carry[2 * h + 1]
                lb, l1m, cs = _attn_scores(qs[h], kblk, uu, causal)
                w = jnp.exp(lb + cs + run)
                if masked:
                    w = jnp.where(causal, w, 0.0)
                acc = acc + jnp.dot(w.astype(BF16), vblk, preferred_element_type=F32)
                run = run + jnp.sum(l1m, axis=1, keepdims=True)
                out += [acc, run]
            return tuple(out)

        carry = (jnp.zeros((tq, LANES), F32), jnp.zeros((tq, 1), F32)) * 2
        for d in reversed(range(nsub)):
            carry = block(qi * nsub + d, carry, True)
        carry = lax.fori_loop(0, qi * nsub, lambda i, c: block(qi * nsub - 1 - i, c, False), carry)
        o_ref[...] = jnp.where(first, carry[0], carry[2])

    return pl.pallas_call(
        body, name=name, grid=(nhp, nq),
        in_specs=[pl.BlockSpec((tq, LANES), lambda p, i: (i, p)),
                  pl.BlockSpec((T, LANES), lambda p, i: (0, nhp + p)),
                  pl.BlockSpec((T, LANES), lambda p, i: (0, 2 * nhp + p))],
        out_specs=pl.BlockSpec((tq, LANES), lambda p, i: (i, p)),
        out_shape=jax.ShapeDtypeStruct((T, D_MODEL), F32),
        compiler_params=_cp("parallel", "parallel"),
    )(qkv, qkv, qkv)


def _attn_bwd(qkv, o, d_o, name, tq=128):
    T = qkv.shape[0]
    tq = min(tq, T)
    nq = T // tq
    nsub = tq // ATT_BK
    nhp = N_HEADS // 2

    def body(q_ref, k_ref, v_ref, o_ref, do_ref, dq_ref, dk_ref, dv_ref, dk_acc, dv_acc):
        qi = pl.program_id(1)

        @pl.when(qi == 0)
        def _():
            dk_acc[...] = jnp.zeros_like(dk_acc)
            dv_acc[...] = jnp.zeros_like(dv_acc)

        lane = lax.broadcasted_iota(jnp.int32, (1, LANES), 1)
        first = lane < HEAD_DIM
        q2 = (q_ref[...].astype(F32) * HEAD_DIM ** -0.5).astype(BF16)
        zero = jnp.zeros_like(q2)
        qs = (jnp.where(first, q2, zero), jnp.where(first, zero, q2))
        do2 = do_ref[...].astype(BF16)
        dos = (jnp.where(first, do2, zero), jnp.where(first, zero, do2))
        prod = do2.astype(F32) * o_ref[...]
        deltas = (jnp.sum(jnp.where(first, prod, 0.0), axis=1, keepdims=True),
                  jnp.sum(jnp.where(first, 0.0, prod), axis=1, keepdims=True))
        uu = _attn_consts()
        row = lax.broadcasted_iota(jnp.int32, (tq, ATT_BK), 0)
        col = lax.broadcasted_iota(jnp.int32, (tq, ATT_BK), 1)
        tn = (((0,), (0,)), ((), ()))

        def block(kb, carry, masked):
            off = pl.multiple_of(kb * ATT_BK, ATT_BK)
            kblk = k_ref[pl.ds(off, ATT_BK), :]
            vblk = v_ref[pl.ds(off, ATT_BK), :]
            causal = (kb * ATT_BK + col) < (qi * tq + row) if masked else None
            out = []
            dk_part = jnp.zeros((ATT_BK, LANES), F32)
            dv_part = jnp.zeros((ATT_BK, LANES), F32)
            for h in range(2):
                dq, run, grun = carry[3 * h], carry[3 * h + 1], carry[3 * h + 2]
                lb, l1m, cs = _attn_scores(qs[h], kblk, uu, causal)
                w = jnp.exp(lb + cs + run)
                if masked:
                    w = jnp.where(causal, w, 0.0)
                wb = w.astype(BF16)
                dw = lax.dot_general(dos[h], vblk, (((1,), (1,)), ((), ())), preferred_element_type=F32)
                g = wb.astype(F32) * dw
                gpre = deltas[h] - grun - _split_dot(g, uu) - g
                dz = g - jnp.exp(lb) * (g + gpre)
                if masked:
                    dz = jnp.where(causal, dz, 0.0)
                dzb = dz.astype(BF16)
                dq = dq + jnp.dot(dzb, kblk, preferred_element_type=F32)
                dk_part = dk_part + lax.dot_general(dzb, qs[h], tn, preferred_element_type=F32)
                dv_part = dv_part + lax.dot_general(wb, dos[h], tn, preferred_element_type=F32)
                run = run + jnp.sum(l1m, axis=1, keepdims=True)
                grun = grun + jnp.sum(g, axis=1, keepdims=True)
                out += [dq, run, grun]
            dk_acc[pl.ds(off, ATT_BK), :] += dk_part
            dv_acc[pl.ds(off, ATT_BK), :] += dv_part
            return tuple(out)

        carry = (jnp.zeros((tq, LANES), F32), jnp.zeros((tq, 1), F32), jnp.zeros((tq, 1), F32)) * 2
        for d in reversed(range(nsub)):
            carry = block(qi * nsub + d, carry, True)
        carry = lax.fori_loop(0, qi * nsub, lambda i, c: block(qi * nsub - 1 - i, c, False), carry)
        dq_ref[...] = (jnp.where(first, carry[0], carry[3]) * HEAD_DIM ** -0.5).astype(dq_ref.dtype)

        @pl.when(qi == nq - 1)
        def _():
            dk_ref[...] = dk_acc[...].astype(dk_ref.dtype)
            dv_ref[...] = dv_acc[...].astype(dv_ref.dtype)

    qspec = pl.BlockSpec((tq, LANES), lambda p, i: (i, p))
    full = pl.BlockSpec((T, LANES), lambda p, i: (0, p))
    return pl.pallas_call(
        body, name=name, grid=(nhp, nq),
        in_specs=[qspec, pl.BlockSpec((T, LANES), lambda p, i: (0, nhp + p)),
                  pl.BlockSpec((T, LANES), lambda p, i: (0, 2 * nhp + p)), qspec, qspec],
        out_specs=[qspec, full, full],
        out_shape=[jax.ShapeDtypeStruct((T, D_MODEL), BF16)] * 3,
        scratch_shapes=[pltpu.VMEM((T, LANES), F32), pltpu.VMEM((T, LANES), F32)],
        compiler_params=_cp("parallel", "arbitrary"),
    )(qkv, qkv, qkv, o, d_o)


def _rg_conv_fwd(x, conv_w, conv_b, name):
    T, C = x.shape
    bm = _row_block(T, 256)
    bc = C
    prev, _ = _halo_specs(T, bm, bc)

    def body(x_ref, h_ref, w_ref, b_ref, o_ref):
        halo = jnp.where(pl.program_id(1) > 0, h_ref[...], 0.0)
        o_ref[...] = _causal_conv(x_ref[...], halo, w_ref, b_ref, 4)

    return pl.pallas_call(
        body, name=name, grid=(C // bc, T // bm),
        in_specs=[pl.BlockSpec((bm, bc), lambda j, i: (i, j)), prev,
                  pl.BlockSpec((SUBLANES, bc), lambda j, i: (0, j)), pl.BlockSpec((1, bc), lambda j, i: (0, j))],
        out_specs=pl.BlockSpec((bm, bc), lambda j, i: (i, j)),
        out_shape=jax.ShapeDtypeStruct((T, C), F32),
        compiler_params=_cp("parallel", "parallel"),
    )(x, x, conv_w, conv_b)


def _rg_gate_math(ra, rx, rec, vec_ref):
    r = _sigmoid(ra + vec_ref[0:1, :])
    ig = _sigmoid(rx + vec_ref[1:2, :])
    lam = vec_ref[2:3, :]
    sp = jnp.maximum(-lam, 0.0) + jnp.log(1.0 + jnp.exp(-jnp.abs(lam)))
    log_a = -RG_C * r * sp
    a = jnp.exp(log_a)
    th = jnp.tanh(log_a)
    mult = jnp.sqrt(-2.0 * th / (1.0 - th))
    return r, ig, sp, a, mult


def _rg_gates_fwd(rax, rec, vecs, name):
    T, C = rec.shape
    bm = _row_block(T, 256)

    def body(ra_ref, rx_ref, rec_ref, vec_ref, a_ref, u_ref):
        recv = rec_ref[...]
        _, ig, _, a, mult = _rg_gate_math(ra_ref[...], rx_ref[...], recv, vec_ref)
        a_ref[...] = a
        u_ref[...] = mult * ig * recv

    blk = pl.BlockSpec((bm, C), lambda i: (i, 0))
    return pl.pallas_call(
        body, name=name, grid=(T // bm,),
        in_specs=[blk, pl.BlockSpec((bm, C), lambda i: (i, 1)), blk, pl.BlockSpec((SUBLANES, C), lambda i: (0, 0))],
        out_specs=[blk, blk],
        out_shape=[jax.ShapeDtypeStruct((T, C), F32)] * 2,
        compiler_params=_cp("parallel"),
    )(rax, rax, rec, vecs)


def _linear_scan(a, b, name, reverse=False, shift_coef=False):
    T, C = a.shape
    bm = _row_block(T, 256)
    bc = LANES
    nt = T // bm
    _, nxt = _halo_specs(T, bm, bc)
    steps = [1 << s for s in range(int(math.log2(bm)))]
    assert 1 << len(steps) == bm

    def body(*refs):
        if shift_coef:
            a_ref, an_ref, b_ref, h_ref, carry = refs
        else:
            a_ref, b_ref, h_ref, carry = refs
        i = pl.program_id(1)

        @pl.when(i == 0)
        def _():
            carry[...] = jnp.zeros_like(carry)

        coef = a_ref[...]
        if shift_coef:
            coef = _shift_up(coef, an_ref[...], 1)
        val = b_ref[...]
        row = lax.broadcasted_iota(jnp.int32, (bm, bc), 0)
        for s in steps:
            if reverse:
                keep = row < bm - s
                cs, vs = pltpu.roll(coef, bm - s, 0), pltpu.roll(val, bm - s, 0)
            else:
                keep = row >= s
                cs, vs = pltpu.roll(coef, s, 0), pltpu.roll(val, s, 0)
            val = jnp.where(keep, coef * vs + val, val)
            coef = jnp.where(keep, coef * cs, coef)
        h = val + coef * carry[0:1, :]
        h_ref[...] = h
        edge = h[0:1, :] if reverse else h[bm - 1:bm, :]
        carry[...] = jnp.broadcast_to(edge, carry.shape)

    if reverse:
        blk = pl.BlockSpec((bm, bc), lambda j, i: (nt - 1 - i, j))
        r = bm // SUBLANES
        last = T // SUBLANES - 1
        nxt = pl.BlockSpec((SUBLANES, bc), lambda j, i: (jnp.minimum((nt - i) * r, last), j))
    else:
        blk = pl.BlockSpec((bm, bc), lambda j, i: (i, j))
    in_specs = [blk] + ([nxt] if shift_coef else []) + [blk]
    args = (a,) + ((a,) if shift_coef else ()) + (b,)
    return pl.pallas_call(
        body, name=name, grid=(C // bc, nt),
        in_specs=in_specs, out_specs=blk,
        out_shape=jax.ShapeDtypeStruct((T, C), F32),
        scratch_shapes=[pltpu.VMEM((SUBLANES, bc), F32)],
        compiler_params=_cp("parallel", "arbitrary"),
    )(*args)


def _rg_out_fwd(gate_branch, h, name):
    T, C = h.shape
    bm = _row_block(T, 256)

    def body(g_ref, h_ref, o_ref):
        o_ref[...] = (_gelu(g_ref[...]) * h_ref[...]).astype(o_ref.dtype)

    blk = pl.BlockSpec((bm, C), lambda i: (i, 0))
    return pl.pallas_call(
        body, name=name, grid=(T // bm,), in_specs=[blk, blk], out_specs=blk,
        out_shape=jax.ShapeDtypeStruct((T, C), BF16), compiler_params=_cp("parallel"),
    )(gate_branch, h)


def _rg_out_bwd(d_gh, gate_branch, h, name):
    T, C = h.shape
    bm = _row_block(T, 256)

    def body(d_ref, g_ref, h_ref, dg_ref, dh_ref):
        gv, gg = _gelu_and_grad(g_ref[...])
        d = d_ref[...]
        dg_ref[...] = (d * h_ref[...] * gg).astype(dg_ref.dtype)
        dh_ref[...] = d * gv

    blk = pl.BlockSpec((bm, C), lambda i: (i, 0))
    return pl.pallas_call(
        body, name=name, grid=(T // bm,), in_specs=[blk, blk, blk], out_specs=[blk, blk],
        out_shape=[jax.ShapeDtypeStruct((T, C), BF16), jax.ShapeDtypeStruct((T, C), F32)],
        compiler_params=_cp("parallel"),
    )(d_gh, gate_branch, h)


def _rg_gates_bwd(dhs, h, rax, rec, vecs, name):
    T, C = rec.shape
    bm = _row_block(T, 256)
    bc = LANES
    prev, _ = _halo_specs(T, bm, bc)
    nc = C // bc

    def body(d_ref, h_ref, hh_ref, ra_ref, rx_ref, rec_ref, vec_ref, dra_ref, drx_ref, drec_ref, st_ref):
        i = pl.program_id(1)
        halo = jnp.where(i > 0, hh_ref[...], 0.0)
        recv = rec_ref[...]
        du = d_ref[...]
        r, ig, sp, a, mult = _rg_gate_math(ra_ref[...], rx_ref[...], recv, vec_ref)
        da = du * _shift_down(h_ref[...], halo, 1)
        d_i = du * mult * recv
        drec_ref[...] = du * mult * ig
        d_mult = du * ig * recv
        d_log_a = da * a - d_mult * (a * a) / mult
        d_ra = d_log_a * (-RG_C) * sp * r * (1.0 - r)
        d_rx = d_i * ig * (1.0 - ig)
        dra_ref[...] = d_ra.astype(dra_ref.dtype)
        drx_ref[...] = d_rx.astype(drx_ref.dtype)
        lam = vec_ref[2:3, :]
        d_sp = jnp.sum(d_log_a * (-RG_C) * r, axis=0, keepdims=True)
        rows = [jnp.sum(d_ra, axis=0, keepdims=True), jnp.sum(d_rx, axis=0, keepdims=True),
                d_sp * (_sigmoid(lam) - 1.0), jnp.zeros((5, bc), F32)]

        @pl.when(i == 0)
        def _():
            st_ref[...] = jnp.zeros_like(st_ref)

        st_ref[...] += jnp.concatenate(rows, axis=0)

    blk = pl.BlockSpec((bm, bc), lambda j, i: (i, j))
    blk_x = pl.BlockSpec((bm, bc), lambda j, i: (i, nc + j))
    vec = pl.BlockSpec((SUBLANES, bc), lambda j, i: (0, j))
    d_ra, d_rx, d_rec, stats = pl.pallas_call(
        body, name=name, grid=(nc, T // bm),
        in_specs=[blk, blk, prev, blk, blk_x, blk, vec],
        out_specs=[blk, blk, blk, vec],
        out_shape=[jax.ShapeDtypeStruct((T, C), BF16), jax.ShapeDtypeStruct((T, C), BF16),
                   jax.ShapeDtypeStruct((T, C), F32), jax.ShapeDtypeStruct((SUBLANES, C), F32)],
        compiler_params=_cp("parallel", "arbitrary"),
    )(dhs, h, h, rax, rax, rec, vecs)
    return d_ra, d_rx, d_rec, stats


def _rg_conv_bwd_stats(d_rec, x, name):
    T, C = x.shape
    bm = _row_block(T, 256)
    bc = LANES
    prev, _ = _halo_specs(T, bm, bc)

    def body(d_ref, x_ref, h_ref, st_ref):
        i = pl.program_id(1)
        halo = jnp.where(i > 0, h_ref[...], 0.0)
        d = d_ref[...]
        xv = x_ref[...]
        rows = [jnp.sum(d * _shift_down(xv, halo, 3 - k), axis=0, keepdims=True) for k in range(3)]
        rows += [jnp.sum(d * xv, axis=0, keepdims=True), jnp.sum(d, axis=0, keepdims=True), jnp.zeros((3, bc), F32)]

        @pl.when(i == 0)
        def _():
            st_ref[...] = jnp.zeros_like(st_ref)

        st_ref[...] += jnp.concatenate(rows, axis=0)

    blk = pl.BlockSpec((bm, bc), lambda j, i: (i, j))
    vec = pl.BlockSpec((SUBLANES, bc), lambda j, i: (0, j))
    return pl.pallas_call(
        body, name=name, grid=(C // bc, T // bm),
        in_specs=[blk, blk, prev], out_specs=vec,
        out_shape=jax.ShapeDtypeStruct((SUBLANES, C), F32),
        compiler_params=_cp("parallel", "arbitrary"),
    )(d_rec, x, x)


def _adamw(w, g, m, v, name):
    R, W = w.shape
    bm = R
    for cand in (512, 256, 128, 64, 32, 16, 8):
        if R % cand == 0:
            bm = cand
            break
    bc1 = 1.0 - ADAM_B1 ** ADAM_STEP
    bc2 = 1.0 - ADAM_B2 ** ADAM_STEP

    def body(w_ref, g_ref, m_ref, v_ref, d_ref, nm_ref, nv_ref):
        gv = g_ref[...]
        nm = ADAM_B1 * m_ref[...] + (1.0 - ADAM_B1) * gv
        nv = ADAM_B2 * v_ref[...] + (1.0 - ADAM_B2) * (gv * gv)
        m_hat = nm / bc1
        v_hat = nv / bc2
        d_ref[...] = -ADAM_LR * (m_hat / (jnp.sqrt(v_hat) + ADAM_EPS) + ADAM_WD * w_ref[...])
        nm_ref[...] = nm
        nv_ref[...] = nv

    blk = pl.BlockSpec((bm, W), lambda i: (i, 0))
    return pl.pallas_call(
        body, name=name, grid=(R // bm,), in_specs=[blk] * 4, out_specs=[blk] * 3,
        out_shape=[jax.ShapeDtypeStruct((R, W), F32)] * 3, compiler_params=_cp("parallel"),
    )(w, g, m, v)


_ANY = pl.BlockSpec(memory_space=pl.ANY)


def _me():
    return lax.axis_index("x"), lax.axis_index("y"), lax.axis_index("c")


def _other_chips(x, y):
    return [(1 - x, y), (x, 1 - y), (1 - x, 1 - y)]


def _all_gather_chips(shard, name):
    R, W = shard.shape
    Rh = R // 2
    assert Rh * 2 == R and Rh % 16 == 0

    def body(x_ref, out_ref, send_sems, recv_sems, local_sem):
        x, y, c = _me()
        s_me = 2 * x + y
        chips = _other_chips(x, y)
        half = pl.ds(pl.multiple_of(c * Rh, 16), Rh)
        other_half = pl.ds(pl.multiple_of((1 - c) * Rh, 16), Rh)

        def copy(k, src, dst, to):
            return pltpu.make_async_remote_copy(src_ref=src, dst_ref=dst, send_sem=send_sems.at[k],
                                                recv_sem=recv_sems.at[k], device_id=to, device_id_type=MESH)

        mine = pltpu.make_async_copy(x_ref, out_ref.at[s_me], local_sem)
        mine.start()
        first = [copy(j, x_ref.at[half], out_ref.at[s_me, half], (*chip, c)) for j, chip in enumerate(chips)]
        for cp in first:
            cp.start()
        passed = []
        for j, (cx, cy) in enumerate(chips):
            landed = out_ref.at[2 * cx + cy, half]
            copy(j, landed, landed, (cx, cy, c)).wait_recv()
            fwd = copy(3 + j, landed, landed, (x, y, 1 - c))
            fwd.start()
            passed.append(fwd)
        for j, (cx, cy) in enumerate(chips):
            landed = out_ref.at[2 * cx + cy, other_half]
            copy(3 + j, landed, landed, (x, y, 1 - c)).wait_recv()
        for cp in first + passed:
            cp.wait_send()
        mine.wait()

    return pl.pallas_call(
        body, name=name, in_specs=[_ANY], out_specs=_ANY,
        out_shape=jax.ShapeDtypeStruct((N_CHIPS, R, W), shard.dtype),
        scratch_shapes=[pltpu.SemaphoreType.DMA((6,)), pltpu.SemaphoreType.DMA((6,)), pltpu.SemaphoreType.DMA],
    )(shard)


def _pair_send_other_halves(g, name):
    n, R, W = g.shape
    Rh = R // 2
    assert Rh % 8 == 0

    def body(g_ref, land_ref, send_sem, recv_sem):
        x, y, c = _me()
        src = g_ref.at[:, pl.ds(pl.multiple_of((1 - c) * Rh, 8), Rh), :]
        cp = pltpu.make_async_remote_copy(src_ref=src, dst_ref=land_ref, send_sem=send_sem, recv_sem=recv_sem,
                                          device_id=(x, y, 1 - c), device_id_type=MESH)
        cp.start()
        cp.wait()

    return pl.pallas_call(
        body, name=name, in_specs=[_ANY], out_specs=_ANY,
        out_shape=jax.ShapeDtypeStruct((n, Rh, W), g.dtype),
        scratch_shapes=[pltpu.SemaphoreType.DMA, pltpu.SemaphoreType.DMA],
    )(g)


def _add_own_half(g, landed, name):
    n, R, W = g.shape
    Rh = R // 2
    bm = 8
    for cand in (512, 256, 128, 64, 32, 16, 8):
        if Rh % cand == 0:
            bm = cand
            break
    nb = Rh // bm
    c_arr = lax.axis_index("c").astype(jnp.int32).reshape((1,))

    def body(c_ref, g_ref, l_ref, o_ref):
        o_ref[...] = g_ref[...] + l_ref[...]

    grid_spec = pltpu.PrefetchScalarGridSpec(
        num_scalar_prefetch=1, grid=(n, nb),
        in_specs=[pl.BlockSpec((1, bm, W), lambda s, i, c_ref: (s, c_ref[0] * nb + i, 0)),
                  pl.BlockSpec((1, bm, W), lambda s, i, c_ref: (s, i, 0))],
        out_specs=pl.BlockSpec((1, bm, W), lambda s, i, c_ref: (s, i, 0)))
    return pl.pallas_call(
        body, name=name, grid_spec=grid_spec, out_shape=jax.ShapeDtypeStruct((n, Rh, W), g.dtype),
        compiler_params=_cp("parallel", "parallel"),
    )(c_arr, g, landed)


def _scatter_to_chips(p, name):
    n, Rh, W = p.shape

    def body(p_ref, land_ref, send_sems, recv_sems, local_sem):
        x, y, c = _me()
        s_me = 2 * x + y
        chips = _other_chips(x, y)
        mine = pltpu.make_async_copy(p_ref.at[s_me], land_ref.at[s_me], local_sem)
        mine.start()
        sends = []
        for j, (cx, cy) in enumerate(chips):
            cp = pltpu.make_async_remote_copy(src_ref=p_ref.at[2 * cx + cy], dst_ref=land_ref.at[s_me],
                                              send_sem=send_sems.at[j], recv_sem=recv_sems.at[j],
                                              device_id=(cx, cy, c), device_id_type=MESH)
            cp.start()
            sends.append(cp)
        for j, (cx, cy) in enumerate(chips):
            slot = land_ref.at[2 * cx + cy]
            pltpu.make_async_remote_copy(src_ref=slot, dst_ref=slot, send_sem=send_sems.at[j], recv_sem=recv_sems.at[j],
                                         device_id=(cx, cy, c), device_id_type=MESH).wait_recv()
        for cp in sends:
            cp.wait_send()
        mine.wait()

    return pl.pallas_call(
        body, name=name, in_specs=[_ANY], out_specs=_ANY,
        out_shape=jax.ShapeDtypeStruct((n, Rh, W), p.dtype),
        scratch_shapes=[pltpu.SemaphoreType.DMA((3,)), pltpu.SemaphoreType.DMA((3,)), pltpu.SemaphoreType.DMA],
    )(p)


def _sum_slots(l, name):
    n, Rh, W = l.shape
    bm = 8
    for cand in (512, 256, 128, 64, 32, 16, 8):
        if Rh % cand == 0:
            bm = cand
            break

    def body(l_ref, o_ref):
        o_ref[...] = ((l_ref[0] + l_ref[1]) + l_ref[2]) + l_ref[3]

    return pl.pallas_call(
        body, name=name, grid=(Rh // bm,),
        in_specs=[pl.BlockSpec((n, bm, W), lambda i: (0, i, 0))], out_specs=pl.BlockSpec((bm, W), lambda i: (i, 0)),
        out_shape=jax.ShapeDtypeStruct((Rh, W), l.dtype), compiler_params=_cp("parallel"),
    )(l)


def _pair_gather(f, name):
    Rh, W = f.shape

    def body(f_ref, out_ref, send_sem, recv_sem, local_sem):
        x, y, c = _me()
        mine = pltpu.make_async_copy(f_ref, out_ref.at[c], local_sem)
        mine.start()
        cp = pltpu.make_async_remote_copy(src_ref=f_ref, dst_ref=out_ref.at[c], send_sem=send_sem, recv_sem=recv_sem,
                                          device_id=(x, y, 1 - c), device_id_type=MESH)
        cp.start()
        sib = out_ref.at[1 - c]
        pltpu.make_async_remote_copy(src_ref=sib, dst_ref=sib, send_sem=send_sem, recv_sem=recv_sem,
                                     device_id=(x, y, 1 - c), device_id_type=MESH).wait_recv()
        cp.wait_send()
        mine.wait()

    return pl.pallas_call(
        body, name=name, in_specs=[_ANY], out_specs=_ANY,
        out_shape=jax.ShapeDtypeStruct((2, Rh, W), f.dtype),
        scratch_shapes=[pltpu.SemaphoreType.DMA, pltpu.SemaphoreType.DMA, pltpu.SemaphoreType.DMA],
    )(f)


def _pack(arrays, rows_multiple, dtype):
    flat = jnp.concatenate([a.reshape(-1).astype(dtype) for a in arrays])
    n = flat.shape[0]
    rows = -(-n // FLAT_W)
    rows = -(-rows // rows_multiple) * rows_multiple
    return jnp.pad(flat, (0, rows * FLAT_W - n)).reshape(rows, FLAT_W)


def _unpack(flat, shapes):
    flat = flat.reshape(-1)
    out, off = [], 0
    for shp in shapes:
        n = math.prod(shp)
        out.append(flat[off:off + n].reshape(shp))
        off += n
    return out


def _pad_cols(a, width):
    return jnp.pad(a, [(0, 0)] * (a.ndim - 1) + [(0, width - a.shape[-1])])


def _rows8(a):
    return jnp.pad(a, ((0, SUBLANES - a.shape[0]), (0, 0)))


def _ffn_interleave(a):
    lead = a.shape[:-1]
    n = D_FF // FFN_TILE
    return jnp.swapaxes(a.reshape(lead + (2, n, FFN_TILE)), -3, -2).reshape(lead + (2 * D_FF,))


def _ffn_deinterleave(a):
    lead = a.shape[:-1]
    n = D_FF // FFN_TILE
    return jnp.swapaxes(a.reshape(lead + (n, 2, FFN_TILE)), -3, -2).reshape(lead + (2 * D_FF,))


def _block_diag_dense(w):
    eye = jnp.eye(RG_BLOCKS, dtype=w.dtype)
    dense = jnp.einsum("ncd,nm->ncmd", w, eye).reshape(D_RNN, D_RNN)
    return jnp.pad(dense, ((0, D_RNN_PAD - D_RNN), (0, D_RNN_PAD - D_RNN)))


def _block_diag_extract(dense):
    d = dense[:D_RNN, :D_RNN].reshape(RG_BLOCKS, RG_BLOCK_W, RG_BLOCKS, RG_BLOCK_W)
    return jnp.stack([d[n, :, n, :] for n in range(RG_BLOCKS)])


def _ffn_fwd(x, l, P, tag):
    n = _rmsnorm_fwd(x, P["ffn_pre_g"][l], f"{tag}_prenorm")
    u_pre = _mm(n, P["ffn_w_up"][l], "nn", F32, f"{tag}_up", bn=2 * FFN_TILE * 1)
    act = _ffn_act_fwd(u_pre, P["ffn_conv_w"][l], P["ffn_conv_b"][l], f"{tag}_act")
    f = _mm(act, P["ffn_w_down"][l], "nn", F32, f"{tag}_down", bk=D_FF)
    return f, (n, u_pre, act)


def _ffn_bwd(df, x, saved, l, P, tag):
    n, u_pre, act = saved
    g_down = _mm(act, df, "tn", F32, f"{tag}_dwdown", bm=D_FF // 2, bk=1024)
    d_act = _mm(df, P["ffn_w_down"][l], "nt", F32, f"{tag}_dact", bn=D_FF // 2)
    du, stats = _ffn_act_bwd(u_pre, P["ffn_conv_w"][l], P["ffn_conv_b"][l], d_act, f"{tag}_actbwd")
    du_pre = _conv_transpose(du, P["ffn_conv_w"][l], 3, f"{tag}_convT", 2 * FFN_TILE)
    g_up = _mm(n, du_pre, "tn", F32, f"{tag}_dwup", bn=2 * D_FF // 4, bk=1024)
    dn = _mm(du_pre, P["ffn_w_up"][l], "nt", F32, f"{tag}_dn", bk=2 * D_FF // 4)
    return dn, dict(ffn_w_down=g_down, ffn_w_up=g_up, ffn_conv_w=stats[0:3], ffn_conv_b=stats[3])


def _local_step(x, target, P):
    G = {}
    n0 = _rmsnorm_fwd(x, P["mix_pre_g"][0], "l0_prenorm")
    qkv = _mm(n0, P["attn_w_qkv"], "nn", BF16, "l0_qkv")
    o = _attn_fwd(qkv, "l0_attn")
    a0 = _mm(o, P["attn_w_o"], "nn", F32, "l0_wo")
    x1 = _postnorm_residual(x, a0, P["mix_post_g"][0], "l0_postnorm")
    f0, ffn0 = _ffn_fwd(x1, 0, P, "f0")
    x2 = _postnorm_residual(x1, f0, P["ffn_post_g"][0], "f0_postnorm")
    n1 = _rmsnorm_fwd(x2, P["mix_pre_g"][1], "l1_prenorm")
    gate_b = _mm(n1, P["rg_w_in_gate"], "nn", F32, "l1_in_gate", bn=D_RNN_PAD)
    rec_b = _mm(n1, P["rg_w_in_rec"], "nn", F32, "l1_in_rec", bn=D_RNN_PAD)
    rec = _rg_conv_fwd(rec_b, P["rg_conv_w"], P["rg_conv_b"], "l1_conv")
    rax = _mm(rec, P["rg_w_ax"], "nn", F32, "l1_gates_mm", bn=D_RNN_PAD, bk=D_RNN_PAD)
    a, u = _rg_gates_fwd(rax, rec, P["rg_vecs"], "l1_gates")
    h = _linear_scan(a, u, "l1_scan")
    gh = _rg_out_fwd(gate_b, h, "l1_gate_out")
    y1 = _mm(gh, P["rg_w_out"], "nn", F32, "l1_wout", bk=D_RNN_PAD)
    x3 = _postnorm_residual(x2, y1, P["mix_post_g"][1], "l1_postnorm")
    f1, ffn1 = _ffn_fwd(x3, 1, P, "f1")
    dx, loss = _postnorm_loss(x3, f1, P["ffn_post_g"][1], target, "loss")

    df, G["ffn_post_g1"] = _rmsnorm_bwd(dx, f1, P["ffn_post_g"][1], "f1_postnorm_bwd", out_dtype=BF16)
    dn, g = _ffn_bwd(df, x3, ffn1, 1, P, "f1")
    G.update({k + "1": v for k, v in g.items()})
    dx, G["ffn_pre_g1"] = _rmsnorm_bwd(dn, x3, P["ffn_pre_g"][1], "f1_prenorm_bwd", res=dx)
    dy, G["mix_post_g1"] = _rmsnorm_bwd(dx, y1, P["mix_post_g"][1], "l1_postnorm_bwd", out_dtype=BF16)
    G["rg_w_out"] = _mm(gh, dy, "tn", F32, "l1_dwout", bm=D_RNN_PAD, bk=1024)
    d_gh = _mm(dy, P["rg_w_out"], "nt", F32, "l1_dgh", bn=D_RNN_PAD)
    d_gate, d_h = _rg_out_bwd(d_gh, gate_b, h, "l1_gate_out_bwd")
    dhs = _linear_scan(a, d_h, "l1_scan_bwd", reverse=True, shift_coef=True)
    d_ra, d_rx, d_rec_direct, st = _rg_gates_bwd(dhs, h, rax, rec, P["rg_vecs"], "l1_gates_bwd")
    G["rg_b_a"], G["rg_b_x"], G["rg_lambda"] = st[0], st[1], st[2]
    d_rax = jnp.concatenate([d_ra, d_rx], axis=1)
    G["rg_w_ax"] = _mm(rec, d_rax, "tn", F32, "l1_dwax", bm=D_RNN_PAD, bn=D_RNN_PAD, bk=1024)
    d_rec = _mm(d_rax, P["rg_w_ax"], "nt", F32, "l1_drec", bm=512, bn=D_RNN_PAD, bk=2 * D_RNN_PAD, add=d_rec_direct)
    st = _rg_conv_bwd_stats(d_rec, rec_b, "l1_conv_stats")
    G["rg_conv_w"], G["rg_conv_b"] = st[0:4], st[4]
    d_rec_b = _conv_transpose(d_rec, P["rg_conv_w"], 4, "l1_convT", D_RNN_PAD)
    G["rg_w_in_gate"] = _mm(n1, d_gate, "tn", F32, "l1_dwin_gate", bn=D_RNN_PAD, bk=1024)
    G["rg_w_in_rec"] = _mm(n1, d_rec_b, "tn", F32, "l1_dwin_rec", bn=D_RNN_PAD, bk=1024)
    dn = _mm(d_gate, P["rg_w_in_gate"], "nt", F32, "l1_dn_gate", bk=D_RNN_PAD)
    dn = _mm(d_rec_b, P["rg_w_in_rec"], "nt", F32, "l1_dn_rec", bk=D_RNN_PAD, add=dn)
    dx, G["mix_pre_g1"] = _rmsnorm_bwd(dn, x2, P["mix_pre_g"][1], "l1_prenorm_bwd", res=dx)
    df, G["ffn_post_g0"] = _rmsnorm_bwd(dx, f0, P["ffn_post_g"][0], "f0_postnorm_bwd", out_dtype=BF16)
    dn, g = _ffn_bwd(df, x1, ffn0, 0, P, "f0")
    G.update({k + "0": v for k, v in g.items()})
    dx, G["ffn_pre_g0"] = _rmsnorm_bwd(dn, x1, P["ffn_pre_g"][0], "f0_prenorm_bwd", res=dx)
    da, G["mix_post_g0"] = _rmsnorm_bwd(dx, a0, P["mix_post_g"][0], "l0_postnorm_bwd", out_dtype=BF16)
    G["attn_w_o"] = _mm(o, da, "tn", F32, "l0_dwo", bk=1024)
    d_o = _mm(da, P["attn_w_o"], "nt", F32, "l0_do")
    dq, dk, dv = _attn_bwd(qkv, o, d_o, "l0_attn_bwd")
    d_qkv = jnp.concatenate([dq, dk, dv], axis=1)
    G["attn_w_qkv"] = _mm(n0, d_qkv, "tn", F32, "l0_dwqkv", bk=1024)
    dn = _mm(d_qkv, P["attn_w_qkv"], "nt", F32, "l0_dn")
    dx, G["mix_pre_g0"] = _rmsnorm_bwd(dn, x, P["mix_pre_g"][0], "l0_prenorm_bwd", res=dx)
    return loss[0, 0], dx, G


_BIG = [("attn_w_qkv", 2), ("attn_w_o", 1), ("rg_w_in", 2), ("rg_w_out", 1), ("ffn_w_up", 2), ("ffn_w_down", 1)]
_SMALL_SHARDED = [("rg_conv_w", 2), ("rg_conv_b", 1), ("rg_b_a", 1), ("rg_b_x", 1), ("rg_lambda", 1), ("ffn_conv_w", 2)]
_REPLICATED = ["rg_w_a", "rg_w_x", "ffn_conv_b", "mix_pre_g", "mix_post_g", "ffn_pre_g", "ffn_post_g"]
_ORDER = ["attn_w_qkv", "attn_w_o", "rg_w_in", "rg_conv_w", "rg_conv_b", "rg_w_a", "rg_b_a", "rg_w_x", "rg_b_x",
          "rg_lambda", "rg_w_out", "ffn_w_up", "ffn_conv_w", "ffn_conv_b", "ffn_w_down", "mix_pre_g", "mix_post_g",
          "ffn_pre_g", "ffn_post_g"]


def _join(gathered, shapes, axes):
    per_chip = [_unpack(gathered[s], shapes) for s in range(N_CHIPS)]
    return [jnp.concatenate([per_chip[s][i] for s in range(N_CHIPS)], axis=ax) for i, ax in enumerate(axes)]


def _split(full, axis):
    return jnp.split(full, N_CHIPS, axis=axis)


def _gather_params(W):
    big_shapes = [W[n].shape for n, _ in _BIG]
    big = _all_gather_chips(_pack([W[n] for n, _ in _BIG], 32, BF16), "gather_weights")
    small_shapes = [W[n].shape for n, _ in _SMALL_SHARDED]
    small = _all_gather_chips(_pack([W[n] for n, _ in _SMALL_SHARDED], 32, F32), "gather_small")
    full = dict(zip([n for n, _ in _BIG], _join(big, big_shapes, [ax for _, ax in _BIG])))
    full.update(zip([n for n, _ in _SMALL_SHARDED], _join(small, small_shapes, [ax for _, ax in _SMALL_SHARDED])))
    for n in _REPLICATED:
        full[n] = W[n]
    return full


def _layout_params(full):
    P = {}
    P["attn_w_qkv"] = full["attn_w_qkv"][0]
    P["attn_w_o"] = full["attn_w_o"][0]
    w_in = full["rg_w_in"][0]
    P["rg_w_in_gate"] = _pad_cols(w_in[:, :D_RNN], D_RNN_PAD)
    P["rg_w_in_rec"] = _pad_cols(w_in[:, D_RNN:], D_RNN_PAD)
    P["rg_w_out"] = jnp.pad(full["rg_w_out"][0], ((0, D_RNN_PAD - D_RNN), (0, 0)))
    P["rg_conv_w"] = _rows8(_pad_cols(full["rg_conv_w"][0], D_RNN_PAD))
    P["rg_conv_b"] = _pad_cols(full["rg_conv_b"], D_RNN_PAD)
    P["rg_vecs"] = _rows8(_pad_cols(jnp.concatenate([full["rg_b_a"], full["rg_b_x"], full["rg_lambda"]], axis=0), D_RNN_PAD))
    P["rg_w_ax"] = jnp.concatenate([_block_diag_dense(full["rg_w_a"][0]), _block_diag_dense(full["rg_w_x"][0])], axis=1).astype(BF16)
    P["ffn_w_up"] = [_ffn_interleave(full["ffn_w_up"][l]) for l in range(2)]
    P["ffn_conv_w"] = [_rows8(_ffn_interleave(full["ffn_conv_w"][l])) for l in range(2)]
    P["ffn_conv_b"] = [_ffn_interleave(full["ffn_conv_b"][l:l + 1]) for l in range(2)]
    P["ffn_w_down"] = [full["ffn_w_down"][l] for l in range(2)]
    for n in ("mix_pre_g", "mix_post_g", "ffn_pre_g", "ffn_post_g"):
        P[n] = [full[n][l:l + 1] for l in range(2)]
    return P


def _assemble_grads(G):
    out = {}
    out["attn_w_qkv"] = G["attn_w_qkv"][None]
    out["attn_w_o"] = G["attn_w_o"][None]
    out["rg_w_in"] = jnp.concatenate([G["rg_w_in_gate"][:, :D_RNN], G["rg_w_in_rec"][:, :D_RNN]], axis=1)[None]
    out["rg_conv_w"] = G["rg_conv_w"][:, :D_RNN][None]
    out["rg_conv_b"] = G["rg_conv_b"][:D_RNN][None]
    out["rg_w_a"] = _block_diag_extract(G["rg_w_ax"][:, :D_RNN_PAD])[None]
    out["rg_w_x"] = _block_diag_extract(G["rg_w_ax"][:, D_RNN_PAD:])[None]
    out["rg_b_a"] = G["rg_b_a"][:D_RNN][None]
    out["rg_b_x"] = G["rg_b_x"][:D_RNN][None]
    out["rg_lambda"] = G["rg_lambda"][:D_RNN][None]
    out["rg_w_out"] = G["rg_w_out"][:D_RNN][None]
    out["ffn_w_up"] = jnp.stack([_ffn_deinterleave(G[f"ffn_w_up{l}"]) for l in range(2)])
    out["ffn_conv_w"] = jnp.stack([_ffn_deinterleave(G[f"ffn_conv_w{l}"]) for l in range(2)])
    out["ffn_conv_b"] = jnp.stack([_ffn_deinterleave(G[f"ffn_conv_b{l}"]) for l in range(2)])
    out["ffn_w_down"] = jnp.stack([G[f"ffn_w_down{l}"] for l in range(2)])
    for n in ("mix_pre_g", "mix_post_g", "ffn_pre_g", "ffn_post_g"):
        out[n] = jnp.concatenate([G[n + "0"], G[n + "1"]], axis=0)
    return out


def _reduce_grads(full_grads, W):
    sharded = _BIG + _SMALL_SHARDED
    rep_flat = jnp.concatenate([full_grads[n].reshape(-1) for n in _REPLICATED])
    n_rep = rep_flat.shape[0]
    piece = -(-n_rep // (N_CHIPS * FLAT_W)) * FLAT_W
    rep_flat = jnp.pad(rep_flat, (0, N_CHIPS * piece - n_rep)).reshape(N_CHIPS, piece)
    blocks = []
    for s in range(N_CHIPS):
        parts = [_split(full_grads[n], ax)[s] for n, ax in sharded] + [rep_flat[s]]
        blocks.append(_pack(parts, 32, F32))
    g = jnp.stack(blocks)
    landed = _pair_send_other_halves(g, "rs_pair_send")
    chip_sum = _add_own_half(g, landed, "rs_pair_add")
    slots = _scatter_to_chips(chip_sum, "rs_scatter")
    half = _sum_slots(slots, "rs_sum")
    mine = _pair_gather(half, "rs_pair_gather").reshape(-1, FLAT_W)
    shapes = [W[n].shape for n, _ in sharded] + [(piece,)]
    parts = _unpack(mine, shapes)
    out = dict(zip([n for n, _ in sharded], parts[:-1]))
    rep_rows = piece // FLAT_W
    rep_pad = -(-rep_rows // 32) * 32
    rep_piece = jnp.pad(parts[-1].reshape(rep_rows, FLAT_W), ((0, rep_pad - rep_rows), (0, 0)))
    rep_all = _all_gather_chips(rep_piece, "gather_replicated_grads")[:, :rep_rows].reshape(-1)
    out.update(zip(_REPLICATED, _unpack(rep_all, [W[n].shape for n in _REPLICATED])))
    return out


def _update(W, M, V, grads):
    delta, new_m, new_v = {}, {}, {}
    small = [n for n in _ORDER if n not in dict(_BIG)]
    for n, _ in _BIG:
        shp = W[n].shape
        two_d = (-1, shp[-1])
        d, m, v = _adamw(W[n].reshape(two_d), grads[n].reshape(two_d), M[n].reshape(two_d), V[n].reshape(two_d), f"adamw_{n}")
        delta[n], new_m[n], new_v[n] = d.reshape(shp), m.reshape(shp), v.reshape(shp)
    packed = [_pack([src[n] for n in small], 8, F32) for src in (W, grads, M, V)]
    outs = _adamw(*packed, "adamw_small")
    shapes = [W[n].shape for n in small]
    for tgt, flat in zip((delta, new_m, new_v), outs):
        tgt.update(zip(small, _unpack(flat, shapes)))
    return delta, new_m, new_v


def kernel(x, attn_w_qkv, attn_w_o, rg_w_in, rg_conv_w, rg_conv_b, rg_w_a, rg_b_a, rg_w_x, rg_b_x, rg_lambda, rg_w_out, ffn_w_up, ffn_conv_w, ffn_conv_b, ffn_w_down, mix_pre_g, mix_post_g, ffn_pre_g, ffn_post_g, loss_target, m_attn_w_qkv, m_attn_w_o, m_rg_w_in, m_rg_conv_w, m_rg_conv_b, m_rg_w_a, m_rg_b_a, m_rg_w_x, m_rg_b_x, m_rg_lambda, m_rg_w_out, m_ffn_w_up, m_ffn_conv_w, m_ffn_conv_b, m_ffn_w_down, m_mix_pre_g, m_mix_post_g, m_ffn_pre_g, m_ffn_post_g, v_attn_w_qkv, v_attn_w_o, v_rg_w_in, v_rg_conv_w, v_rg_conv_b, v_rg_w_a, v_rg_b_a, v_rg_w_x, v_rg_b_x, v_rg_lambda, v_rg_w_out, v_ffn_w_up, v_ffn_conv_w, v_ffn_conv_b, v_ffn_w_down, v_mix_pre_g, v_mix_post_g, v_ffn_pre_g, v_ffn_post_g):
    W = dict(attn_w_qkv=attn_w_qkv, attn_w_o=attn_w_o, rg_w_in=rg_w_in, rg_conv_w=rg_conv_w, rg_conv_b=rg_conv_b,
             rg_w_a=rg_w_a, rg_b_a=rg_b_a, rg_w_x=rg_w_x, rg_b_x=rg_b_x, rg_lambda=rg_lambda, rg_w_out=rg_w_out,
             ffn_w_up=ffn_w_up, ffn_conv_w=ffn_conv_w, ffn_conv_b=ffn_conv_b, ffn_w_down=ffn_w_down,
             mix_pre_g=mix_pre_g, mix_post_g=mix_post_g, ffn_pre_g=ffn_pre_g, ffn_post_g=ffn_post_g)
    M = dict(attn_w_qkv=m_attn_w_qkv, attn_w_o=m_attn_w_o, rg_w_in=m_rg_w_in, rg_conv_w=m_rg_conv_w, rg_conv_b=m_rg_conv_b,
             rg_w_a=m_rg_w_a, rg_b_a=m_rg_b_a, rg_w_x=m_rg_w_x, rg_b_x=m_rg_b_x, rg_lambda=m_rg_lambda, rg_w_out=m_rg_w_out,
             ffn_w_up=m_ffn_w_up, ffn_conv_w=m_ffn_conv_w, ffn_conv_b=m_ffn_conv_b, ffn_w_down=m_ffn_w_down,
             mix_pre_g=m_mix_pre_g, mix_post_g=m_mix_post_g, ffn_pre_g=m_ffn_pre_g, ffn_post_g=m_ffn_post_g)
    V = dict(attn_w_qkv=v_attn_w_qkv, attn_w_o=v_attn_w_o, rg_w_in=v_rg_w_in, rg_conv_w=v_rg_conv_w, rg_conv_b=v_rg_conv_b,
             rg_w_a=v_rg_w_a, rg_b_a=v_rg_b_a, rg_w_x=v_rg_w_x, rg_b_x=v_rg_b_x, rg_lambda=v_rg_lambda, rg_w_out=v_rg_w_out,
             ffn_w_up=v_ffn_w_up, ffn_conv_w=v_ffn_conv_w, ffn_conv_b=v_ffn_conv_b, ffn_w_down=v_ffn_w_down,
             mix_pre_g=v_mix_pre_g, mix_post_g=v_mix_post_g, ffn_pre_g=v_ffn_pre_g, ffn_post_g=v_ffn_post_g)
    P = _layout_params(_gather_params(W))
    loss_part, grad_x, G = _local_step(x[0], loss_target[0], P)
    loss = lax.psum(loss_part, ("x", "y", "c"))
    grads = _reduce_grads(_assemble_grads(G), W)
    delta, new_m, new_v = _update(W, M, V, grads)
    return (loss, grad_x[None], *[grads[n] for n in _ORDER], *[delta[n] for n in _ORDER],
            *[new_m[n] for n in _ORDER], *[new_v[n] for n in _ORDER])
```

```python
import functools
import math

import jax
import jax.numpy as jnp
from jax import lax
from jax.experimental import pallas as pl
from jax.experimental.pallas import tpu as pltpu

D_MODEL = 1024
N_HEADS = 16
HEAD_DIM = 64
D_RNN = 1344
D_RNN_PAD = 1408
RG_BLOCKS = 16
RG_BLOCK_W = 84
D_FF = 2816
FFN_TILE = 256
NORM_EPS = 1e-6
RG_C = 8.0

ADAM_LR = 0.001
ADAM_B1 = 0.9
ADAM_B2 = 0.999
ADAM_EPS = 1e-08
ADAM_WD = 0.01
ADAM_STEP = 10

LANES = 128
SUBLANES = 8
VMEM_LIMIT = 56 * 1024 * 1024
FLAT_W = 1024
N_CHIPS = 4

F32 = jnp.float32
BF16 = jnp.bfloat16
MESH = pl.DeviceIdType.MESH


def _cp(*sem):
    return pltpu.CompilerParams(dimension_semantics=tuple(sem), vmem_limit_bytes=VMEM_LIMIT)


def _mm(a, b, mode, out_dtype, name, bm=1024, bn=1024, bk=1024, add=None):
    if mode == "nn":
        (M, K), (K2, N) = a.shape, b.shape
    elif mode == "nt":
        (M, K), (N, K2) = a.shape, b.shape
    else:
        (K, M), (K2, N) = a.shape, b.shape
    assert K == K2, (a.shape, b.shape, mode)
    bm, bn, bk = min(bm, M), min(bn, N), min(bk, K)
    assert M % bm == 0 and N % bn == 0 and K % bk == 0, (M, N, K, bm, bn, bk)
    nk = K // bk
    dims = {"nn": (((1,), (0,)), ((), ())), "nt": (((1,), (1,)), ((), ())), "tn": (((0,), (0,)), ((), ()))}[mode]
    has_add = add is not None

    def body(*refs):
        a_ref, b_ref = refs[0], refs[1]
        add_ref = refs[2] if has_add else None
        o_ref = refs[2 + has_add]
        p = lax.dot_general(a_ref[...].astype(BF16), b_ref[...].astype(BF16), dims, preferred_element_type=F32)
        if nk == 1:
            if has_add:
                p = p + add_ref[...]
            o_ref[...] = p.astype(o_ref.dtype)
        else:
            acc = refs[3 + has_add]
            k = pl.program_id(2)

            @pl.when(k == 0)
            def _():
                acc[...] = p

            @pl.when(k > 0)
            def _():
                acc[...] += p

            @pl.when(k == nk - 1)
            def _():
                r = acc[...]
                if has_add:
                    r = r + add_ref[...]
                o_ref[...] = r.astype(o_ref.dtype)

    if mode == "tn":
        a_spec = pl.BlockSpec((bk, bm), lambda i, j, k: (k, i))
    else:
        a_spec = pl.BlockSpec((bm, bk), lambda i, j, k: (i, k))
    if mode == "nt":
        b_spec = pl.BlockSpec((bn, bk), lambda i, j, k: (j, k))
    else:
        b_spec = pl.BlockSpec((bk, bn), lambda i, j, k: (k, j))
    o_spec = pl.BlockSpec((bm, bn), lambda i, j, k: (i, j))
    in_specs = [a_spec, b_spec] + ([o_spec] if has_add else [])
    args = (a, b) + ((add,) if has_add else ())
    return pl.pallas_call(
        body, name=name, grid=(M // bm, N // bn, nk),
        in_specs=in_specs, out_specs=o_spec,
        out_shape=jax.ShapeDtypeStruct((M, N), out_dtype),
        scratch_shapes=[pltpu.VMEM((bm, bn), F32)] if nk > 1 else [],
        compiler_params=_cp("parallel", "parallel", "arbitrary"),
    )(*args)


def _row_block(T, bm=512):
    bm = min(bm, T)
    assert T % bm == 0
    return bm


def _rmsnorm_fwd(x, g, name):
    T, D = x.shape
    bm = _row_block(T)

    def body(x_ref, g_ref, o_ref):
        xv = x_ref[...]
        inv = lax.rsqrt(jnp.mean(xv * xv, axis=-1, keepdims=True) + NORM_EPS)
        o_ref[...] = (xv * inv * g_ref[...]).astype(o_ref.dtype)

    return pl.pallas_call(
        body, name=name, grid=(T // bm,),
        in_specs=[pl.BlockSpec((bm, D), lambda i: (i, 0)), pl.BlockSpec((1, D), lambda i: (0, 0))],
        out_specs=pl.BlockSpec((bm, D), lambda i: (i, 0)),
        out_shape=jax.ShapeDtypeStruct((T, D), BF16),
        compiler_params=_cp("parallel"),
    )(x, g)


def _postnorm_residual(x, h, g, name):
    T, D = x.shape
    bm = _row_block(T)

    def body(x_ref, h_ref, g_ref, o_ref):
        hv = h_ref[...]
        inv = lax.rsqrt(jnp.mean(hv * hv, axis=-1, keepdims=True) + NORM_EPS)
        o_ref[...] = x_ref[...] + hv * inv * g_ref[...]

    return pl.pallas_call(
        body, name=name, grid=(T // bm,),
        in_specs=[pl.BlockSpec((bm, D), lambda i: (i, 0)), pl.BlockSpec((bm, D), lambda i: (i, 0)),
                  pl.BlockSpec((1, D), lambda i: (0, 0))],
        out_specs=pl.BlockSpec((bm, D), lambda i: (i, 0)),
        out_shape=jax.ShapeDtypeStruct((T, D), F32),
        compiler_params=_cp("parallel"),
    )(x, h, g)


def _postnorm_loss(x, h, g, target, name):
    T, D = x.shape
    bm = _row_block(T)

    def body(x_ref, h_ref, g_ref, t_ref, dy_ref, loss_ref):
        hv = h_ref[...]
        inv = lax.rsqrt(jnp.mean(hv * hv, axis=-1, keepdims=True) + NORM_EPS)
        err = x_ref[...] + hv * inv * g_ref[...] - t_ref[...]
        dy_ref[...] = err * (1.0 / D)
        part = jnp.sum(jnp.sum(err * err, axis=-1, keepdims=True), axis=0, keepdims=True) * (0.5 / D)

        @pl.when(pl.program_id(0) == 0)
        def _():
            loss_ref[...] = jnp.zeros_like(loss_ref)

        loss_ref[...] += jnp.broadcast_to(part, loss_ref.shape)

    row = pl.BlockSpec((bm, D), lambda i: (i, 0))
    return pl.pallas_call(
        body, name=name, grid=(T // bm,),
        in_specs=[row, row, pl.BlockSpec((1, D), lambda i: (0, 0)), row],
        out_specs=[row, pl.BlockSpec((1, LANES), lambda i: (0, 0))],
        out_shape=[jax.ShapeDtypeStruct((T, D), F32), jax.ShapeDtypeStruct((1, LANES), F32)],
        compiler_params=_cp("arbitrary"),
    )(x, h, g, target)


def _rmsnorm_bwd(dy, xin, g, name, res=None, out_dtype=F32):
    T, D = xin.shape
    bm = _row_block(T)
    has_res = res is not None

    def body(*refs):
        dy_ref, x_ref, g_ref = refs[:3]
        res_ref = refs[3] if has_res else None
        dx_ref, dg_ref = refs[3 + has_res], refs[4 + has_res]
        xv = x_ref[...]
        dyv = dy_ref[...].astype(F32)
        inv = lax.rsqrt(jnp.mean(xv * xv, axis=-1, keepdims=True) + NORM_EPS)
        xhat = xv * inv
        dyg = dyv * g_ref[...]
        c = jnp.mean(dyg * xhat, axis=-1, keepdims=True)
        dx = inv * (dyg - xhat * c)
        if has_res:
            dx = dx + res_ref[...]
        dx_ref[...] = dx.astype(dx_ref.dtype)

        @pl.when(pl.program_id(0) == 0)
        def _():
            dg_ref[...] = jnp.zeros_like(dg_ref)

        dg_ref[...] += jnp.sum(dyv * xhat, axis=0, keepdims=True)

    row = pl.BlockSpec((bm, D), lambda i: (i, 0))
    vec = pl.BlockSpec((1, D), lambda i: (0, 0))
    args = (dy, xin, g) + ((res,) if has_res else ())
    return pl.pallas_call(
        body, name=name, grid=(T // bm,),
        in_specs=[row, row, vec] + ([row] if has_res else []),
        out_specs=[row, vec],
        out_shape=[jax.ShapeDtypeStruct((T, D), out_dtype), jax.ShapeDtypeStruct((1, D), F32)],
        compiler_params=_cp("arbitrary"),
    )(*args)


_GELU_C = math.sqrt(2.0 / math.pi)


def _gelu(x):
    return 0.5 * x * (1.0 + jnp.tanh(_GELU_C * (x + 0.044715 * x * x * x)))


def _gelu_and_grad(x):
    t = jnp.tanh(_GELU_C * (x + 0.044715 * x * x * x))
    val = 0.5 * x * (1.0 + t)
    grad = 0.5 * (1.0 + t) + 0.5 * x * (1.0 - t * t) * _GELU_C * (1.0 + 3.0 * 0.044715 * x * x)
    return val, grad


def _sigmoid(x):
    return 1.0 / (1.0 + jnp.exp(-x))


def _shift_down(x, halo_prev, k):
    xe = jnp.concatenate([halo_prev, x], axis=0)
    return pltpu.roll(xe, k, 0)[SUBLANES:]


def _shift_up(x, halo_next, k):
    xe = jnp.concatenate([x, halo_next], axis=0)
    return pltpu.roll(xe, xe.shape[0] - k, 0)[:x.shape[0]]


def _halo_specs(T, bm, bc):
    r = bm // SUBLANES
    last = T // SUBLANES - 1
    prev = pl.BlockSpec((SUBLANES, bc), lambda j, i: (jnp.maximum(i * r - 1, 0), j))
    nxt = pl.BlockSpec((SUBLANES, bc), lambda j, i: (jnp.minimum((i + 1) * r, last), j))
    return prev, nxt


def _causal_conv(x, halo, w_ref, b_ref, width):
    out = x * w_ref[width - 1:width, :] + b_ref[...]
    for k in range(width - 1):
        out = out + _shift_down(x, halo, width - 1 - k) * w_ref[k:k + 1, :]
    return out


def _ffn_act_fwd(u_pre, conv_w, conv_b, name):
    T, C2 = u_pre.shape
    bm = _row_block(T, 512)
    bc = 2 * FFN_TILE
    nj = C2 // bc
    prev, _ = _halo_specs(T, bm, bc)

    def body(u_ref, h_ref, w_ref, b_ref, o_ref):
        i = pl.program_id(1)
        halo = jnp.where(i > 0, h_ref[...], 0.0)
        u = _causal_conv(u_ref[...], halo, w_ref, b_ref, 3)
        o_ref[...] = (_gelu(u[:, :FFN_TILE]) * u[:, FFN_TILE:]).astype(o_ref.dtype)

    return pl.pallas_call(
        body, name=name, grid=(nj, T // bm),
        in_specs=[pl.BlockSpec((bm, bc), lambda j, i: (i, j)), prev,
                  pl.BlockSpec((SUBLANES, bc), lambda j, i: (0, j)), pl.BlockSpec((1, bc), lambda j, i: (0, j))],
        out_specs=pl.BlockSpec((bm, FFN_TILE), lambda j, i: (i, j)),
        out_shape=jax.ShapeDtypeStruct((T, C2 // 2), BF16),
        compiler_params=_cp("parallel", "parallel"),
    )(u_pre, u_pre, conv_w, conv_b)


def _ffn_act_bwd(u_pre, conv_w, conv_b, d_act, name):
    T, C2 = u_pre.shape
    bm = _row_block(T, 512)
    bc = 2 * FFN_TILE
    nj = C2 // bc
    prev, _ = _halo_specs(T, bm, bc)

    def body(u_ref, h_ref, w_ref, b_ref, da_ref, du_ref, st_ref):
        i = pl.program_id(1)
        halo = jnp.where(i > 0, h_ref[...], 0.0)
        x = u_ref[...]
        xm1 = _shift_down(x, halo, 1)
        xm2 = _shift_down(x, halo, 2)
        u = xm2 * w_ref[0:1, :] + xm1 * w_ref[1:2, :] + x * w_ref[2:3, :] + b_ref[...]
        gv, gg = _gelu_and_grad(u[:, :FFN_TILE])
        da = da_ref[...]
        du = jnp.concatenate([da * u[:, FFN_TILE:] * gg, da * gv], axis=1)
        du_ref[...] = du
        rows = [jnp.sum(du * xm2, axis=0, keepdims=True), jnp.sum(du * xm1, axis=0, keepdims=True),
                jnp.sum(du * x, axis=0, keepdims=True), jnp.sum(du, axis=0, keepdims=True),
                jnp.zeros((4, bc), F32)]

        @pl.when(i == 0)
        def _():
            st_ref[...] = jnp.zeros_like(st_ref)

        st_ref[...] += jnp.concatenate(rows, axis=0)

    return pl.pallas_call(
        body, name=name, grid=(nj, T // bm),
        in_specs=[pl.BlockSpec((bm, bc), lambda j, i: (i, j)), prev,
                  pl.BlockSpec((SUBLANES, bc), lambda j, i: (0, j)), pl.BlockSpec((1, bc), lambda j, i: (0, j)),
                  pl.BlockSpec((bm, FFN_TILE), lambda j, i: (i, j))],
        out_specs=[pl.BlockSpec((bm, bc), lambda j, i: (i, j)), pl.BlockSpec((SUBLANES, bc), lambda j, i: (0, j))],
        out_shape=[jax.ShapeDtypeStruct((T, C2), F32), jax.ShapeDtypeStruct((SUBLANES, C2), F32)],
        compiler_params=_cp("parallel", "arbitrary"),
    )(u_pre, u_pre, conv_w, conv_b, d_act)


def _conv_transpose(dy, conv_w, width, name, bc):
    T, C = dy.shape
    bm = _row_block(T, 512)
    _, nxt = _halo_specs(T, bm, bc)
    nt = T // bm

    def body(y_ref, h_ref, w_ref, o_ref):
        i = pl.program_id(1)
        halo = jnp.where(i < nt - 1, h_ref[...], 0.0)
        y = y_ref[...]
        out = y * w_ref[width - 1:width, :]
        for k in range(width - 1):
            out = out + _shift_up(y, halo, width - 1 - k) * w_ref[k:k + 1, :]
        o_ref[...] = out.astype(o_ref.dtype)

    return pl.pallas_call(
        body, name=name, grid=(C // bc, nt),
        in_specs=[pl.BlockSpec((bm, bc), lambda j, i: (i, j)), nxt, pl.BlockSpec((SUBLANES, bc), lambda j, i: (0, j))],
        out_specs=pl.BlockSpec((bm, bc), lambda j, i: (i, j)),
        out_shape=jax.ShapeDtypeStruct((T, C), BF16),
        compiler_params=_cp("parallel", "parallel"),
    )(dy, dy, conv_w)


ATT_BK = 128


def _split_dot(x, uu):
    hi = x.astype(BF16)
    lo = (x - hi.astype(F32)).astype(BF16)
    return jnp.dot(jnp.concatenate([hi, lo], axis=1), uu, preferred_element_type=F32)


def _log_sigmoid(z):
    return jnp.minimum(z, 0.0) - jnp.log(1.0 + jnp.exp(-jnp.abs(z)))


ATT_TQ = 256
ATT_KS = 512
ATT_SUB = ATT_KS // ATT_BK


def _suffix_consts():
    r = lax.broadcasted_iota(jnp.int32, (ATT_BK, 2 * ATT_BK), 0)
    c = lax.broadcasted_iota(jnp.int32, (ATT_BK, 2 * ATT_BK), 1)
    m = jnp.where((r > c) | (c >= ATT_BK), 1.0, 0.0).astype(BF16)
    return jnp.concatenate([m, m], axis=0)


def _suffix_sums(x, uu):
    ext = _split_dot(x, uu)
    return ext[:, :ATT_BK], ext[:, ATT_BK:]


def _masked_heads(x_f32, first):
    return (jnp.where(first, x_f32, 0.0).astype(BF16), jnp.where(first, 0.0, x_f32).astype(BF16))


def _sb_attn_fwd(qkv, name):
    T = qkv.shape[0]
    tq, ks = min(ATT_TQ, T), min(ATT_KS, T)
    assert T % tq == 0 and T % ks == 0 and ks % tq == 0
    nq = T // tq
    nsub = ks // ATT_BK
    nhp = N_HEADS // 2
    nt = (((1,), (1,)), ((), ()))

    def body(q_ref, k_ref, v_ref, o_ref):
        qi = pl.program_id(1)
        first = lax.broadcasted_iota(jnp.int32, (1, LANES), 1) < HEAD_DIM
        qs = _masked_heads(q_ref[...].astype(F32) * HEAD_DIM ** -0.5, first)
        uu = _suffix_consts()
        tpos = qi * tq + lax.broadcasted_iota(jnp.int32, (tq, ATT_BK), 0)
        col = lax.broadcasted_iota(jnp.int32, (tq, ATT_BK), 1)

        def block(kb, carry, masked):
            off = pl.multiple_of(kb * ks, ks)
            kblk = k_ref[pl.ds(off, ks), :]
            vblk = v_ref[pl.ds(off, ks), :]
            out = []
            for h in range(2):
                acc, run = carry[2 * h], carry[2 * h + 1]
                z = lax.dot_general(qs[h], kblk, nt, preferred_element_type=F32)
                lb = _log_sigmoid(z)
                l1m = lb - z
                ws = [None] * nsub
                for j in reversed(range(nsub)):
                    sl = slice(j * ATT_BK, (j + 1) * ATT_BK)
                    l1 = l1m[:, sl]
                    if masked:
                        causal = (off + j * ATT_BK + col) < tpos
                        l1 = jnp.where(causal, l1, 0.0)
                    cs, tot = _suffix_sums(l1, uu)
                    w = jnp.exp(lb[:, sl] + cs + run)
                    ws[j] = jnp.where(causal, w, 0.0) if masked else w
                    run = run + tot
                w = jnp.concatenate(ws, axis=1)
                acc = acc + jnp.dot(w.astype(BF16), vblk, preferred_element_type=F32)
                out += [acc, run]
            return tuple(out)

        nfull = (qi * tq) // ks
        carry = (jnp.zeros((tq, LANES), F32),) * 4
        carry = block(nfull, carry, True)
        carry = lax.fori_loop(0, nfull, lambda i, c: block(nfull - 1 - i, c, False), carry)
        o_ref[...] = jnp.where(first, carry[0], carry[2])

    return pl.pallas_call(
        body, name=name, grid=(nhp, nq),
        in_specs=[pl.BlockSpec((tq, LANES), lambda p, i: (i, p)),
                  pl.BlockSpec((T, LANES), lambda p, i: (0, nhp + p)),
                  pl.BlockSpec((T, LANES), lambda p, i: (0, 2 * nhp + p))],
        out_specs=pl.BlockSpec((tq, LANES), lambda p, i: (i, p)),
        out_shape=jax.ShapeDtypeStruct((T, D_MODEL), F32),
        compiler_params=_cp("parallel", "parallel"),
    )(qkv, qkv, qkv)


def _sb_attn_bwd(qkv, o, d_o, name):
    T = qkv.shape[0]
    tq, ks = min(ATT_TQ, T), min(ATT_KS, T)
    assert T % tq == 0 and T % ks == 0 and ks % tq == 0
    nq = T // tq
    nks = T // ks
    nsub = ks // ATT_BK
    nhp = N_HEADS // 2
    nt = (((1,), (1,)), ((), ()))

    def body(q_ref, k_ref, v_ref, o_ref, do_ref, dq_ref, dk_ref, dv_ref, dkt_acc, dvt_acc):
        qi = pl.program_id(1)

        @pl.when(qi == 0)
        def _():
            dkt_acc[...] = jnp.zeros_like(dkt_acc)
            dvt_acc[...] = jnp.zeros_like(dvt_acc)

        first = lax.broadcasted_iota(jnp.int32, (1, LANES), 1) < HEAD_DIM
        first_t = lax.broadcasted_iota(jnp.int32, (LANES, 1), 0) < HEAD_DIM
        qf = q_ref[...].astype(F32) * HEAD_DIM ** -0.5
        qs = _masked_heads(qf, first)
        qts = _masked_heads(qf.T, first_t)
        dof = do_ref[...].astype(BF16).astype(F32)
        dos = _masked_heads(dof, first)
        dots = _masked_heads(dof.T, first_t)
        prod = dof * o_ref[...]
        deltas = (jnp.broadcast_to(jnp.sum(jnp.where(first, prod, 0.0), axis=1, keepdims=True), (tq, LANES)),
                  jnp.broadcast_to(jnp.sum(jnp.where(first, 0.0, prod), axis=1, keepdims=True), (tq, LANES)))
        uu = _suffix_consts()
        tpos = qi * tq + lax.broadcasted_iota(jnp.int32, (tq, ATT_BK), 0)
        col = lax.broadcasted_iota(jnp.int32, (tq, ATT_BK), 1)

        def block(kb, carry, masked):
            off = pl.multiple_of(kb * ks, ks)
            kblk = k_ref[pl.ds(off, ks), :]
            vblk = v_ref[pl.ds(off, ks), :]
            out = []
            dkt = jnp.zeros((LANES, ks), F32)
            dvt = jnp.zeros((LANES, ks), F32)
            for h in range(2):
                dq, run, grun = carry[3 * h], carry[3 * h + 1], carry[3 * h + 2]
                z = lax.dot_general(qs[h], kblk, nt, preferred_element_type=F32)
                lb = _log_sigmoid(z)
                l1m = lb - z
                dw = lax.dot_general(dos[h], vblk, nt, preferred_element_type=F32)
                wbs, dzs = [None] * nsub, [None] * nsub
                for j in reversed(range(nsub)):
                    sl = slice(j * ATT_BK, (j + 1) * ATT_BK)
                    l1 = l1m[:, sl]
                    if masked:
                        causal = (off + j * ATT_BK + col) < tpos
                        l1 = jnp.where(causal, l1, 0.0)
                    cs, tot = _suffix_sums(l1, uu)
                    w = jnp.exp(lb[:, sl] + cs + run)
                    if masked:
                        w = jnp.where(causal, w, 0.0)
                    wbs[j] = w.astype(BF16)
                    g = wbs[j].astype(F32) * dw[:, sl]
                    gcs, gtot = _suffix_sums(g, uu)
                    gpre = deltas[h] - grun - gcs - g
                    dz = g - jnp.exp(lb[:, sl]) * (g + gpre)
                    if masked:
                        dz = jnp.where(causal, dz, 0.0)
                    dzs[j] = dz.astype(BF16)
                    run = run + tot
                    grun = grun + gtot
                wb = jnp.concatenate(wbs, axis=1)
                dzb = jnp.concatenate(dzs, axis=1)
                dq = dq + jnp.dot(dzb, kblk, preferred_element_type=F32)
                dkt = dkt + jnp.dot(qts[h], dzb, preferred_element_type=F32)
                dvt = dvt + jnp.dot(dots[h], wb, preferred_element_type=F32)
                out += [dq, run, grun]
            dkt_acc[kb] += dkt
            dvt_acc[kb] += dvt
            return tuple(out)

        nfull = (qi * tq) // ks
        carry = (jnp.zeros((tq, LANES), F32),) * 6
        carry = block(nfull, carry, True)
        carry = lax.fori_loop(0, nfull, lambda i, c: block(nfull - 1 - i, c, False), carry)
        dq_ref[...] = (jnp.where(first, carry[0], carry[3]) * HEAD_DIM ** -0.5).astype(dq_ref.dtype)

        @pl.when(qi == nq - 1)
        def _():
            for kb in range(nks):
                dk_ref[kb * ks:(kb + 1) * ks, :] = dkt_acc[kb].T.astype(dk_ref.dtype)
                dv_ref[kb * ks:(kb + 1) * ks, :] = dvt_acc[kb].T.astype(dv_ref.dtype)

    qspec = pl.BlockSpec((tq, LANES), lambda p, i: (i, p))
    full = pl.BlockSpec((T, LANES), lambda p, i: (0, p))
    return pl.pallas_call(
        body, name=name, grid=(nhp, nq),
        in_specs=[qspec, pl.BlockSpec((T, LANES), lambda p, i: (0, nhp + p)),
                  pl.BlockSpec((T, LANES), lambda p, i: (0, 2 * nhp + p)), qspec, qspec],
        out_specs=[qspec, full, full],
        out_shape=[jax.ShapeDtypeStruct((T, D_MODEL), BF16)] * 3,
        scratch_shapes=[pltpu.VMEM((nks, LANES, ks), F32), pltpu.VMEM((nks, LANES, ks), F32)],
        compiler_params=_cp("parallel", "arbitrary"),
    )(qkv, qkv, qkv, o, d_o)


def _rg_conv_fwd(x, conv_w, conv_b, name):
    T, C = x.shape
    bm = _row_block(T, 512)
    bc = C
    prev, _ = _halo_specs(T, bm, bc)

    def body(x_ref, h_ref, w_ref, b_ref, o_ref):
        halo = jnp.where(pl.program_id(1) > 0, h_ref[...], 0.0)
        o_ref[...] = _causal_conv(x_ref[...], halo, w_ref, b_ref, 4)

    return pl.pallas_call(
        body, name=name, grid=(C // bc, T // bm),
        in_specs=[pl.BlockSpec((bm, bc), lambda j, i: (i, j)), prev,
                  pl.BlockSpec((SUBLANES, bc), lambda j, i: (0, j)), pl.BlockSpec((1, bc), lambda j, i: (0, j))],
        out_specs=pl.BlockSpec((bm, bc), lambda j, i: (i, j)),
        out_shape=jax.ShapeDtypeStruct((T, C), F32),
        compiler_params=_cp("parallel", "parallel"),
    )(x, x, conv_w, conv_b)


def _rg_gate_math(ra, rx, rec, vec_ref):
    r = _sigmoid(ra + vec_ref[0:1, :])
    ig = _sigmoid(rx + vec_ref[1:2, :])
    lam = vec_ref[2:3, :]
    sp = jnp.maximum(-lam, 0.0) + jnp.log(1.0 + jnp.exp(-jnp.abs(lam)))
    log_a = -RG_C * r * sp
    a = jnp.exp(log_a)
    th = jnp.tanh(log_a)
    mult = jnp.sqrt(-2.0 * th / (1.0 - th))
    return r, ig, sp, a, mult


def _rg_gates_fwd(rax, rec, vecs, name):
    T, C = rec.shape
    bm = _row_block(T, 512)

    def body(ra_ref, rx_ref, rec_ref, vec_ref, a_ref, u_ref):
        recv = rec_ref[...]
        _, ig, _, a, mult = _rg_gate_math(ra_ref[...], rx_ref[...], recv, vec_ref)
        a_ref[...] = a
        u_ref[...] = mult * ig * recv

    blk = pl.BlockSpec((bm, C), lambda i: (i, 0))
    return pl.pallas_call(
        body, name=name, grid=(T // bm,),
        in_specs=[blk, pl.BlockSpec((bm, C), lambda i: (i, 1)), blk, pl.BlockSpec((SUBLANES, C), lambda i: (0, 0))],
        out_specs=[blk, blk],
        out_shape=[jax.ShapeDtypeStruct((T, C), F32)] * 2,
        compiler_params=_cp("parallel"),
    )(rax, rax, rec, vecs)


def _linear_scan(a, b, name, reverse=False, shift_coef=False):
    T, C = a.shape
    bm = _row_block(T, 256)
    bc = LANES
    nt = T // bm
    _, nxt = _halo_specs(T, bm, bc)
    steps =[1 << s for s in range(int(math.log2(bm)))]
    assert 1 << len(steps) == bm

    def body(*refs):
        if shift_coef:
            a_ref, an_ref, b_ref, h_ref, carry = refs
        else:
            a_ref, b_ref, h_ref, carry = refs
        i = pl.program_id(1)

        @pl.when(i == 0)
        def _():
            carry[...] = jnp.zeros_like(carry)

        coef = a_ref[...]
        if shift_coef:
            coef = _shift_up(coef, an_ref[...], 1)
        val = b_ref[...]
        row = lax.broadcasted_iota(jnp.int32, (bm, bc), 0)
        for s in steps:
            if reverse:
                keep = row < bm - s
                cs, vs = pltpu.roll(coef, bm - s, 0), pltpu.roll(val, bm - s, 0)
            else:
                keep = row >= s
                cs, vs = pltpu.roll(coef, s, 0), pltpu.roll(val, s, 0)
            val = jnp.where(keep, coef * vs + val, val)
            coef = jnp.where(keep, coef * cs, coef)
        h = val + coef * carry[0:1, :]
        h_ref[...] = h
        edge = h[0:1, :] if reverse else h[bm - 1:bm, :]
        carry[...] = jnp.broadcast_to(edge, carry.shape)

    if reverse:
        blk = pl.BlockSpec((bm, bc), lambda j, i: (nt - 1 - i, j))
        r = bm // SUBLANES
        last = T // SUBLANES - 1
        nxt = pl.BlockSpec((SUBLANES, bc), lambda j, i: (jnp.minimum((nt - i) * r, last), j))
    else:
        blk = pl.BlockSpec((bm, bc), lambda j, i: (i, j))
    in_specs = [blk] + ([nxt] if shift_coef else []) + [blk]
    args = (a,) + ((a,) if shift_coef else ()) + (b,)
    return pl.pallas_call(
        body, name=name, grid=(C // bc, nt),
        in_specs=in_specs, out_specs=blk,
        out_shape=jax.ShapeDtypeStruct((T, C), F32),
        scratch_shapes=[pltpu.VMEM((SUBLANES, bc), F32)],
        compiler_params=_cp("parallel", "arbitrary"),
    )(*args)


def _rg_out_fwd(gate_branch, h, name):
    T, C = h.shape
    bm = _row_block(T, 512)

    def body(g_ref, h_ref, o_ref):
        o_ref[...] = (_gelu(g_ref[...]) * h_ref[...]).astype(o_ref.dtype)

    blk = pl.BlockSpec((bm, C), lambda i: (i, 0))
    return pl.pallas_call(
        body, name=name, grid=(T // bm,), in_specs=[blk, blk], out_specs=blk,
        out_shape=jax.ShapeDtypeStruct((T, C), BF16), compiler_params=_cp("parallel"),
    )(gate_branch, h)


def _rg_out_bwd(d_gh, gate_branch, h, name):
    T, C = h.shape
    bm = _row_block(T, 512)

    def body(d_ref, g_ref, h_ref, dg_ref, dh_ref):
        gv, gg = _gelu_and_grad(g_ref[...])
        d = d_ref[...]
        dg_ref[...] = (d * h_ref[...] * gg).astype(dg_ref.dtype)
        dh_ref[...] = d * gv

    blk = pl.BlockSpec((bm, C), lambda i: (i, 0))
    return pl.pallas_call(
        body, name=name, grid=(T // bm,), in_specs=[blk, blk, blk], out_specs=[blk, blk],
        out_shape=[jax.ShapeDtypeStruct((T, C), BF16), jax.ShapeDtypeStruct((T, C), F32)],
        compiler_params=_cp("parallel"),
    )(d_gh, gate_branch, h)


def _rg_gates_bwd(dhs, h, rax, rec, vecs, name):
    T, C = rec.shape
    bm = _row_block(T, 1024)
    bc = LANES
    prev, _ = _halo_specs(T, bm, bc)
    nc = C // bc

    def body(d_ref, h_ref, hh_ref, ra_ref, rx_ref, rec_ref, vec_ref, dra_ref, drx_ref, drec_ref, st_ref):
        i = pl.program_id(1)
        halo = jnp.where(i > 0, hh_ref[...], 0.0)
        recv = rec_ref[...]
        du = d_ref[...]
        r, ig, sp, a, mult = _rg_gate_math(ra_ref[...], rx_ref[...], recv, vec_ref)
        da = du * _shift_down(h_ref[...], halo, 1)
        d_i = du * mult * recv
        drec_ref[...] = du * mult * ig
        d_mult = du * ig * recv
        d_log_a = da * a - d_mult * (a * a) / mult
        d_ra = d_log_a * (-RG_C) * sp * r * (1.0 - r)
        d_rx = d_i * ig * (1.0 - ig)
        dra_ref[...] = d_ra.astype(dra_ref.dtype)
        drx_ref[...] = d_rx.astype(drx_ref.dtype)
        lam = vec_ref[2:3, :]
        d_sp = jnp.sum(d_log_a * (-RG_C) * r, axis=0, keepdims=True)
        rows = [jnp.sum(d_ra, axis=0, keepdims=True), jnp.sum(d_rx, axis=0, keepdims=True),
                d_sp * (_sigmoid(lam) - 1.0), jnp.zeros((5, bc), F32)]

        @pl.when(i == 0)
        def _():
            st_ref[...] = jnp.zeros_like(st_ref)

        st_ref[...] += jnp.concatenate(rows, axis=0)

    blk = pl.BlockSpec((bm, bc), lambda j, i: (i, j))
    blk_x = pl.BlockSpec((bm, bc), lambda j, i: (i, nc + j))
    vec = pl.BlockSpec((SUBLANES, bc), lambda j, i: (0, j))
    d_ra, d_rx, d_rec, stats = pl.pallas_call(
        body, name=name, grid=(nc, T // bm),
        in_specs=[blk, blk, prev, blk, blk_x, blk, vec],
        out_specs=[blk, blk, blk, vec],
        out_shape=[jax.ShapeDtypeStruct((T, C), BF16), jax.ShapeDtypeStruct((T, C), BF16),
                   jax.ShapeDtypeStruct((T, C), F32), jax.ShapeDtypeStruct((SUBLANES, C), F32)],
        compiler_params=_cp("parallel", "arbitrary"),
    )(dhs, h, h, rax, rax, rec, vecs)
    return d_ra, d_rx, d_rec, stats


def _rg_conv_bwd_stats(d_rec, x, name):
    T, C = x.shape
    bm = _row_block(T, 1024)
    bc = LANES
    prev, _ = _halo_specs(T, bm, bc)

    def body(d_ref, x_ref, h_ref, st_ref):
        i = pl.program_id(1)
        halo = jnp.where(i > 0, h_ref[...], 0.0)
        d = d_ref[...]
        xv = x_ref[...]
        rows = [jnp.sum(d * _shift_down(xv, halo, 3 - k), axis=0, keepdims=True) for k in range(3)]
        rows += [jnp.sum(d * xv, axis=0, keepdims=True), jnp.sum(d, axis=0, keepdims=True), jnp.zeros((3, bc), F32)]

        @pl.when(i == 0)
        def _():
            st_ref[...] = jnp.zeros_like(st_ref)

        st_ref[...] += jnp.concatenate(rows, axis=0)

    blk = pl.BlockSpec((bm, bc), lambda j, i: (i, j))
    vec = pl.BlockSpec((SUBLANES, bc), lambda j, i: (0, j))
    return pl.pallas_call(
        body, name=name, grid=(C // bc, T // bm),
        in_specs=[blk, blk, prev], out_specs=vec,
        out_shape=jax.ShapeDtypeStruct((SUBLANES, C), F32),
        compiler_params=_cp("parallel", "arbitrary"),
    )(d_rec, x, x)


def _adamw(w, g, m, v, name):
    R, W = w.shape
    bm = R
    for cand in (512, 256, 128, 64, 32, 16, 8):
        if R % cand == 0:
            bm = cand
            break
    bc1 = 1.0 - ADAM_B1 ** ADAM_STEP
    bc2 = 1.0 - ADAM_B2 ** ADAM_STEP

    def body(w_ref, g_ref, m_ref, v_ref, d_ref, nm_ref, nv_ref):
        gv = g_ref[...]
        nm = ADAM_B1 * m_ref[...] + (1.0 - ADAM_B1) * gv
        nv = ADAM_B2 * v_ref[...] + (1.0 - ADAM_B2) * (gv * gv)
        m_hat = nm / bc1
        v_hat = nv / bc2
        d_ref[...] = -ADAM_LR * (m_hat / (jnp.sqrt(v_hat) + ADAM_EPS) + ADAM_WD * w_ref[...])
        nm_ref[...] = nm
        nv_ref[...] = nv

    blk = pl.BlockSpec((bm, W), lambda i: (i, 0))
    return pl.pallas_call(
        body, name=name, grid=(R // bm,), in_specs=[blk] * 4, out_specs=[blk] * 3,
        out_shape=[jax.ShapeDtypeStruct((R, W), F32)] * 3, compiler_params=_cp("parallel"),
    )(w, g, m, v)


_ANY = pl.BlockSpec(memory_space=pl.ANY)


def _me():
    return lax.axis_index("x"), lax.axis_index("y"), lax.axis_index("c")


def _other_chips(x, y):
    return [(1 - x, y), (x, 1 - y), (1 - x, 1 - y)]


def _all_gather_chips(shard, name):
    R, W = shard.shape
    Rh = R // 2
    assert Rh * 2 == R and Rh % 16 == 0

    def body(x_ref, out_ref, send_sems, recv_sems, local_sem):
        x, y, c = _me()
        s_me = 2 * x + y
        chips = _other_chips(x, y)
        half = pl.ds(pl.multiple_of(c * Rh, 16), Rh)
        other_half = pl.ds(pl.multiple_of((1 - c) * Rh, 16), Rh)

        def copy(k, src, dst, to):
            return pltpu.make_async_remote_copy(src_ref=src, dst_ref=dst, send_sem=send_sems.at[k],
                                                recv_sem=recv_sems.at[k], device_id=to, device_id_type=MESH)

        mine = pltpu.make_async_copy(x_ref, out_ref.at[s_me], local_sem)
        mine.start()
        first = [copy(j, x_ref.at[half], out_ref.at[s_me, half], (*chip, c)) for j, chip in enumerate(chips)]
        for cp in first:
            cp.start()
        passed = []
        for j, (cx, cy) in enumerate(chips):
            landed = out_ref.at[2 * cx + cy, half]
            copy(j, landed, landed, (cx, cy, c)).wait_recv()
            fwd = copy(3 + j, landed, landed, (x, y, 1 - c))
            fwd.start()
            passed.append(fwd)
        for j, (cx, cy) in enumerate(chips):
            landed = out_ref.at[2 * cx + cy, other_half]
            copy(3 + j, landed, landed, (x, y, 1 - c)).wait_recv()
        for cp in first + passed:
            cp.wait_send()
        mine.wait()

    return pl.pallas_call(
        body, name=name, in_specs=[_ANY], out_specs=_ANY,
        out_shape=jax.ShapeDtypeStruct((N_CHIPS, R, W), shard.dtype),
        scratch_shapes=[pltpu.SemaphoreType.DMA((6,)), pltpu.SemaphoreType.DMA((6,)), pltpu.SemaphoreType.DMA],
    )(shard)


def _pair_send_other_halves(g, name):
    n, R, W = g.shape
    Rh = R // 2
    assert Rh % 8 == 0

    def body(g_ref, land_ref, send_sem, recv_sem):
        x, y, c = _me()
        src = g_ref.at[:, pl.ds(pl.multiple_of((1 - c) * Rh, 8), Rh), :]
        cp = pltpu.make_async_remote_copy(src_ref=src, dst_ref=land_ref, send_sem=send_sem, recv_sem=recv_sem,
                                          device_id=(x, y, 1 - c), device_id_type=MESH)
        cp.start()
        cp.wait()

    return pl.pallas_call(
        body, name=name, in_specs=[_ANY], out_specs=_ANY,
        out_shape=jax.ShapeDtypeStruct((n, Rh, W), g.dtype),
        scratch_shapes=[pltpu.SemaphoreType.DMA, pltpu.SemaphoreType.DMA],
    )(g)


def _add_own_half(g, landed, name):
    n, R, W = g.shape
    Rh = R // 2
    bm = 8
    for cand in (512, 256, 128, 64, 32, 16, 8):
        if Rh % cand == 0:
            bm = cand
            break
    nb = Rh // bm
    c_arr = lax.axis_index("c").astype(jnp.int32).reshape((1,))

    def body(c_ref, g_ref, l_ref, o_ref):
        o_ref[...] = g_ref[...] + l_ref[...]

    grid_spec = pltpu.PrefetchScalarGridSpec(
        num_scalar_prefetch=1, grid=(n, nb),
        in_specs=[pl.BlockSpec((1, bm, W), lambda s, i, c_ref: (s, c_ref[0] * nb + i, 0)),
                  pl.BlockSpec((1, bm, W), lambda s, i, c_ref: (s, i, 0))],
        out_specs=pl.BlockSpec((1, bm, W), lambda s, i, c_ref: (s, i, 0)))
    return pl.pallas_call(
        body, name=name, grid_spec=grid_spec, out_shape=jax.ShapeDtypeStruct((n, Rh, W), g.dtype),
        compiler_params=_cp("parallel", "parallel"),
    )(c_arr, g, landed)


def _scatter_to_chips(p, name):
    n, Rh, W = p.shape

    def body(p_ref, land_ref, send_sems, recv_sems, local_sem):
        x, y, c = _me()
        s_me = 2 * x + y
        chips = _other_chips(x, y)
        mine = pltpu.make_async_copy(p_ref.at[s_me], land_ref.at[s_me], local_sem)
        mine.start()
        sends = []
        for j, (cx, cy) in enumerate(chips):
            cp = pltpu.make_async_remote_copy(src_ref=p_ref.at[2 * cx + cy], dst_ref=land_ref.at[s_me],
                                              send_sem=send_sems.at[j], recv_sem=recv_sems.at[j],
                                              device_id=(cx, cy, c), device_id_type=MESH)
            cp.start()
            sends.append(cp)
        for j, (cx, cy) in enumerate(chips):
            slot = land_ref.at[2 * cx + cy]
            pltpu.make_async_remote_copy(src_ref=slot, dst_ref=slot, send_sem=send_sems.at[j], recv_sem=recv_sems.at[j],
                                         device_id=(cx, cy, c), device_id_type=MESH).wait_recv()
        for cp in sends:
            cp.wait_send()
        mine.wait()

    return pl.pallas_call(
        body, name=name, in_specs=[_ANY], out_specs=_ANY,
        out_shape=jax.ShapeDtypeStruct((n, Rh, W), p.dtype),
        scratch_shapes=[pltpu.SemaphoreType.DMA((3,)), pltpu.SemaphoreType.DMA((3,)), pltpu.SemaphoreType.DMA],
    )(p)


def _sum_slots(l, name):
    n, Rh, W = l.shape
    bm = 8
    for cand in (512, 256, 128, 64, 32, 16, 8):
        if Rh % cand == 0:
            bm = cand
            break

    def body(l_ref, o_ref):
        o_ref[...] = ((l_ref[0] + l_ref[1]) + l_ref[2]) + l_ref[3]

    return pl.pallas_call(
        body, name=name, grid=(Rh // bm,),
        in_specs=[pl.BlockSpec((n, bm, W), lambda i: (0, i, 0))], out_specs=pl.BlockSpec((bm, W), lambda i: (i, 0)),
        out_shape=jax.ShapeDtypeStruct((Rh, W), l.dtype), compiler_params=_cp("parallel"),
    )(l)


def _pair_gather(f, name):
    Rh, W = f.shape

    def body(f_ref, out_ref, send_sem, recv_sem, local_sem):
        x, y, c = _me()
        mine = pltpu.make_async_copy(f_ref, out_ref.at[c], local_sem)
        mine.start()
        cp = pltpu.make_async_remote_copy(src_ref=f_ref, dst_ref=out_ref.at[c], send_sem=send_sem, recv_sem=recv_sem,
                                          device_id=(x, y, 1 - c), device_id_type=MESH)
        cp.start()
        sib = out_ref.at[1 - c]
        pltpu.make_async_remote_copy(src_ref=sib, dst_ref=sib, send_sem=send_sem, recv_sem=recv_sem,
                                     device_id=(x, y, 1 - c), device_id_type=MESH).wait_recv()
        cp.wait_send()
        mine.wait()

    return pl.pallas_call(
        body, name=name, in_specs=[_ANY], out_specs=_ANY,
        out_shape=jax.ShapeDtypeStruct((2, Rh, W), f.dtype),
        scratch_shapes=[pltpu.SemaphoreType.DMA, pltpu.SemaphoreType.DMA, pltpu.SemaphoreType.DMA],
    )(f)


def _pack(arrays, rows_multiple, dtype):
    flat = jnp.concatenate([a.reshape(-1).astype(dtype) for a in arrays])
    n = flat.shape[0]
    rows = -(-n // FLAT_W)
    rows = -(-rows // rows_multiple) * rows_multiple
    return jnp.pad(flat, (0, rows * FLAT_W - n)).reshape(rows, FLAT_W)


def _unpack(flat, shapes):
    flat = flat.reshape(-1)
    out, off = [], 0
    for shp in shapes:
        n = math.prod(shp)
        out.append(flat[off:off + n].reshape(shp))
        off += n
    return out


def _pad_cols(a, width):
    return jnp.pad(a, [(0, 0)] * (a.ndim - 1) + [(0, width - a.shape[-1])])


def _rows8(a):
    return jnp.pad(a, ((0, SUBLANES - a.shape[0]), (0, 0)))


def _ffn_interleave(a):
    lead = a.shape[:-1]
    n = D_FF // FFN_TILE
    return jnp.swapaxes(a.reshape(lead + (2, n, FFN_TILE)), -3, -2).reshape(lead + (2 * D_FF,))


def _ffn_deinterleave(a):
    lead = a.shape[:-1]
    n = D_FF // FFN_TILE
    return jnp.swapaxes(a.reshape(lead + (n, 2, FFN_TILE)), -3, -2).reshape(lead + (2 * D_FF,))


def _block_diag_dense(w):
    eye = jnp.eye(RG_BLOCKS, dtype=w.dtype)
    dense = jnp.einsum("ncd,nm->ncmd", w, eye).reshape(D_RNN, D_RNN)
    return jnp.pad(dense, ((0, D_RNN_PAD - D_RNN), (0, D_RNN_PAD - D_RNN)))


def _block_diag_extract(dense):
    d = dense[:D_RNN, :D_RNN].reshape(RG_BLOCKS, RG_BLOCK_W, RG_BLOCKS, RG_BLOCK_W)
    return jnp.stack([d[n, :, n, :] for n in range(RG_BLOCKS)])


def _ffn_fwd(x, l, P, tag):
    n = _rmsnorm_fwd(x, P["ffn_pre_g"][l], f"{tag}_prenorm")
    u_pre = _mm(n, P["ffn_w_up"][l], "nn", F32, f"{tag}_up", bn=2 * FFN_TILE * 1)
    act = _ffn_act_fwd(u_pre, P["ffn_conv_w"][l], P["ffn_conv_b"][l], f"{tag}_act")
    f = _mm(act, P["ffn_w_down"][l], "nn", F32, f"{tag}_down", bk=D_FF)
    return f, (n, u_pre, act)


def _ffn_bwd(df, x, saved, l, P, tag):
    n, u_pre, act = saved
    g_down = _mm(act, df, "tn", F32, f"{tag}_dwdown", bm=D_FF // 2, bk=1024)
    d_act = _mm(df, P["ffn_w_down"][l], "nt", F32, f"{tag}_dact", bn=D_FF // 2)
    du, stats = _ffn_act_bwd(u_pre, P["ffn_conv_w"][l], P["ffn_conv_b"][l], d_act, f"{tag}_actbwd")
    du_pre = _conv_transpose(du, P["ffn_conv_w"][l], 3, f"{tag}_convT", 2 * FFN_TILE)
    g_up = _mm(n, du_pre, "tn", F32, f"{tag}_dwup", bn=2 * D_FF // 4, bk=1024)
    dn = _mm(du_pre, P["ffn_w_up"][l], "nt", F32, f"{tag}_dn", bk=2 * D_FF // 4)
    return dn, dict(ffn_w_down=g_down, ffn_w_up=g_up, ffn_conv_w=stats[0:3], ffn_conv_b=stats[3])


def _local_step(x, target, P):
    G = {}
    n0 = _rmsnorm_fwd(x, P["mix_pre_g"][0], "l0_prenorm")
    qkv = _mm(n0, P["attn_w_qkv"], "nn", BF16, "l0_qkv")
    o = _sb_attn_fwd(qkv, "l0_attn")
    a0 = _mm(o, P["attn_w_o"], "nn", F32, "l0_wo")
    x1 = _postnorm_residual(x, a0, P["mix_post_g"][0], "l0_postnorm")
    f0, ffn0 = _ffn_fwd(x1, 0, P, "f0")
    x2 = _postnorm_residual(x1, f0, P["ffn_post_g"][0], "f0_postnorm")
    n1 = _rmsnorm_fwd(x2, P["mix_pre_g"][1], "l1_prenorm")
    gate_b = _mm(n1, P["rg_w_in_gate"], "nn", F32, "l1_in_gate", bn=D_RNN_PAD)
    rec_b = _mm(n1, P["rg_w_in_rec"], "nn", F32, "l1_in_rec", bn=D_RNN_PAD)
    rec = _rg_conv_fwd(rec_b, P["rg_conv_w"], P["rg_conv_b"], "l1_conv")
    rax = _mm(rec, P["rg_w_ax"], "nn", F32, "l1_gates_mm", bn=D_RNN_PAD, bk=D_RNN_PAD)
    a, u = _rg_gates_fwd(rax, rec, P["rg_vecs"], "l1_gates")
    h = _linear_scan(a, u, "l1_scan")
    gh = _rg_out_fwd(gate_b, h, "l1_gate_out")
    y1 = _mm(gh, P["rg_w_out"], "nn", F32, "l1_wout", bk=D_RNN_PAD)
    x3 = _postnorm_residual(x2, y1, P["mix_post_g"][1], "l1_postnorm")
    f1, ffn1 = _ffn_fwd(x3, 1, P, "f1")
    dx, loss = _postnorm_loss(x3, f1, P["ffn_post_g"][1], target, "loss")

    df, G["ffn_post_g1"] = _rmsnorm_bwd(dx, f1, P["ffn_post_g"][1], "f1_postnorm_bwd", out_dtype=BF16)
    dn, g = _ffn_bwd(df, x3, ffn1, 1, P, "f1")
    G.update({k + "1": v for k, v in g.items()})
    dx, G["ffn_pre_g1"] = _rmsnorm_bwd(dn, x3, P["ffn_pre_g"][1], "f1_prenorm_bwd", res=dx)
    dy, G["mix_post_g1"] = _rmsnorm_bwd(dx, y1, P["mix_post_g"][1], "l1_postnorm_bwd", out_dtype=BF16)
    G["rg_w_out"] = _mm(gh, dy, "tn", F32, "l1_dwout", bm=D_RNN_PAD, bk=1024)
    d_gh = _mm(dy, P["rg_w_out"], "nt", F32, "l1_dgh", bn=D_RNN_PAD)
    d_gate, d_h = _rg_out_bwd(d_gh, gate_b, h, "l1_gate_out_bwd")
    dhs = _linear_scan(a, d_h, "l1_scan_bwd", reverse=True, shift_coef=True)
    d_ra, d_rx, d_rec_direct, st = _rg_gates_bwd(dhs, h, rax, rec, P["rg_vecs"], "l1_gates_bwd")
    G["rg_b_a"], G["rg_b_x"], G["rg_lambda"] = st[0], st[1], st[2]
    d_rax = jnp.concatenate([d_ra, d_rx], axis=1)
    G["rg_w_ax"] = _mm(rec, d_rax, "tn", F32, "l1_dwax", bm=D_RNN_PAD, bn=D_RNN_PAD, bk=1024)
    d_rec = _mm(d_rax, P["rg_w_ax"], "nt", F32, "l1_drec", bm=512, bn=D_RNN_PAD, bk=2 * D_RNN_PAD, add=d_rec_direct)
    st = _rg_conv_bwd_stats(d_rec, rec_b, "l1_conv_stats")
    G["rg_conv_w"], G["rg_conv_b"] = st[0:4], st[4]
    d_rec_b = _conv_transpose(d_rec, P["rg_conv_w"], 4, "l1_convT", D_RNN_PAD)
    G["rg_w_in_gate"] = _mm(n1, d_gate, "tn", F32, "l1_dwin_gate", bn=D_RNN_PAD, bk=1024)
    G["rg_w_in_rec"] = _mm(n1, d_rec_b, "tn", F32, "l1_dwin_rec", bn=D_RNN_PAD, bk=1024)
    dn = _mm(d_gate, P["rg_w_in_gate"], "nt", F32, "l1_dn_gate", bk=D_RNN_PAD)
    dn = _mm(d_rec_b, P["rg_w_in_rec"], "nt", F32, "l1_dn_rec", bk=D_RNN_PAD, add=dn)
    dx, G["mix_pre_g1"] = _rmsnorm_bwd(dn, x2, P["mix_pre_g"][1], "l1_prenorm_bwd", res=dx)
    df, G["ffn_post_g0"] = _rmsnorm_bwd(dx, f0, P["ffn_post_g"][0], "f0_postnorm_bwd", out_dtype=BF16)
    dn, g = _ffn_bwd(df, x1, ffn0, 0, P, "f0")
    G.update({k + "0": v for k, v in g.items()})
    dx, G["ffn_pre_g0"] = _rmsnorm_bwd(dn, x1, P["ffn_pre_g"][0], "f0_prenorm_bwd", res=dx)
    da, G["mix_post_g0"] = _rmsnorm_bwd(dx, a0, P["mix_post_g"][0], "l0_postnorm_bwd", out_dtype=BF16)
    G["attn_w_o"] = _mm(o, da, "tn", F32, "l0_dwo", bk=1024)
    d_o = _mm(da, P["attn_w_o"], "nt", F32, "l0_do")
    dq, dk, dv = _sb_attn_bwd(qkv, o, d_o, "l0_attn_bwd")
    d_qkv = jnp.concatenate([dq, dk, dv], axis=1)
    G["attn_w_qkv"] = _mm(n0, d_qkv, "tn", F32, "l0_dwqkv", bk=1024)
    dn = _mm(d_qkv, P["attn_w_qkv"], "nt", F32, "l0_dn")
    dx, G["mix_pre_g0"] = _rmsnorm_bwd(dn, x, P["mix_pre_g"][0], "l0_prenorm_bwd", res=dx)
    return loss[0, 0], dx, G


_BIG = [("attn_w_qkv", 2), ("attn_w_o", 1), ("rg_w_in", 2), ("rg_w_out", 1), ("ffn_w_up", 2), ("ffn_w_down", 1)]
_SMALL_SHARDED = [("rg_conv_w", 2), ("rg_conv_b", 1), ("rg_b_a", 1), ("rg_b_x", 1), ("rg_lambda", 1), ("ffn_conv_w", 2)]
_REPLICATED = ["rg_w_a", "rg_w_x", "ffn_conv_b", "mix_pre_g", "mix_post_g", "ffn_pre_g", "ffn_post_g"]
_ORDER = ["attn_w_qkv", "attn_w_o", "rg_w_in", "rg_conv_w", "rg_conv_b", "rg_w_a", "rg_b_a", "rg_w_x", "rg_b_x",
          "rg_lambda", "rg_w_out", "ffn_w_up", "ffn_conv_w", "ffn_conv_b", "ffn_w_down", "mix_pre_g", "mix_post_g",
          "ffn_pre_g", "ffn_post_g"]


def _join(gathered, shapes, axes):
    per_chip = [_unpack(gathered[s], shapes) for s in range(N_CHIPS)]
    return [jnp.concatenate([per_chip[s][i] for s in range(N_CHIPS)], axis=ax) for i, ax in enumerate(axes)]


def _split(full, axis):
    return jnp.split(full, N_CHIPS, axis=axis)


def _gather_params(W):
    big_shapes = [W[n].shape for n, _ in _BIG]
    big = _all_gather_chips(_pack([W[n] for n, _ in _BIG], 32, BF16), "gather_weights")
    small_shapes = [W[n].shape for n, _ in _SMALL_SHARDED]
    small = _all_gather_chips(_pack([W[n] for n, _ in _SMALL_SHARDED], 32, F32), "gather_small")
    full = dict(zip([n for n, _ in _BIG], _join(big, big_shapes, [ax for _, ax in _BIG])))
    full.update(zip([n for n, _ in _SMALL_SHARDED], _join(small, small_shapes, [ax for _, ax in _SMALL_SHARDED])))
    for n in _REPLICATED:
        full[n] = W[n]
    return full


def _layout_params(full):
    P = {}
    P["attn_w_qkv"] = full["attn_w_qkv"][0]
    P["attn_w_o"] = full["attn_w_o"][0]
    w_in = full["rg_w_in"][0]
    P["rg_w_in_gate"] = _pad_cols(w_in[:, :D_RNN], D_RNN_PAD)
    P["rg_w_in_rec"] = _pad_cols(w_in[:, D_RNN:], D_RNN_PAD)
    P["rg_w_out"] = jnp.pad(full["rg_w_out"][0], ((0, D_RNN_PAD - D_RNN), (0, 0)))
    P["rg_conv_w"] = _rows8(_pad_cols(full["rg_conv_w"][0], D_RNN_PAD))
    P["rg_conv_b"] = _pad_cols(full["rg_conv_b"], D_RNN_PAD)
    P["rg_vecs"] = _rows8(_pad_cols(jnp.concatenate([full["rg_b_a"], full["rg_b_x"], full["rg_lambda"]], axis=0), D_RNN_PAD))
    P["rg_w_ax"] = jnp.concatenate([_block_diag_dense(full["rg_w_a"][0]), _block_diag_dense(full["rg_w_x"][0])], axis=1).astype(BF16)
    P["ffn_w_up"] = [_ffn_interleave(full["ffn_w_up"][l]) for l in range(2)]
    P["ffn_conv_w"] = [_rows8(_ffn_interleave(full["ffn_conv_w"][l])) for l in range(2)]
    P["ffn_conv_b"] = [_ffn_interleave(full["ffn_conv_b"][l:l + 1]) for l in range(2)]
    P["ffn_w_down"] = [full["ffn_w_down"][l] for l in range(2)]
    for n in ("mix_pre_g", "mix_post_g", "ffn_pre_g", "ffn_post_g"):
        P[n] = [full[n][l:l + 1] for l in range(2)]
    return P


def _assemble_grads(G):
    out = {}
    out["attn_w_qkv"] = G["attn_w_qkv"][None]
    out["attn_w_o"] = G["attn_w_o"][None]
    out["rg_w_in"] = jnp.concatenate([G["rg_w_in_gate"][:, :D_RNN], G["rg_w_in_rec"][:, :D_RNN]], axis=1)[None]
    out["rg_conv_w"] = G["rg_conv_w"][:, :D_RNN][None]
    out["rg_conv_b"] = G["rg_conv_b"][:D_RNN][None]
    out["rg_w_a"] = _block_diag_extract(G["rg_w_ax"][:, :D_RNN_PAD])[None]
    out["rg_w_x"] = _block_diag_extract(G["rg_w_ax"][:, D_RNN_PAD:])[None]
    out["rg_b_a"] = G["rg_b_a"][:D_RNN][None]
    out["rg_b_x"] = G["rg_b_x"][:D_RNN][None]
    out["rg_lambda"] = G["rg_lambda"][:D_RNN][None]
    out["rg_w_out"] = G["rg_w_out"][:D_RNN][None]
    out["ffn_w_up"] = jnp.stack([_ffn_deinterleave(G[f"ffn_w_up{l}"]) for l in range(2)])
    out["ffn_conv_w"] = jnp.stack([_ffn_deinterleave(G[f"ffn_conv_w{l}"]) for l in range(2)])
    out["ffn_conv_b"] = jnp.stack([_ffn_deinterleave(G[f"ffn_conv_b{l}"]) for l in range(2)])
    out["ffn_w_down"] = jnp.stack([G[f"ffn_w_down{l}"] for l in range(2)])
    for n in ("mix_pre_g", "mix_post_g", "ffn_pre_g", "ffn_post_g"):
        out[n] = jnp.concatenate([G[n + "0"], G[n + "1"]], axis=0)
    return out


def _reduce_grads(full_grads, W):
    sharded = _BIG + _SMALL_SHARDED
    rep_flat = jnp.concatenate([full_grads[n].reshape(-1) for n in _REPLICATED])
    n_rep = rep_flat.shape[0]
    piece = -(-n_rep // (N_CHIPS * FLAT_W)) * FLAT_W
    rep_flat = jnp.pad(rep_flat, (0, N_CHIPS * piece - n_rep)).reshape(N_CHIPS, piece)
    blocks = []
    for s in range(N_CHIPS):
        parts = [_split(full_grads[n], ax)[s] for n, ax in sharded] + [rep_flat[s]]
        blocks.append(_pack(parts, 32, F32))
    g = jnp.stack(blocks)
    landed = _pair_send_other_halves(g, "rs_pair_send")
    chip_sum = _add_own_half(g, landed, "rs_pair_add")
    slots = _scatter_to_chips(chip_sum, "rs_scatter")
    half = _sum_slots(slots, "rs_sum")
    mine = _pair_gather(half, "rs_pair_gather").reshape(-1, FLAT_W)
    shapes = [W[n].shape for n, _ in sharded] + [(piece,)]
    parts = _unpack(mine, shapes)
    out = dict(zip([n for n, _ in sharded], parts[:-1]))
    rep_rows = piece // FLAT_W
    rep_pad = -(-rep_rows // 32) * 32
    rep_piece = jnp.pad(parts[-1].reshape(rep_rows, FLAT_W), ((0, rep_pad - rep_rows), (0, 0)))
    rep_all = _all_gather_chips(rep_piece, "gather_replicated_grads")[:, :rep_rows].reshape(-1)
    out.update(zip(_REPLICATED, _unpack(rep_all, [W[n].shape for n in _REPLICATED])))
    return out


def _update(W, M, V, grads):
    delta, new_m, new_v = {}, {}, {}
    small = [n for n in _ORDER if n not in dict(_BIG)]
    for n, _ in _BIG:
        shp = W[n].shape
        two_d = (-1, shp[-1])
        d, m, v = _adamw(W[n].reshape(two_d), grads[n].reshape(two_d), M[n].reshape(two_d), V[n].reshape(two_d), f"adamw_{n}")
        delta[n], new_m[n], new_v[n] = d.reshape(shp), m.reshape(shp), v.reshape(shp)
    packed = [_pack([src[n] for n in small], 8, F32) for src in (W, grads, M, V)]
    outs = _adamw(*packed, "adamw_small")
    shapes = [W[n].shape for n in small]
    for tgt, flat in zip((delta, new_m, new_v), outs):
        tgt.update(zip(small, _unpack(flat, shapes)))
    return delta, new_m, new_v


def kernel(x, attn_w_qkv, attn_w_o, rg_w_in, rg_conv_w, rg_conv_b, rg_w_a, rg_b_a, rg_w_x, rg_b_x, rg_lambda, rg_w_out, ffn_w_up, ffn_conv_w, ffn_conv_b, ffn_w_down, mix_pre_g, mix_post_g, ffn_pre_g, ffn_post_g, loss_target, m_attn_w_qkv, m_attn_w_o, m_rg_w_in, m_rg_conv_w, m_rg_conv_b, m_rg_w_a, m_rg_b_a, m_rg_w_x, m_rg_b_x, m_rg_lambda, m_rg_w_out, m_ffn_w_up, m_ffn_conv_w, m_ffn_conv_b, m_ffn_w_down, m_mix_pre_g, m_mix_post_g, m_ffn_pre_g, m_ffn_post_g, v_attn_w_qkv, v_attn_w_o, v_rg_w_in, v_rg_conv_w, v_rg_conv_b, v_rg_w_a, v_rg_b_a, v_rg_w_x, v_rg_b_x, v_rg_lambda, v_rg_w_out, v_ffn_w_up, v_ffn_conv_w, v_ffn_conv_b, v_ffn_w_down, v_mix_pre_g, v_mix_post_g, v_ffn_pre_g, v_ffn_post_g):
    W = dict(attn_w_qkv=attn_w_qkv, attn_w_o=attn_w_o, rg_w_in=rg_w_in, rg_conv_w=rg_conv_w, rg_conv_b=rg_conv_b,
             rg_w_a=rg_w_a, rg_b_a=rg_b_a, rg_w_x=rg_w_x, rg_b_x=rg_b_x, rg_lambda=rg_lambda, rg_w_out=rg_w_out,
             ffn_w_up=ffn_w_up, ffn_conv_w=ffn_conv_w, ffn_conv_b=ffn_conv_b, ffn_w_down=ffn_w_down,
             mix_pre_g=mix_pre_g, mix_post_g=mix_post_g, ffn_pre_g=ffn_pre_g, ffn_post_g=ffn_post_g)
    M = dict(attn_w_qkv=m_attn_w_qkv, attn_w_o=m_attn_w_o, rg_w_in=m_rg_w_in, rg_conv_w=m_rg_conv_w, rg_conv_b=m_rg_conv_b,
             rg_w_a=m_rg_w_a, rg_b_a=m_rg_b_a, rg_w_x=m_rg_w_x, rg_b_x=m_rg_b_x, rg_lambda=m_rg_lambda, rg_w_out=m_rg_w_out,
             ffn_w_up=m_ffn_w_up, ffn_conv_w=m_ffn_conv_w, ffn_conv_b=m_ffn_conv_b, ffn_w_down=m_ffn_w_down,
             mix_pre_g=m_mix_pre_g, mix_post_g=m_mix_post_g, ffn_pre_g=m_ffn_pre_g, ffn_post_g=m_ffn_post_g)
    V = dict(attn_w_qkv=v_attn_w_qkv, attn_w_o=v_attn_w_o, rg_w_in=v_rg_w_in, rg_conv_w=v_rg_conv_w, rg_conv_b=v_rg_conv_b,
             rg_w_a=v_rg_w_a, rg_b_a=v_rg_b_a, rg_w_x=v_rg_w_x, rg_b_x=v_rg_b_x, rg_lambda=v_rg_lambda, rg_w_out=v_rg_w_out,
             ffn_w_up=v_ffn_w_up, ffn_conv_w=v_ffn_conv_w, ffn_conv_b=v_ffn_conv_b, ffn_w_down=v_ffn_w_down,
             mix_pre_g=v_mix_pre_g, mix_post_g=v_mix_post_g, ffn_pre_g=v_ffn_pre_g, ffn_post_g=v_ffn_post_g)
    P = _layout_params(_gather_params(W))
    loss_part, grad_x, G = _local_step(x[0], loss_target[0], P)
    loss = lax.psum(loss_part, ("x", "y", "c"))
    grads = _reduce_grads(_assemble_grads(G), W)
    delta, new_m, new_v = _update(W, M, V, grads)
    return (loss, grad_x[None], *[grads[n] for n in _ORDER], *[delta[n] for n in _ORDER],
            *[new_m[n] for n in _ORDER], *[new_v[n] for n in _ORDER])
```

```python
import functools
import math

import jax
import jax.numpy as jnp
from jax import lax
from jax.experimental import pallas as pl
from jax.experimental.pallas import tpu as pltpu

D_MODEL = 1024
N_HEADS = 16
HEAD_DIM = 64
D_RNN = 1344
D_RNN_PAD = 1408
RG_BLOCKS = 16
RG_BLOCK_W = 84
D_FF = 2816
FFN_TILE = 256
NORM_EPS = 1e-6
RG_C = 8.0

ADAM_LR = 0.001
ADAM_B1 = 0.9
ADAM_B2 = 0.999
ADAM_EPS = 1e-08
ADAM_WD = 0.01
ADAM_STEP = 10

LANES = 128
SUBLANES = 8
VMEM_LIMIT = 56 * 1024 * 1024
FLAT_W = 1024
N_CHIPS = 4

F32 = jnp.float32
BF16 = jnp.bfloat16
MESH = pl.DeviceIdType.MESH


def _cp(*sem):
    return pltpu.CompilerParams(dimension_semantics=tuple(sem), vmem_limit_bytes=VMEM_LIMIT)


def _mm(a, b, mode, out_dtype, name, bm=1024, bn=1024, bk=1024, add=None):
    if mode == "nn":
        (M, K), (K2, N) = a.shape, b.shape
    elif mode == "nt":
        (M, K), (N, K2) = a.shape, b.shape
    else:
        (K, M), (K2, N) = a.shape, b.shape
    assert K == K2, (a.shape, b.shape, mode)
    bm, bn, bk = min(bm, M), min(bn, N), min(bk, K)
    assert M % bm == 0 and N % bn == 0 and K % bk == 0, (M, N, K, bm, bn, bk)
    nk = K // bk
    dims = {"nn": (((1,), (0,)), ((), ())), "nt": (((1,), (1,)), ((), ())), "tn": (((0,), (0,)), ((), ()))}[mode]
    has_add = add is not None

    def body(*refs):
        a_ref, b_ref = refs[0], refs[1]
        add_ref = refs[2] if has_add else None
        o_ref = refs[2 + has_add]
        p = lax.dot_general(a_ref[...].astype(BF16), b_ref[...].astype(BF16), dims, preferred_element_type=F32)
        if nk == 1:
            if has_add:
                p = p + add_ref[...]
            o_ref[...] = p.astype(o_ref.dtype)
        else:
            acc = refs[3 + has_add]
            k = pl.program_id(2)

            @pl.when(k == 0)
            def _():
                acc[...] = p

            @pl.when(k > 0)
            def _():
                acc[...] += p

            @pl.when(k == nk - 1)
            def _():
                r = acc[...]
                if has_add:
                    r = r + add_ref[...]
                o_ref[...] = r.astype(o_ref.dtype)

    if mode == "tn":
        a_spec = pl.BlockSpec((bk, bm), lambda i, j, k: (k, i))
    else:
        a_spec = pl.BlockSpec((bm, bk), lambda i, j, k: (i, k))
    if mode == "nt":
        b_spec = pl.BlockSpec((bn, bk), lambda i, j, k: (j, k))
    else:
        b_spec = pl.BlockSpec((bk, bn), lambda i, j, k: (k, j))
    o_spec = pl.BlockSpec((bm, bn), lambda i, j, k: (i, j))
    in_specs = [a_spec, b_spec] + ([o_spec] if has_add else [])
    args = (a, b) + ((add,) if has_add else ())
    return pl.pallas_call(
        body, name=name, grid=(M // bm, N // bn, nk),
        in_specs=in_specs, out_specs=o_spec,
        out_shape=jax.ShapeDtypeStruct((M, N), out_dtype),
        scratch_shapes=[pltpu.VMEM((bm, bn), F32)] if nk > 1 else [],
        compiler_params=_cp("parallel", "parallel", "arbitrary"),
    )(*args)


def _row_block(T, bm=512):
    bm = min(bm, T)
    assert T % bm == 0
    return bm


def _rmsnorm_fwd(x, g, name):
    T, D = x.shape
    bm = _row_block(T)

    def body(x_ref, g_ref, o_ref):
        xv = x_ref[...]
        inv = lax.rsqrt(jnp.mean(xv * xv, axis=-1, keepdims=True) + NORM_EPS)
        o_ref[...] = (xv * inv * g_ref[...]).astype(o_ref.dtype)

    return pl.pallas_call(
        body, name=name, grid=(T // bm,),
        in_specs=[pl.BlockSpec((bm, D), lambda i: (i, 0)), pl.BlockSpec((1, D), lambda i: (0, 0))],
        out_specs=pl.BlockSpec((bm, D), lambda i: (i, 0)),
        out_shape=jax.ShapeDtypeStruct((T, D), BF16),
        compiler_params=_cp("parallel"),
    )(x, g)


def _postnorm_residual(x, h, g, name):
    T, D = x.shape
    bm = _row_block(T)

    def body(x_ref, h_ref, g_ref, o_ref):
        hv = h_ref[...]
        inv = lax.rsqrt(jnp.mean(hv * hv, axis=-1, keepdims=True) + NORM_EPS)
        o_ref[...] = x_ref[...] + hv * inv * g_ref[...]

    return pl.pallas_call(
        body, name=name, grid=(T // bm,),
        in_specs=[pl.BlockSpec((bm, D), lambda i: (i, 0)), pl.BlockSpec((bm, D), lambda i: (i, 0)),
                  pl.BlockSpec((1, D), lambda i: (0, 0))],
        out_specs=pl.BlockSpec((bm, D), lambda i: (i, 0)),
        out_shape=jax.ShapeDtypeStruct((T, D), F32),
        compiler_params=_cp("parallel"),
    )(x, h, g)


def _postnorm_loss(x, h, g, target, name):
    T, D = x.shape
    bm = _row_block(T)

    def body(x_ref, h_ref, g_ref, t_ref, dy_ref, loss_ref):
        hv = h_ref[...]
        inv = lax.rsqrt(jnp.mean(hv * hv, axis=-1, keepdims=True) + NORM_EPS)
        err = x_ref[...] + hv * inv * g_ref[...] - t_ref[...]
        dy_ref[...] = err * (1.0 / D)
        part = jnp.sum(jnp.sum(err * err, axis=-1, keepdims=True), axis=0, keepdims=True) * (0.5 / D)

        @pl.when(pl.program_id(0) == 0)
        def _():
            loss_ref[...] = jnp.zeros_like(loss_ref)

        loss_ref[...] += jnp.broadcast_to(part, loss_ref.shape)

    row = pl.BlockSpec((bm, D), lambda i: (i, 0))
    return pl.pallas_call(
        body, name=name, grid=(T // bm,),
        in_specs=[row, row, pl.BlockSpec((1, D), lambda i: (0, 0)), row],
        out_specs=[row, pl.BlockSpec((1, LANES), lambda i: (0, 0))],
        out_shape=[jax.ShapeDtypeStruct((T, D), F32), jax.ShapeDtypeStruct((1, LANES), F32)],
        compiler_params=_cp("arbitrary"),
    )(x, h, g, target)


def _rmsnorm_bwd(dy, xin, g, name, res=None, out_dtype=F32):
    T, D = xin.shape
    bm = _row_block(T)
    has_res = res is not None

    def body(*refs):
        dy_ref, x_ref, g_ref = refs[:3]
        res_ref = refs[3] if has_res else None
        dx_ref, dg_ref = refs[3 + has_res], refs[4 + has_res]
        xv = x_ref[...]
        dyv = dy_ref[...].astype(F32)
        inv = lax.rsqrt(jnp.mean(xv * xv, axis=-1, keepdims=True) + NORM_EPS)
        xhat = xv * inv
        dyg = dyv * g_ref[...]
        c = jnp.mean(dyg * xhat, axis=-1, keepdims=True)
        dx = inv * (dyg - xhat * c)
        if has_res:
            dx = dx + res_ref[...]
        dx_ref[...] = dx.astype(dx_ref.dtype)

        @pl.when(pl.program_id(0) == 0)
        def _():
            dg_ref[...] = jnp.zeros_like(dg_ref)

        dg_ref[...] += jnp.sum(dyv * xhat, axis=0, keepdims=True)

    row = pl.BlockSpec((bm, D), lambda i: (i, 0))
    vec = pl.BlockSpec((1, D), lambda i: (0, 0))
    args = (dy, xin, g) + ((res,) if has_res else ())
    return pl.pallas_call(
        body, name=name, grid=(T // bm,),
        in_specs=[row, row, vec] + ([row] if has_res else []),
        out_specs=[row, vec],
        out_shape=[jax.ShapeDtypeStruct((T, D), out_dtype), jax.ShapeDtypeStruct((1, D), F32)],
        compiler_params=_cp("arbitrary"),
    )(*args)


_GELU_C = math.sqrt(2.0 / math.pi)


def _gelu(x):
    return 0.5 * x * (1.0 + jnp.tanh(_GELU_C * (x + 0.044715 * x * x * x)))


def _gelu_and_grad(x):
    t = jnp.tanh(_GELU_C * (x + 0.044715 * x * x * x))
    val = 0.5 * x * (1.0 + t)
    grad = 0.5 * (1.0 + t) + 0.5 * x * (1.0 - t * t) * _GELU_C * (1.0 + 3.0 * 0.044715 * x * x)
    return val, grad


def _sigmoid(x):
    return 1.0 / (1.0 + jnp.exp(-x))


def _shift_down(x, halo_prev, k):
    xe = jnp.concatenate([halo_prev, x], axis=0)
    return pltpu.roll(xe, k, 0)[SUBLANES:]


def _shift_up(x, halo_next, k):
    xe = jnp.concatenate([x, halo_next], axis=0)
    return pltpu.roll(xe, xe.shape[0] - k, 0)[:x.shape[0]]


def _halo_specs(T, bm, bc):
    r = bm // SUBLANES
    last = T // SUBLANES - 1
    prev = pl.BlockSpec((SUBLANES, bc), lambda j, i: (jnp.maximum(i * r - 1, 0), j))
    nxt = pl.BlockSpec((SUBLANES, bc), lambda j, i: (jnp.minimum((i + 1) * r, last), j))
    return prev, nxt


def _causal_conv(x, halo, w_ref, b_ref, width):
    out = x * w_ref[width - 1:width, :] + b_ref[...]
    for k in range(width - 1):
        out = out + _shift_down(x, halo, width - 1 - k) * w_ref[k:k + 1, :]
    return out


def _ffn_act_fwd(u_pre, conv_w, conv_b, name):
    T, C2 = u_pre.shape
    bm = _row_block(T, 1024)
    bc = 2 * FFN_TILE
    nj = C2 // bc
    prev, _ = _halo_specs(T, bm, bc)

    def body(u_ref, h_ref, w_ref, b_ref, o_ref):
        i = pl.program_id(1)
        halo = jnp.where(i > 0, h_ref[...], 0.0)
        u = _causal_conv(u_ref[...], halo, w_ref, b_ref, 3)
        o_ref[...] = (_gelu(u[:, :FFN_TILE]) * u[:, FFN_TILE:]).astype(o_ref.dtype)

    return pl.pallas_call(
        body, name=name, grid=(nj, T // bm),
        in_specs=[pl.BlockSpec((bm, bc), lambda j, i: (i, j)), prev,
                  pl.BlockSpec((SUBLANES, bc), lambda j, i: (0, j)), pl.BlockSpec((1, bc), lambda j, i: (0, j))],
        out_specs=pl.BlockSpec((bm, FFN_TILE), lambda j, i: (i, j)),
        out_shape=jax.ShapeDtypeStruct((T, C2 // 2), BF16),
        compiler_params=_cp("parallel", "parallel"),
    )(u_pre, u_pre, conv_w, conv_b)


def _ffn_act_bwd(u_pre, conv_w, conv_b, d_act, name):
    T, C2 = u_pre.shape
    bm = _row_block(T, 1024)
    bc = 2 * FFN_TILE
    nj = C2 // bc
    nt = T // bm
    r = bm // SUBLANES

    def body(u_ref, h_ref, w_ref, b_ref, da_ref, dx_ref, st_ref, later):
        i = pl.program_id(1)

        @pl.when(i == 0)
        def _():
            st_ref[...] = jnp.zeros_like(st_ref)
            later[...] = jnp.zeros_like(later)

        halo = jnp.where(i < nt - 1, h_ref[...], 0.0)
        x = u_ref[...]
        xm1 = _shift_down(x, halo, 1)
        xm2 = _shift_down(x, halo, 2)
        u = xm2 * w_ref[0:1, :] + xm1 * w_ref[1:2, :] + x * w_ref[2:3, :] + b_ref[...]
        gv, gg = _gelu_and_grad(u[:, :FFN_TILE])
        da = da_ref[...]
        du = jnp.concatenate([da * u[:, FFN_TILE:] * gg, da * gv], axis=1)
        nxt = later[...]
        dx = du * w_ref[2:3, :] + _shift_up(du, nxt, 1) * w_ref[1:2, :] + _shift_up(du, nxt, 2) * w_ref[0:1, :]
        dx_ref[...] = dx.astype(dx_ref.dtype)
        later[...] = du[:SUBLANES, :]
        rows = [jnp.sum(du * xm2, axis=0, keepdims=True), jnp.sum(du * xm1, axis=0, keepdims=True),
                jnp.sum(du * x, axis=0, keepdims=True), jnp.sum(du, axis=0, keepdims=True),
                jnp.zeros((4, bc), F32)]
        st_ref[...] += jnp.concatenate(rows, axis=0)

    rows_spec = pl.BlockSpec((bm, bc), lambda j, i: (nt - 1 - i, j))
    prev = pl.BlockSpec((SUBLANES, bc), lambda j, i: (jnp.maximum((nt - 1 - i) * r - 1, 0), j))
    return pl.pallas_call(
        body, name=name, grid=(nj, nt),
        in_specs=[rows_spec, prev,
                  pl.BlockSpec((SUBLANES, bc), lambda j, i: (0, j)), pl.BlockSpec((1, bc), lambda j, i: (0, j)),
                  pl.BlockSpec((bm, FFN_TILE), lambda j, i: (nt - 1 - i, j))],
        out_specs=[rows_spec, pl.BlockSpec((SUBLANES, bc), lambda j, i: (0, j))],
        out_shape=[jax.ShapeDtypeStruct((T, C2), BF16), jax.ShapeDtypeStruct((SUBLANES, C2), F32)],
        scratch_shapes=[pltpu.VMEM((SUBLANES, bc), F32)],
        compiler_params=_cp("parallel", "arbitrary"),
    )(u_pre, u_pre, conv_w, conv_b, d_act)


def _conv_transpose(dy, conv_w, width, name, bc):
    T, C = dy.shape
    bm = _row_block(T, 512)
    _, nxt = _halo_specs(T, bm, bc)
    nt = T // bm

    def body(y_ref, h_ref, w_ref, o_ref):
        i = pl.program_id(1)
        halo = jnp.where(i < nt - 1, h_ref[...], 0.0)
        y = y_ref[...]
        out = y * w_ref[width - 1:width, :]
        for k in range(width - 1):
            out = out + _shift_up(y, halo, width - 1 - k) * w_ref[k:k + 1, :]
        o_ref[...] = out.astype(o_ref.dtype)

    return pl.pallas_call(
        body, name=name, grid=(C // bc, nt),
        in_specs=[pl.BlockSpec((bm, bc), lambda j, i: (i, j)), nxt, pl.BlockSpec((SUBLANES, bc), lambda j, i: (0, j))],
        out_specs=pl.BlockSpec((bm, bc), lambda j, i: (i, j)),
        out_shape=jax.ShapeDtypeStruct((T, C), BF16),
        compiler_params=_cp("parallel", "parallel"),
    )(dy, dy, conv_w)


ATT_BK = 128


def _split_dot(x, uu):
    hi = x.astype(BF16)
    lo = (x - hi.astype(F32)).astype(BF16)
    return jnp.dot(jnp.concatenate([hi, lo], axis=1), uu, preferred_element_type=F32)


def _log_sigmoid(z):
    return jnp.minimum(z, 0.0) - jnp.log(1.0 + jnp.exp(-jnp.abs(z)))


ATT_TQ = 512
ATT_KS = 512
ATT_SUB = ATT_KS // ATT_BK


def _suffix_consts():
    r = lax.broadcasted_iota(jnp.int32, (ATT_BK, 2 * ATT_BK), 0)
    c = lax.broadcasted_iota(jnp.int32, (ATT_BK, 2 * ATT_BK), 1)
    m = jnp.where((r > c) | (c >= ATT_BK), 1.0, 0.0).astype(BF16)
    return jnp.concatenate([m, m], axis=0)


def _suffix_sums(x, uu):
    ext = _split_dot(x, uu)
    return ext[:, :ATT_BK], ext[:, ATT_BK:]


def _masked_heads(x_f32, first):
    return (jnp.where(first, x_f32, 0.0).astype(BF16), jnp.where(first, 0.0, x_f32).astype(BF16))


def _sb_attn_fwd(qkv, name):
    T = qkv.shape[0]
    tq, ks = min(ATT_TQ, T), min(ATT_KS, T)
    assert T % tq == 0 and T % ks == 0 and ks % tq == 0
    nq = T // tq
    nsub = ks // ATT_BK
    nhp = N_HEADS // 2
    nt = (((1,), (1,)), ((), ()))

    def body(q_ref, k_ref, v_ref, o_ref):
        qi = pl.program_id(1)
        first = lax.broadcasted_iota(jnp.int32, (1, LANES), 1) < HEAD_DIM
        qs = _masked_heads(q_ref[...].astype(F32) * HEAD_DIM ** -0.5, first)
        uu = _suffix_consts()
        tpos = qi * tq + lax.broadcasted_iota(jnp.int32, (tq, ATT_BK), 0)
        col = lax.broadcasted_iota(jnp.int32, (tq, ATT_BK), 1)

        def block(kb, carry, masked):
            off = pl.multiple_of(kb * ks, ks)
            kblk = k_ref[pl.ds(off, ks), :]
            vblk = v_ref[pl.ds(off, ks), :]
            out = []
            for h in range(2):
                acc, run = carry[2 * h], carry[2 * h + 1]
                z = lax.dot_general(qs[h], kblk, nt, preferred_element_type=F32)
                lb = _log_sigmoid(z)
                l1m = lb - z
                ws = [None] * nsub
                for j in reversed(range(nsub)):
                    sl = slice(j * ATT_BK, (j + 1) * ATT_BK)
                    l1 = l1m[:, sl]
                    if masked:
                        causal = (off + j * ATT_BK + col) < tpos
                        l1 = jnp.where(causal, l1, 0.0)
                    cs, tot = _suffix_sums(l1, uu)
                    w = jnp.exp(lb[:, sl] + cs + run)
                    ws[j] = jnp.where(causal, w, 0.0) if masked else w
                    run = run + tot
                w = jnp.concatenate(ws, axis=1)
                acc = acc + jnp.dot(w.astype(BF16), vblk, preferred_element_type=F32)
                out += [acc, run]
            return tuple(out)

        nfull = (qi * tq) // ks
        carry = (jnp.zeros((tq, LANES), F32),) * 4
        carry = block(nfull, carry, True)
        carry = lax.fori_loop(0, nfull, lambda i, c: block(nfull - 1 - i, c, False), carry)
        o_ref[...] = jnp.where(first, carry[0], carry[2])

    return pl.pallas_call(
        body, name=name, grid=(nhp, nq),
        in_specs=[pl.BlockSpec((tq, LANES), lambda p, i: (i, p)),
                  pl.BlockSpec((T, LANES), lambda p, i: (0, nhp + p)),
                  pl.BlockSpec((T, LANES), lambda p, i: (0, 2 * nhp + p))],
        out_specs=pl.BlockSpec((tq, LANES), lambda p, i: (i, p)),
        out_shape=jax.ShapeDtypeStruct((T, D_MODEL), F32),
        compiler_params=_cp("parallel", "parallel"),
    )(qkv, qkv, qkv)


def _sb_attn_bwd(qkv, o, d_o, name):
    T = qkv.shape[0]
    tq, ks = min(ATT_TQ, T), min(ATT_KS, T)
    assert T % tq == 0 and T % ks == 0 and ks % tq == 0
    nq = T // tq
    nks = T // ks
    nsub = ks // ATT_BK
    nhp = N_HEADS // 2
    nt = (((1,), (1,)), ((), ()))

    def body(q_ref, k_ref, v_ref, o_ref, do_ref, dq_ref, dk_ref, dv_ref, dkt_acc, dvt_acc):
        qi = pl.program_id(1)

        @pl.when(qi == 0)
        def _():
            dkt_acc[...] = jnp.zeros_like(dkt_acc)
            dvt_acc[...] = jnp.zeros_like(dvt_acc)

        first = lax.broadcasted_iota(jnp.int32, (1, LANES), 1) < HEAD_DIM
        first_t = lax.broadcasted_iota(jnp.int32, (LANES, 1), 0) < HEAD_DIM
        qf = q_ref[...].astype(F32) * HEAD_DIM ** -0.5
        qs = _masked_heads(qf, first)
        qts = _masked_heads(qf.T, first_t)
        dof = do_ref[...].astype(BF16).astype(F32)
        dos = _masked_heads(dof, first)
        dots = _masked_heads(dof.T, first_t)
        prod = dof * o_ref[...]
        deltas = (jnp.broadcast_to(jnp.sum(jnp.where(first, prod, 0.0), axis=1, keepdims=True), (tq, LANES)),
                  jnp.broadcast_to(jnp.sum(jnp.where(first, 0.0, prod), axis=1, keepdims=True), (tq, LANES)))
        uu = _suffix_consts()
        tpos = qi * tq + lax.broadcasted_iota(jnp.int32, (tq, ATT_BK), 0)
        col = lax.broadcasted_iota(jnp.int32, (tq, ATT_BK), 1)

        def block(kb, carry, masked):
            off = pl.multiple_of(kb * ks, ks)
            kblk = k_ref[pl.ds(off, ks), :]
            vblk = v_ref[pl.ds(off, ks), :]
            out = []
            dkt = jnp.zeros((LANES, ks), F32)
            dvt = jnp.zeros((LANES, ks), F32)
            for h in range(2):
                dq, run, grun = carry[3 * h], carry[3 * h + 1], carry[3 * h + 2]
                z = lax.dot_general(qs[h], kblk, nt, preferred_element_type=F32)
                lb = _log_sigmoid(z)
                l1m = lb - z
                dw = lax.dot_general(dos[h], vblk, nt, preferred_element_type=F32)
                wbs, dzs = [None] * nsub, [None] * nsub
                for j in reversed(range(nsub)):
                    sl = slice(j * ATT_BK, (j + 1) * ATT_BK)
                    l1 = l1m[:, sl]
                    if masked:
                        causal = (off + j * ATT_BK + col) < tpos
                        l1 = jnp.where(causal, l1, 0.0)
                    cs, tot = _suffix_sums(l1, uu)
                    w = jnp.exp(lb[:, sl] + cs + run)
                    if masked:
                        w = jnp.where(causal, w, 0.0)
                    wbs[j] = w.astype(BF16)
                    g = wbs[j].astype(F32) * dw[:, sl]
                    gcs, gtot = _suffix_sums(g, uu)
                    gpre = deltas[h] - grun - gcs - g
                    dz = g - jnp.exp(lb[:, sl]) * (g + gpre)
                    if masked:
                        dz = jnp.where(causal, dz, 0.0)
                    dzs[j] = dz.astype(BF16)
                    run = run + tot
                    grun = grun + gtot
                wb = jnp.concatenate(wbs, axis=1)
                dzb = jnp.concatenate(dzs, axis=1)
                dq = dq + jnp.dot(dzb, kblk, preferred_element_type=F32)
                dkt = dkt + jnp.dot(qts[h], dzb, preferred_element_type=F32)
                dvt = dvt + jnp.dot(dots[h], wb, preferred_element_type=F32)
                out += [dq, run, grun]
            dkt_acc[kb] += dkt
            dvt_acc[kb] += dvt
            return tuple(out)

        nfull = (qi * tq) // ks
        carry = (jnp.zeros((tq, LANES), F32),) * 6
        carry = block(nfull, carry, True)
        carry = lax.fori_loop(0, nfull, lambda i, c: block(nfull - 1 - i, c, False), carry)
        dq_ref[...] = (jnp.where(first, carry[0], carry[3]) * HEAD_DIM ** -0.5).astype(dq_ref.dtype)

        @pl.when(qi == nq - 1)
        def _():
            for kb in range(nks):
                dk_ref[kb * ks:(kb + 1) * ks, :] = dkt_acc[kb].T.astype(dk_ref.dtype)
                dv_ref[kb * ks:(kb + 1) * ks, :] = dvt_acc[kb].T.astype(dv_ref.dtype)

    qspec = pl.BlockSpec((tq, LANES), lambda p, i: (i, p))
    full = pl.BlockSpec((T, LANES), lambda p, i: (0, p))
    return pl.pallas_call(
        body, name=name, grid=(nhp, nq),
        in_specs=[qspec, pl.BlockSpec((T, LANES), lambda p, i: (0, nhp + p)),
                  pl.BlockSpec((T, LANES), lambda p, i: (0, 2 * nhp + p)), qspec, qspec],
        out_specs=[qspec, full, full],
        out_shape=[jax.ShapeDtypeStruct((T, D_MODEL), BF16)] * 3,
        scratch_shapes=[pltpu.VMEM((nks, LANES, ks), F32), pltpu.VMEM((nks, LANES, ks), F32)],
        compiler_params=_cp("parallel", "arbitrary"),
    )(qkv, qkv, qkv, o, d_o)


def _rg_conv_fwd(x, conv_w, conv_b, name):
    T, C = x.shape
    bm = _row_block(T, 512)
    bc = C
    prev, _ = _halo_specs(T, bm, bc)

    def body(x_ref, h_ref, w_ref, b_ref, o_ref):
        halo = jnp.where(pl.program_id(1) > 0, h_ref[...], 0.0)
        o_ref[...] = _causal_conv(x_ref[...], halo, w_ref, b_ref, 4)

    return pl.pallas_call(
        body, name=name, grid=(C // bc, T // bm),
        in_specs=[pl.BlockSpec((bm, bc), lambda j, i: (i, j)), prev,
                  pl.BlockSpec((SUBLANES, bc), lambda j, i: (0, j)), pl.BlockSpec((1, bc), lambda j, i: (0, j))],
        out_specs=pl.BlockSpec((bm, bc), lambda j, i: (i, j)),
        out_shape=jax.ShapeDtypeStruct((T, C), F32),
        compiler_params=_cp("parallel", "parallel"),
    )(x, x, conv_w, conv_b)


def _rg_gate_math(ra, rx, rec, vec_ref):
    r = _sigmoid(ra + vec_ref[0:1, :])
    ig = _sigmoid(rx + vec_ref[1:2, :])
    lam = vec_ref[2:3, :]
    sp = jnp.maximum(-lam, 0.0) + jnp.log(1.0 + jnp.exp(-jnp.abs(lam)))
    log_a = -RG_C * r * sp
    a = jnp.exp(log_a)
    th = jnp.tanh(log_a)
    mult = jnp.sqrt(-2.0 * th / (1.0 - th))
    return r, ig, sp, a, mult


def _rg_gates_fwd(rax, rec, vecs, name):
    T, C = rec.shape
    bm = _row_block(T, 512)

    def body(ra_ref, rx_ref, rec_ref, vec_ref, a_ref, u_ref):
        recv = rec_ref[...]
        _, ig, _, a, mult = _rg_gate_math(ra_ref[...], rx_ref[...], recv, vec_ref)
        a_ref[...] = a
        u_ref[...] = mult * ig * recv

    blk = pl.BlockSpec((bm, C), lambda i: (i, 0))
    return pl.pallas_call(
        body, name=name, grid=(T // bm,),
        in_specs=[blk, pl.BlockSpec((bm, C), lambda i: (i, 1)), blk, pl.BlockSpec((SUBLANES, C), lambda i: (0, 0))],
        out_specs=[blk, blk],
        out_shape=[jax.ShapeDtypeStruct((T, C), F32)] * 2,
        compiler_params=_cp("parallel"),
    )(rax, rax, rec, vecs)


def _linear_scan(a, b, name, reverse=False, shift_coef=False):
    T, C = a.shape
    bm = _row_block(T, 256)
    bc = C
    nt = T // bm
    _, nxt = _halo_specs(T, bm, bc)
    groups = bm // SUBLANES

    def body(*refs):
        if shift_coef:
            a_ref, an_ref, b_ref, h_ref, carry = refs
        else:
            a_ref, b_ref, h_ref, carry = refs
        i = pl.program_id(1)

        @pl.when(i == 0)
        def _():
            carry[...] = jnp.zeros_like(carry)

        coef = a_ref[...]
        if shift_coef:
            coef = _shift_up(coef, an_ref[...], 1)
        val = b_ref[...]
        sub = jnp.bitwise_and(lax.broadcasted_iota(jnp.int32, (bm, bc), 0), SUBLANES - 1)
        for s in (1, 2, 4):
            if reverse:
                keep = sub < SUBLANES - s
                cs, vs = pltpu.roll(coef, bm - s, 0), pltpu.roll(val, bm - s, 0)
            else:
                keep = sub >= s
                cs, vs = pltpu.roll(coef, s, 0), pltpu.roll(val, s, 0)
            val = jnp.where(keep, coef * vs + val, val)
            coef = jnp.where(keep, coef * cs, coef)
        edge = carry[0:1, :]
        pieces = [None] * groups
        for g in (reversed(range(groups)) if reverse else range(groups)):
            rows = slice(g * SUBLANES, (g + 1) * SUBLANES)
            hg = val[rows] + coef[rows] * edge
            pieces[g] = hg
            edge = hg[0:1, :] if reverse else hg[SUBLANES - 1:SUBLANES, :]
        h_ref[...] = jnp.concatenate(pieces, axis=0)
        carry[...] = jnp.broadcast_to(edge, carry.shape)

    if reverse:
        blk = pl.BlockSpec((bm, bc), lambda j, i: (nt - 1 - i, j))
        r = bm // SUBLANES
        last = T // SUBLANES - 1
        nxt = pl.BlockSpec((SUBLANES, bc), lambda j, i: (jnp.minimum((nt - i) * r, last), j))
    else:
        blk = pl.BlockSpec((bm, bc), lambda j, i: (i, j))
    in_specs = [blk] + ([nxt] if shift_coef else []) + [blk]
    args = (a,) + ((a,) if shift_coef else ()) + (b,)
    return pl.pallas_call(
        body, name=name, grid=(C // bc, nt),
        in_specs=in_specs, out_specs=blk,
        out_shape=jax.ShapeDtypeStruct((T, C), F32),
        scratch_shapes=[pltpu.VMEM((SUBLANES, bc), F32)],
        compiler_params=_cp("parallel", "arbitrary"),
    )(*args)


def _rg_out_fwd(gate_branch, h, name):
    T, C = h.shape
    bm = _row_block(T, 512)

    def body(g_ref, h_ref, o_ref):
        o_ref[...] = (_gelu(g_ref[...]) * h_ref[...]).astype(o_ref.dtype)

    blk = pl.BlockSpec((bm, C), lambda i: (i, 0))
    return pl.pallas_call(
        body, name=name, grid=(T // bm,), in_specs=[blk, blk], out_specs=blk,
        out_shape=jax.ShapeDtypeStruct((T, C), BF16), compiler_params=_cp("parallel"),
    )(gate_branch, h)


def _rg_out_bwd(d_gh, gate_branch, h, name):
    T, C = h.shape
    bm = _row_block(T, 512)

    def body(d_ref, g_ref, h_ref, dg_ref, dh_ref):
        gv, gg = _gelu_and_grad(g_ref[...])
        d = d_ref[...]
        dg_ref[...] = (d * h_ref[...] * gg).astype(dg_ref.dtype)
        dh_ref[...] = d * gv

    blk = pl.BlockSpec((bm, C), lambda i: (i, 0))
    return pl.pallas_call(
        body, name=name, grid=(T // bm,), in_specs=[blk, blk, blk], out_specs=[blk, blk],
        out_shape=[jax.ShapeDtypeStruct((T, C), BF16), jax.ShapeDtypeStruct((T, C), F32)],
        compiler_params=_cp("parallel"),
    )(d_gh, gate_branch, h)


def _rg_gates_bwd(dhs, h, rax, rec, vecs, name):
    T, C = rec.shape
    bm = _row_block(T, 1024)
    bc = LANES
    prev, _ = _halo_specs(T, bm, bc)
    nc = C // bc

    def body(d_ref, h_ref, hh_ref, ra_ref, rx_ref, rec_ref, vec_ref, dra_ref, drx_ref, drec_ref, st_ref):
        i = pl.program_id(1)
        halo = jnp.where(i > 0, hh_ref[...], 0.0)
        recv = rec_ref[...]
        du = d_ref[...]
        r, ig, sp, a, mult = _rg_gate_math(ra_ref[...], rx_ref[...], recv, vec_ref)
        da = du * _shift_down(h_ref[...], halo, 1)
        d_i = du * mult * recv
        drec_ref[...] = du * mult * ig
        d_mult = du * ig * recv
        d_log_a = da * a - d_mult * (a * a) / mult
        d_ra = d_log_a * (-RG_C) * sp * r * (1.0 - r)
        d_rx = d_i * ig * (1.0 - ig)
        dra_ref[...] = d_ra.astype(dra_ref.dtype)
        drx_ref[...] = d_rx.astype(drx_ref.dtype)
        lam = vec_ref[2:3, :]
        d_sp = jnp.sum(d_log_a * (-RG_C) * r, axis=0, keepdims=True)
        rows = [jnp.sum(d_ra, axis=0, keepdims=True), jnp.sum(d_rx, axis=0, keepdims=True),
                d_sp * (_sigmoid(lam) - 1.0), jnp.zeros((5, bc), F32)]

        @pl.when(i == 0)
        def _():
            st_ref[...] = jnp.zeros_like(st_ref)

        st_ref[...] += jnp.concatenate(rows, axis=0)

    blk = pl.BlockSpec((bm, bc), lambda j, i: (i, j))
    blk_x = pl.BlockSpec((bm, bc), lambda j, i: (i, nc + j))
    vec = pl.BlockSpec((SUBLANES, bc), lambda j, i: (0, j))
    d_ra, d_rx, d_rec, stats = pl.pallas_call(
        body, name=name, grid=(nc, T // bm),
        in_specs=[blk, blk, prev, blk, blk_x, blk, vec],
        out_specs=[blk, blk, blk, vec],
        out_shape=[jax.ShapeDtypeStruct((T, C), BF16), jax.ShapeDtypeStruct((T, C), BF16),
                   jax.ShapeDtypeStruct((T, C), F32), jax.ShapeDtypeStruct((SUBLANES, C), F32)],
        compiler_params=_cp("parallel", "arbitrary"),
    )(dhs, h, h, rax, rax, rec, vecs)
    return d_ra, d_rx, d_rec, stats


def _rg_conv_bwd_stats(d_rec, x, name):
    T, C = x.shape
    bm = _row_block(T, 1024)
    bc = LANES
    prev, _ = _halo_specs(T, bm, bc)

    def body(d_ref, x_ref, h_ref, st_ref):
        i = pl.program_id(1)
        halo = jnp.where(i > 0, h_ref[...], 0.0)
        d = d_ref[...]
        xv = x_ref[...]
        rows = [jnp.sum(d * _shift_down(xv, halo, 3 - k), axis=0, keepdims=True) for k in range(3)]
        rows += [jnp.sum(d * xv, axis=0, keepdims=True), jnp.sum(d, axis=0, keepdims=True), jnp.zeros((3, bc), F32)]

        @pl.when(i == 0)
        def _():
            st_ref[...] = jnp.zeros_like(st_ref)

        st_ref[...] += jnp.concatenate(rows, axis=0)

    blk = pl.BlockSpec((bm, bc), lambda j, i: (i, j))
    vec = pl.BlockSpec((SUBLANES, bc), lambda j, i: (0, j))
    return pl.pallas_call(
        body, name=name, grid=(C // bc, T // bm),
        in_specs=[blk, blk, prev], out_specs=vec,
        out_shape=jax.ShapeDtypeStruct((SUBLANES, C), F32),
        compiler_params=_cp("parallel", "arbitrary"),
    )(d_rec, x, x)


def _adamw(w, g, m, v, name):
    R, W = w.shape
    bm = R
    for cand in (512, 256, 128, 64, 32, 16, 8):
        if R % cand == 0:
            bm = cand
            break
    bc1 = 1.0 - ADAM_B1 ** ADAM_STEP
    bc2 = 1.0 - ADAM_B2 ** ADAM_STEP

    def body(w_ref, g_ref, m_ref, v_ref, d_ref, nm_ref, nv_ref):
        gv = g_ref[...]
        nm = ADAM_B1 * m_ref[...] + (1.0 - ADAM_B1) * gv
        nv = ADAM_B2 * v_ref[...] + (1.0 - ADAM_B2) * (gv * gv)
        m_hat = nm / bc1
        v_hat = nv / bc2
        d_ref[...] = -ADAM_LR * (m_hat / (jnp.sqrt(v_hat) + ADAM_EPS) + ADAM_WD * w_ref[...])
        nm_ref[...] = nm
        nv_ref[...] = nv

    blk = pl.BlockSpec((bm, W), lambda i: (i, 0))
    return pl.pallas_call(
        body, name=name, grid=(R // bm,), in_specs=[blk] * 4, out_specs=[blk] * 3,
        out_shape=[jax.ShapeDtypeStruct((R, W), F32)] * 3, compiler_params=_cp("parallel"),
    )(w, g, m, v)


_ANY = pl.BlockSpec(memory_space=pl.ANY)


def _me():
    return lax.axis_index("x"), lax.axis_index("y"), lax.axis_index("c")


def _other_chips(x, y):
    return [(1 - x, y), (x, 1 - y), (1 - x, 1 - y)]


def _remote(src, dst, send_sem, recv_sem, to):
    return pltpu.make_async_remote_copy(src_ref=src, dst_ref=dst, send_sem=send_sem, recv_sem=recv_sem,
                                        device_id=to, device_id_type=MESH)


def _half_rows(c, rows_half, align):
    return pl.ds(pl.multiple_of(c * rows_half, align), rows_half)


def _divisor_block(rows, cap=512):
    if rows <= cap:
        return rows
    best = None
    for cand in range(16, cap + 1, 16):
        if rows % cand == 0:
            best = cand
    assert best is not None, rows
    return best


def _all_gather_chips(shards, name):
    n = len(shards)
    halves = [s.shape[0] // 2 for s in shards]
    for s, rh in zip(shards, halves):
        assert 2 * rh == s.shape[0] and rh % 16 == 0, s.shape

    def body(*refs):
        x_refs, out_refs = refs[:n], refs[n:2 * n]
        send_sems, recv_sems, local_sems = refs[2 * n:]
        x, y, c = _me()
        s_me = 2 * x + y
        chips = _other_chips(x, y)
        started = []
        for a in range(n):
            mine = pltpu.make_async_copy(x_refs[a], out_refs[a].at[s_me], local_sems.at[a])
            mine.start()
            started.append(mine)
        sends = []
        for a in range(n):
            half = _half_rows(c, halves[a], 16)
            for j, (cx, cy) in enumerate(chips):
                cp = _remote(x_refs[a].at[half], out_refs[a].at[s_me, half], send_sems.at[6 * a + j],
                             recv_sems.at[6 * a + j], (cx, cy, c))
                cp.start()
                sends.append(cp)
        for a in range(n):
            half = _half_rows(c, halves[a], 16)
            for j, (cx, cy) in enumerate(chips):
                landed = out_refs[a].at[2 * cx + cy, half]
                _remote(landed, landed, send_sems.at[6 * a + j], recv_sems.at[6 * a + j], (cx, cy, c)).wait_recv()
                fwd = _remote(landed, landed, send_sems.at[6 * a + 3 + j], recv_sems.at[6 * a + 3 + j], (x, y, 1 - c))
                fwd.start()
                sends.append(fwd)
        for a in range(n):
            other_half = _half_rows(1 - c, halves[a], 16)
            for j, (cx, cy) in enumerate(chips):
                landed = out_refs[a].at[2 * cx + cy, other_half]
                _remote(landed, landed, send_sems.at[6 * a + 3 + j], recv_sems.at[6 * a + 3 + j], (x, y, 1 - c)).wait_recv()
        for cp in sends:
            cp.wait_send()
        for mine in started:
            mine.wait()

    return pl.pallas_call(
        body, name=name, in_specs=[_ANY] * n, out_specs=[_ANY] * n,
        out_shape=[jax.ShapeDtypeStruct((N_CHIPS,) + s.shape, s.dtype) for s in shards],
        scratch_shapes=[pltpu.SemaphoreType.DMA((6 * n,)), pltpu.SemaphoreType.DMA((6 * n,)), pltpu.SemaphoreType.DMA((n,))],
    )(*shards)


def _pair_send_other_halves(gs, name):
    n = len(gs)
    halves = [g.shape[1] // 2 for g in gs]
    for g, rh in zip(gs, halves):
        assert 2 * rh == g.shape[1] and rh % 8 == 0, g.shape

    def body(*refs):
        g_refs, land_refs = refs[:n], refs[n:2 * n]
        send_sems, recv_sems = refs[2 * n:]
        x, y, c = _me()
        copies = []
        for a in range(n):
            src = g_refs[a].at[:, _half_rows(1 - c, halves[a], 8), :]
            cp = _remote(src, land_refs[a], send_sems.at[a], recv_sems.at[a], (x, y, 1 - c))
            cp.start()
            copies.append(cp)
        for cp in copies:
            cp.wait()

    return pl.pallas_call(
        body, name=name, in_specs=[_ANY] * n, out_specs=[_ANY] * n,
        out_shape=[jax.ShapeDtypeStruct((g.shape[0], rh, g.shape[2]), g.dtype) for g, rh in zip(gs, halves)],
        scratch_shapes=[pltpu.SemaphoreType.DMA((n,)), pltpu.SemaphoreType.DMA((n,))],
    )(*gs)


def _add_own_half(g, landed, name, out_dtype):
    n, R, W = g.shape
    Rh = R // 2
    bm = _divisor_block(Rh)
    nb = Rh // bm
    c_arr = lax.axis_index("c").astype(jnp.int32).reshape((1,))

    def body(c_ref, g_ref, l_ref, o_ref):
        o_ref[...] = (g_ref[...] + l_ref[...]).astype(o_ref.dtype)

    grid_spec = pltpu.PrefetchScalarGridSpec(
        num_scalar_prefetch=1, grid=(n, nb),
        in_specs=[pl.BlockSpec((1, bm, W), lambda s, i, c_ref: (s, c_ref[0] * nb + i, 0)),
                  pl.BlockSpec((1, bm, W), lambda s, i, c_ref: (s, i, 0))],
        out_specs=pl.BlockSpec((1, bm, W), lambda s, i, c_ref: (s, i, 0)))
    return pl.pallas_call(
        body, name=name, grid_spec=grid_spec, out_shape=jax.ShapeDtypeStruct((n, Rh, W), out_dtype),
        compiler_params=_cp("parallel", "parallel"),
    )(c_arr, g, landed)


def _scatter_to_chips(ps, name):
    n = len(ps)

    def body(*refs):
        p_refs, land_refs = refs[:n], refs[n:2 * n]
        send_sems, recv_sems, local_sems = refs[2 * n:]
        x, y, c = _me()
        s_me = 2 * x + y
        chips = _other_chips(x, y)
        started, sends = [], []
        for a in range(n):
            mine = pltpu.make_async_copy(p_refs[a].at[s_me], land_refs[a].at[s_me], local_sems.at[a])
            mine.start()
            started.append(mine)
        for a in range(n):
            for j, (cx, cy) in enumerate(chips):
                cp = _remote(p_refs[a].at[2 * cx + cy], land_refs[a].at[s_me], send_sems.at[3 * a + j],
                             recv_sems.at[3 * a + j], (cx, cy, c))
                cp.start()
                sends.append(cp)
        for a in range(n):
            for j, (cx, cy) in enumerate(chips):
                slot = land_refs[a].at[2 * cx + cy]
                _remote(slot, slot, send_sems.at[3 * a + j], recv_sems.at[3 * a + j], (cx, cy, c)).wait_recv()
        for cp in sends:
            cp.wait_send()
        for mine in started:
            mine.wait()

    return pl.pallas_call(
        body, name=name, in_specs=[_ANY] * n, out_specs=[_ANY] * n,
        out_shape=[jax.ShapeDtypeStruct(p.shape, p.dtype) for p in ps],
        scratch_shapes=[pltpu.SemaphoreType.DMA((3 * n,)), pltpu.SemaphoreType.DMA((3 * n,)), pltpu.SemaphoreType.DMA((n,))],
    )(*ps)


def _sum_slots(l, name):
    n, Rh, W = l.shape
    bm = _divisor_block(Rh, 256)

    def body(l_ref, o_ref):
        o_ref[...] = ((l_ref[0].astype(F32) + l_ref[1].astype(F32)) + l_ref[2].astype(F32)) + l_ref[3].astype(F32)

    return pl.pallas_call(
        body, name=name, grid=(Rh // bm,),
        in_specs=[pl.BlockSpec((n, bm, W), lambda i: (0, i, 0))], out_specs=pl.BlockSpec((bm, W), lambda i: (i, 0)),
        out_shape=jax.ShapeDtypeStruct((Rh, W), F32), compiler_params=_cp("parallel"),
    )(l)


def _pair_gather(fs, name):
    n = len(fs)

    def body(*refs):
        f_refs, out_refs = refs[:n], refs[n:2 * n]
        send_sems, recv_sems, local_sems = refs[2 * n:]
        x, y, c = _me()
        started, sends = [], []
        for a in range(n):
            mine = pltpu.make_async_copy(f_refs[a], out_refs[a].at[c], local_sems.at[a])
            mine.start()
            started.append(mine)
            cp = _remote(f_refs[a], out_refs[a].at[c], send_sems.at[a], recv_sems.at[a], (x, y, 1 - c))
            cp.start()
            sends.append(cp)
        for a in range(n):
            sib = out_refs[a].at[1 - c]
            _remote(sib, sib, send_sems.at[a], recv_sems.at[a], (x, y, 1 - c)).wait_recv()
        for cp in sends:
            cp.wait_send()
        for mine in started:
            mine.wait()

    return pl.pallas_call(
        body, name=name, in_specs=[_ANY] * n, out_specs=[_ANY] * n,
        out_shape=[jax.ShapeDtypeStruct((2,) + f.shape, f.dtype) for f in fs],
        scratch_shapes=[pltpu.SemaphoreType.DMA((n,)), pltpu.SemaphoreType.DMA((n,)), pltpu.SemaphoreType.DMA((n,))],
    )(*fs)


def _pack(arrays, rows_multiple, dtype):
    flat = jnp.concatenate([a.reshape(-1).astype(dtype) for a in arrays])
    n = flat.shape[0]
    rows = -(-n // FLAT_W)
    rows = -(-rows // rows_multiple) * rows_multiple
    return jnp.pad(flat, (0, rows * FLAT_W - n)).reshape(rows, FLAT_W)


def _unpack(flat, shapes):
    flat = flat.reshape(-1)
    out, off = [], 0
    for shp in shapes:
        n = math.prod(shp)
        out.append(flat[off:off + n].reshape(shp))
        off += n
    return out


def _pad_cols(a, width):
    return jnp.pad(a, [(0, 0)] * (a.ndim - 1) + [(0, width - a.shape[-1])])


def _rows8(a):
    return jnp.pad(a, ((0, SUBLANES - a.shape[0]), (0, 0)))


def _ffn_interleave(a):
    lead = a.shape[:-1]
    n = D_FF // FFN_TILE
    return jnp.swapaxes(a.reshape(lead + (2, n, FFN_TILE)), -3, -2).reshape(lead + (2 * D_FF,))


def _ffn_deinterleave(a):
    lead = a.shape[:-1]
    n = D_FF // FFN_TILE
    return jnp.swapaxes(a.reshape(lead + (n, 2, FFN_TILE)), -3, -2).reshape(lead + (2 * D_FF,))


def _block_diag_dense(w):
    eye = jnp.eye(RG_BLOCKS, dtype=w.dtype)
    dense = jnp.einsum("ncd,nm->ncmd", w, eye).reshape(D_RNN, D_RNN)
    return jnp.pad(dense, ((0, D_RNN_PAD - D_RNN), (0, D_RNN_PAD - D_RNN)))


def _block_diag_extract(dense):
    d = dense[:D_RNN, :D_RNN].reshape(RG_BLOCKS, RG_BLOCK_W, RG_BLOCKS, RG_BLOCK_W)
    return jnp.stack([d[n, :, n, :] for n in range(RG_BLOCKS)])


def _ffn_fwd(x, l, P, tag):
    n = _rmsnorm_fwd(x, P["ffn_pre_g"][l], f"{tag}_prenorm")
    u_pre = _mm(n, P["ffn_w_up"][l], "nn", F32, f"{tag}_up", bn=2 * FFN_TILE * 1)
    act = _ffn_act_fwd(u_pre, P["ffn_conv_w"][l], P["ffn_conv_b"][l], f"{tag}_act")
    f = _mm(act, P["ffn_w_down"][l], "nn", F32, f"{tag}_down", bk=D_FF)
    return f, (n, u_pre, act)


def _ffn_bwd(df, x, saved, l, P, tag):
    n, u_pre, act = saved
    g_down = _mm(act, df, "tn", F32, f"{tag}_dwdown", bm=D_FF // 2, bk=1024)
    d_act = _mm(df, P["ffn_w_down"][l], "nt", F32, f"{tag}_dact", bn=D_FF // 2)
    du_pre, stats = _ffn_act_bwd(u_pre, P["ffn_conv_w"][l], P["ffn_conv_b"][l], d_act, f"{tag}_actbwd")
    g_up = _mm(n, du_pre, "tn", F32, f"{tag}_dwup", bn=2 * D_FF // 4, bk=1024)
    dn = _mm(du_pre, P["ffn_w_up"][l], "nt", F32, f"{tag}_dn", bk=2 * D_FF // 4)
    return dn, dict(ffn_w_down=g_down, ffn_w_up=g_up, ffn_conv_w=stats[0:3], ffn_conv_b=stats[3])


def _local_step(x, target, P):
    G = {}
    n0 = _rmsnorm_fwd(x, P["mix_pre_g"][0], "l0_prenorm")
    qkv = _mm(n0, P["attn_w_qkv"], "nn", BF16, "l0_qkv")
    o = _sb_attn_fwd(qkv, "l0_attn")
    a0 = _mm(o, P["attn_w_o"], "nn", F32, "l0_wo")
    x1 = _postnorm_residual(x, a0, P["mix_post_g"][0], "l0_postnorm")
    f0, ffn0 = _ffn_fwd(x1, 0, P, "f0")
    x2 = _postnorm_residual(x1, f0, P["ffn_post_g"][0], "f0_postnorm")
    n1 = _rmsnorm_fwd(x2, P["mix_pre_g"][1], "l1_prenorm")
    gate_b = _mm(n1, P["rg_w_in_gate"], "nn", F32, "l1_in_gate", bn=D_RNN_PAD)
    rec_b = _mm(n1, P["rg_w_in_rec"], "nn", F32, "l1_in_rec", bn=D_RNN_PAD)
    rec = _rg_conv_fwd(rec_b, P["rg_conv_w"], P["rg_conv_b"], "l1_conv")
    rax = _mm(rec, P["rg_w_ax"], "nn", F32, "l1_gates_mm", bn=D_RNN_PAD, bk=D_RNN_PAD)
    a, u = _rg_gates_fwd(rax, rec, P["rg_vecs"], "l1_gates")
    h = _linear_scan(a, u, "l1_scan")
    gh = _rg_out_fwd(gate_b, h, "l1_gate_out")
    y1 = _mm(gh, P["rg_w_out"], "nn", F32, "l1_wout", bk=D_RNN_PAD)
    x3 = _postnorm_residual(x2, y1, P["mix_post_g"][1], "l1_postnorm")
    f1, ffn1 = _ffn_fwd(x3, 1, P, "f1")
    dx, loss = _postnorm_loss(x3, f1, P["ffn_post_g"][1], target, "loss")

    df, G["ffn_post_g1"] = _rmsnorm_bwd(dx, f1, P["ffn_post_g"][1], "f1_postnorm_bwd", out_dtype=BF16)
    dn, g = _ffn_bwd(df, x3, ffn1, 1, P, "f1")
    G.update({k + "1": v for k, v in g.items()})
    dx, G["ffn_pre_g1"] = _rmsnorm_bwd(dn, x3, P["ffn_pre_g"][1], "f1_prenorm_bwd", res=dx)
    dy, G["mix_post_g1"] = _rmsnorm_bwd(dx, y1, P["mix_post_g"][1], "l1_postnorm_bwd", out_dtype=BF16)
    G["rg_w_out"] = _mm(gh, dy, "tn", F32, "l1_dwout", bm=D_RNN_PAD, bk=1024)
    d_gh = _mm(dy, P["rg_w_out"], "nt", F32, "l1_dgh", bn=D_RNN_PAD)
    d_gate, d_h = _rg_out_bwd(d_gh, gate_b, h, "l1_gate_out_bwd")
    dhs = _linear_scan(a, d_h, "l1_scan_bwd", reverse=True, shift_coef=True)
    d_ra, d_rx, d_rec_direct, st = _rg_gates_bwd(dhs, h, rax, rec, P["rg_vecs"], "l1_gates_bwd")
    G["rg_b_a"], G["rg_b_x"], G["rg_lambda"] = st[0], st[1], st[2]
    d_rax = jnp.concatenate([d_ra, d_rx], axis=1)
    G["rg_w_ax"] = _mm(rec, d_rax, "tn", F32, "l1_dwax", bm=D_RNN_PAD, bn=D_RNN_PAD, bk=1024)
    d_rec = _mm(d_rax, P["rg_w_ax"], "nt", F32, "l1_drec", bm=512, bn=D_RNN_PAD, bk=2 * D_RNN_PAD, add=d_rec_direct)
    st = _rg_conv_bwd_stats(d_rec, rec_b, "l1_conv_stats")
    G["rg_conv_w"], G["rg_conv_b"] = st[0:4], st[4]
    d_rec_b = _conv_transpose(d_rec, P["rg_conv_w"], 4, "l1_convT", D_RNN_PAD)
    G["rg_w_in_gate"] = _mm(n1, d_gate, "tn", F32, "l1_dwin_gate", bn=D_RNN_PAD, bk=1024)
    G["rg_w_in_rec"] = _mm(n1, d_rec_b, "tn", F32, "l1_dwin_rec", bn=D_RNN_PAD, bk=1024)
    dn = _mm(d_gate, P["rg_w_in_gate"], "nt", F32, "l1_dn_gate", bk=D_RNN_PAD)
    dn = _mm(d_rec_b, P["rg_w_in_rec"], "nt", F32, "l1_dn_rec", bk=D_RNN_PAD, add=dn)
    dx, G["mix_pre_g1"] = _rmsnorm_bwd(dn, x2, P["mix_pre_g"][1], "l1_prenorm_bwd", res=dx)
    df, G["ffn_post_g0"] = _rmsnorm_bwd(dx, f0, P["ffn_post_g"][0], "f0_postnorm_bwd", out_dtype=BF16)
    dn, g = _ffn_bwd(df, x1, ffn0, 0, P, "f0")
    G.update({k + "0": v for k, v in g.items()})
    dx, G["ffn_pre_g0"] = _rmsnorm_bwd(dn, x1, P["ffn_pre_g"][0], "f0_prenorm_bwd", res=dx)
    da, G["mix_post_g0"] = _rmsnorm_bwd(dx, a0, P["mix_post_g"][0], "l0_postnorm_bwd", out_dtype=BF16)
    G["attn_w_o"] = _mm(o, da, "tn", F32, "l0_dwo", bk=1024)
    d_o = _mm(da, P["attn_w_o"], "nt", F32, "l0_do")
    dq, dk, dv = _sb_attn_bwd(qkv, o, d_o, "l0_attn_bwd")
    d_qkv = jnp.concatenate([dq, dk, dv], axis=1)
    G["attn_w_qkv"] = _mm(n0, d_qkv, "tn", F32, "l0_dwqkv", bk=1024)
    dn = _mm(d_qkv, P["attn_w_qkv"], "nt", F32, "l0_dn")
    dx, G["mix_pre_g0"] = _rmsnorm_bwd(dn, x, P["mix_pre_g"][0], "l0_prenorm_bwd", res=dx)
    return loss[0, 0], dx, G


_BIG = [("attn_w_qkv", 2), ("attn_w_o", 1), ("rg_w_in", 2), ("rg_w_out", 1), ("ffn_w_up", 2), ("ffn_w_down", 1)]
_SMALL_SHARDED = [("rg_conv_w", 2), ("rg_conv_b", 1), ("rg_b_a", 1), ("rg_b_x", 1), ("rg_lambda", 1), ("ffn_conv_w", 2)]
_REPLICATED = ["rg_w_a", "rg_w_x", "ffn_conv_b", "mix_pre_g", "mix_post_g", "ffn_pre_g", "ffn_post_g"]
_ORDER = ["attn_w_qkv", "attn_w_o", "rg_w_in", "rg_conv_w", "rg_conv_b", "rg_w_a", "rg_b_a", "rg_w_x", "rg_b_x",
          "rg_lambda", "rg_w_out", "ffn_w_up", "ffn_conv_w", "ffn_conv_b", "ffn_w_down", "mix_pre_g", "mix_post_g",
          "ffn_pre_g", "ffn_post_g"]


def _join(gathered, shapes, axes):
    per_chip = [_unpack(gathered[s], shapes) for s in range(N_CHIPS)]
    return [jnp.concatenate([per_chip[s][i] for s in range(N_CHIPS)], axis=ax) for i, ax in enumerate(axes)]


def _split(full, axis):
    return jnp.split(full, N_CHIPS, axis=axis)


RG_OUT_SHARD_ROWS = D_RNN // N_CHIPS
RG_OUT_SHARD_ROWS_PAD = 352
D_FF_SHARD = D_FF // N_CHIPS


def _two_d(a):
    return a.reshape(-1, a.shape[-1])


def _gather_params(W):
    big = {n: _two_d(W[n]).astype(BF16) for n, _ in _BIG}
    big["rg_w_out"] = jnp.pad(big["rg_w_out"], ((0, RG_OUT_SHARD_ROWS_PAD - RG_OUT_SHARD_ROWS), (0, 0)))
    small_shapes = [W[n].shape for n, _ in _SMALL_SHARDED]
    small_flat = _pack([W[n] for n, _ in _SMALL_SHARDED], 32, F32)
    got = _all_gather_chips([big[n] for n, _ in _BIG] + [small_flat], "gather_weights")
    g = dict(zip([n for n, _ in _BIG], got[:-1]))
    full = {}
    full["attn_w_qkv"] = jnp.swapaxes(g["attn_w_qkv"], 0, 1).reshape(1, D_MODEL, 3 * D_MODEL)
    full["attn_w_o"] = g["attn_w_o"].reshape(1, D_MODEL, D_MODEL)
    full["rg_w_in"] = jnp.swapaxes(g["rg_w_in"], 0, 1).reshape(1, D_MODEL, 2 * D_RNN)
    full["rg_w_out"] = g["rg_w_out"][:, :RG_OUT_SHARD_ROWS].reshape(1, D_RNN, D_MODEL)
    up = g["ffn_w_up"].reshape(N_CHIPS, 2, D_MODEL, 2 * D_FF // N_CHIPS)
    full["ffn_w_up"] = jnp.transpose(up, (1, 2, 0, 3)).reshape(2, D_MODEL, 2 * D_FF)
    down = g["ffn_w_down"].reshape(N_CHIPS, 2, D_FF_SHARD, D_MODEL)
    full["ffn_w_down"] = jnp.swapaxes(down, 0, 1).reshape(2, D_FF, D_MODEL)
    full.update(zip([n for n, _ in _SMALL_SHARDED], _join(got[-1], small_shapes, [ax for _, ax in _SMALL_SHARDED])))
    for n in _REPLICATED:
        full[n] = W[n]
    return full


def _layout_params(full):
    P = {}
    P["attn_w_qkv"] = full["attn_w_qkv"][0]
    P["attn_w_o"] = full["attn_w_o"][0]
    w_in = full["rg_w_in"][0]
    P["rg_w_in_gate"] = _pad_cols(w_in[:, :D_RNN], D_RNN_PAD)
    P["rg_w_in_rec"] = _pad_cols(w_in[:, D_RNN:], D_RNN_PAD)
    P["rg_w_out"] = jnp.pad(full["rg_w_out"][0], ((0, D_RNN_PAD - D_RNN), (0, 0)))
    P["rg_conv_w"] = _rows8(_pad_cols(full["rg_conv_w"][0], D_RNN_PAD))
    P["rg_conv_b"] = _pad_cols(full["rg_conv_b"], D_RNN_PAD)
    P["rg_vecs"] = _rows8(_pad_cols(jnp.concatenate([full["rg_b_a"], full["rg_b_x"], full["rg_lambda"]], axis=0), D_RNN_PAD))
    P["rg_w_ax"] = jnp.concatenate([_block_diag_dense(full["rg_w_a"][0]), _block_diag_dense(full["rg_w_x"][0])], axis=1).astype(BF16)
    P["ffn_w_up"] = [_ffn_interleave(full["ffn_w_up"][l]) for l in range(2)]
    P["ffn_conv_w"] = [_rows8(_ffn_interleave(full["ffn_conv_w"][l])) for l in range(2)]
    P["ffn_conv_b"] = [_ffn_interleave(full["ffn_conv_b"][l:l + 1]) for l in range(2)]
    P["ffn_w_down"] = [full["ffn_w_down"][l] for l in range(2)]
    for n in ("mix_pre_g", "mix_post_g", "ffn_pre_g", "ffn_post_g"):
        P[n] = [full[n][l:l + 1] for l in range(2)]
    return P


def _assemble_grads(G):
    out = {}
    out["attn_w_qkv"] = G["attn_w_qkv"][None]
    out["attn_w_o"] = G["attn_w_o"][None]
    out["rg_w_in"] = jnp.concatenate([G["rg_w_in_gate"][:, :D_RNN], G["rg_w_in_rec"][:, :D_RNN]], axis=1)[None]
    out["rg_conv_w"] = G["rg_conv_w"][:, :D_RNN][None]
    out["rg_conv_b"] = G["rg_conv_b"][:D_RNN][None]
    out["rg_w_a"] = _block_diag_extract(G["rg_w_ax"][:, :D_RNN_PAD])[None]
    out["rg_w_x"] = _block_diag_extract(G["rg_w_ax"][:, D_RNN_PAD:])[None]
    out["rg_b_a"] = G["rg_b_a"][:D_RNN][None]
    out["rg_b_x"] = G["rg_b_x"][:D_RNN][None]
    out["rg_lambda"] = G["rg_lambda"][:D_RNN][None]
    out["rg_w_out"] = G["rg_w_out"][:D_RNN][None]
    out["ffn_w_up"] = jnp.stack([_ffn_deinterleave(G[f"ffn_w_up{l}"]) for l in range(2)])
    out["ffn_conv_w"] = jnp.stack([_ffn_deinterleave(G[f"ffn_conv_w{l}"]) for l in range(2)])
    out["ffn_conv_b"] = jnp.stack([_ffn_deinterleave(G[f"ffn_conv_b{l}"]) for l in range(2)])
    out["ffn_w_down"] = jnp.stack([G[f"ffn_w_down{l}"] for l in range(2)])
    for n in ("mix_pre_g", "mix_post_g", "ffn_pre_g", "ffn_post_g"):
        out[n] = jnp.concatenate([G[n + "0"], G[n + "1"]], axis=0)
    return out


def _big_grad_stacks(G):
    st = {}
    st["attn_w_qkv"] = jnp.swapaxes(G["attn_w_qkv"].reshape(D_MODEL, N_CHIPS, -1), 0, 1)
    st["attn_w_o"] = G["attn_w_o"].reshape(N_CHIPS, -1, D_MODEL)
    w_in = jnp.concatenate([G["rg_w_in_gate"][:, :D_RNN], G["rg_w_in_rec"][:, :D_RNN]], axis=1)
    st["rg_w_in"] = jnp.swapaxes(w_in.reshape(D_MODEL, N_CHIPS, -1), 0, 1)
    st["rg_w_out"] = G["rg_w_out"][:D_RNN].reshape(N_CHIPS, RG_OUT_SHARD_ROWS, D_MODEL)
    up = jnp.stack([_ffn_deinterleave(G[f"ffn_w_up{l}"]) for l in range(2)]).reshape(2, D_MODEL, N_CHIPS, -1)
    st["ffn_w_up"] = jnp.transpose(up, (2, 0, 1, 3)).reshape(N_CHIPS, 2 * D_MODEL, -1)
    down = jnp.stack([G[f"ffn_w_down{l}"] for l in range(2)]).reshape(2, N_CHIPS, D_FF_SHARD, D_MODEL)
    st["ffn_w_down"] = jnp.swapaxes(down, 0, 1).reshape(N_CHIPS, 2 * D_FF_SHARD, D_MODEL)
    return st


def _reduce_grads(G, W):
    big_names = [n for n, _ in _BIG]
    stacks = _big_grad_stacks(G)
    small_grads = _assemble_grads(G)
    rep_flat = jnp.concatenate([small_grads[n].reshape(-1) for n in _REPLICATED])
    n_rep = rep_flat.shape[0]
    piece = -(-n_rep // (N_CHIPS * FLAT_W)) * FLAT_W
    rep_flat = jnp.pad(rep_flat, (0, N_CHIPS * piece - n_rep)).reshape(N_CHIPS, piece)
    small_blocks = []
    for s in range(N_CHIPS):
        parts = [_split(small_grads[n], ax)[s] for n, ax in _SMALL_SHARDED] + [rep_flat[s]]
        small_blocks.append(_pack(parts, 32, F32))
    gs = [stacks[n] for n in big_names] + [jnp.stack(small_blocks)]
    tags = big_names + ["small"]
    landed = _pair_send_other_halves(gs, "rs_pair_send")
    chip_sums = [_add_own_half(g, l, f"rs_pair_add_{t}", F32 if t == "small" else BF16) for g, l, t in zip(gs, landed, tags)]
    slots = _scatter_to_chips(chip_sums, "rs_scatter")
    halves = [_sum_slots(s, f"rs_sum_{t}") for s, t in zip(slots, tags)]
    mine = _pair_gather(halves, "rs_pair_gather")
    out = {n: m.reshape(W[n].shape) for n, m in zip(big_names, mine[:-1])}
    shapes = [W[n].shape for n, _ in _SMALL_SHARDED] + [(piece,)]
    parts = _unpack(mine[-1], shapes)
    out.update(zip([n for n, _ in _SMALL_SHARDED], parts[:-1]))
    rep_rows = piece // FLAT_W
    rep_pad = -(-rep_rows // 32) * 32
    rep_piece = jnp.pad(parts[-1].reshape(rep_rows, FLAT_W), ((0, rep_pad - rep_rows), (0, 0)))
    rep_all = _all_gather_chips([rep_piece], "gather_replicated_grads")[0][:, :rep_rows].reshape(-1)
    out.update(zip(_REPLICATED, _unpack(rep_all, [W[n].shape for n in _REPLICATED])))
    return out


def _update(W, M, V, grads):
    delta, new_m, new_v = {}, {}, {}
    small = [n for n in _ORDER if n not in dict(_BIG)]
    for n, _ in _BIG:
        shp = W[n].shape
        two_d = (-1, shp[-1])
        d, m, v = _adamw(W[n].reshape(two_d), grads[n].reshape(two_d), M[n].reshape(two_d), V[n].reshape(two_d), f"adamw_{n}")
        delta[n], new_m[n], new_v[n] = d.reshape(shp), m.reshape(shp), v.reshape(shp)
    packed = [_pack([src[n] for n in small], 8, F32) for src in (W, grads, M, V)]
    outs = _adamw(*packed, "adamw_small")
    shapes = [W[n].shape for n in small]
    for tgt, flat in zip((delta, new_m, new_v), outs):
        tgt.update(zip(small, _unpack(flat, shapes)))
    return delta, new_m, new_v


def kernel(x, attn_w_qkv, attn_w_o, rg_w_in, rg_conv_w, rg_conv_b, rg_w_a, rg_b_a, rg_w_x, rg_b_x, rg_lambda, rg_w_out, ffn_w_up, ffn_conv_w, ffn_conv_b, ffn_w_down, mix_pre_g, mix_post_g, ffn_pre_g, ffn_post_g, loss_target, m_attn_w_qkv, m_attn_w_o, m_rg_w_in, m_rg_conv_w, m_rg_conv_b, m_rg_w_a, m_rg_b_a, m_rg_w_x, m_rg_b_x, m_rg_lambda, m_rg_w_out, m_ffn_w_up, m_ffn_conv_w, m_ffn_conv_b, m_ffn_w_down, m_mix_pre_g, m_mix_post_g, m_ffn_pre_g, m_ffn_post_g, v_attn_w_qkv, v_attn_w_o, v_rg_w_in, v_rg_conv_w, v_rg_conv_b, v_rg_w_a, v_rg_b_a, v_rg_w_x, v_rg_b_x, v_rg_lambda, v_rg_w_out, v_ffn_w_up, v_ffn_conv_w, v_ffn_conv_b, v_ffn_w_down, v_mix_pre_g, v_mix_post_g, v_ffn_pre_g, v_ffn_post_g):
    W = dict(attn_w_qkv=attn_w_qkv, attn_w_o=attn_w_o, rg_w_in=rg_w_in, rg_conv_w=rg_conv_w, rg_conv_b=rg_conv_b,
             rg_w_a=rg_w_a, rg_b_a=rg_b_a, rg_w_x=rg_w_x, rg_b_x=rg_b_x, rg_lambda=rg_lambda, rg_w_out=rg_w_out,
             ffn_w_up=ffn_w_up, ffn_conv_w=ffn_conv_w, ffn_conv_b=ffn_conv_b, ffn_w_down=ffn_w_down,
             mix_pre_g=mix_pre_g, mix_post_g=mix_post_g, ffn_pre_g=ffn_pre_g, ffn_post_g=ffn_post_g)
    M = dict(attn_w_qkv=m_attn_w_qkv, attn_w_o=m_attn_w_o, rg_w_in=m_rg_w_in, rg_conv_w=m_rg_conv_w, rg_conv_b=m_rg_conv_b,
             rg_w_a=m_rg_w_a, rg_b_a=m_rg_b_a, rg_w_x=m_rg_w_x, rg_b_x=m_rg_b_x, rg_lambda=m_rg_lambda, rg_w_out=m_rg_w_out,
             ffn_w_up=m_ffn_w_up, ffn_conv_w=m_ffn_conv_w, ffn_conv_b=m_ffn_conv_b, ffn_w_down=m_ffn_w_down,
             mix_pre_g=m_mix_pre_g, mix_post_g=m_mix_post_g, ffn_pre_g=m_ffn_pre_g, ffn_post_g=m_ffn_post_g)
    V = dict(attn_w_qkv=v_attn_w_qkv, attn_w_o=v_attn_w_o, rg_w_in=v_rg_w_in, rg_conv_w=v_rg_conv_w, rg_conv_b=v_rg_conv_b,
             rg_w_a=v_rg_w_a, rg_b_a=v_rg_b_a, rg_w_x=v_rg_w_x, rg_b_x=v_rg_b_x, rg_lambda=v_rg_lambda, rg_w_out=v_rg_w_out,
             ffn_w_up=v_ffn_w_up, ffn_conv_w=v_ffn_conv_w, ffn_conv_b=v_ffn_conv_b, ffn_w_down=v_ffn_w_down,
             mix_pre_g=v_mix_pre_g, mix_post_g=v_mix_post_g, ffn_pre_g=v_ffn_pre_g, ffn_post_g=v_ffn_post_g)
    P = _layout_params(_gather_params(W))
    loss_part, grad_x, G = _local_step(x[0], loss_target[0], P)
    loss = lax.psum(loss_part, ("x", "y", "c"))
    grads = _reduce_grads(G, W)
    delta, new_m, new_v = _update(W, M, V, grads)
    return (loss, grad_x[None], *[grads[n] for n in _ORDER], *[delta[n] for n in _ORDER],
            *[new_m[n] for n in _ORDER], *[new_v[n] for n in _ORDER])
```

```python
import functools
import math

import jax
import jax.numpy as jnp
from jax import lax
from jax.experimental import pallas as pl
from jax.experimental.pallas import tpu as pltpu

D_MODEL = 1024
N_HEADS = 16
HEAD_DIM = 64
D_RNN = 1344
D_RNN_PAD = 1408
RG_BLOCKS = 16
RG_BLOCK_W = 84
D_FF = 2816
FFN_TILE = 256
NORM_EPS = 1e-6
RG_C = 8.0

ADAM_LR = 0.001
ADAM_B1 = 0.9
ADAM_B2 = 0.999
ADAM_EPS = 1e-08
ADAM_WD = 0.01
ADAM_STEP = 10

LANES = 128
SUBLANES = 8
VMEM_LIMIT = 56 * 1024 * 1024
FLAT_W = 1024
N_CHIPS = 4

F32 = jnp.float32
BF16 = jnp.bfloat16
MESH = pl.DeviceIdType.MESH


def _cp(*sem):
    return pltpu.CompilerParams(dimension_semantics=tuple(sem), vmem_limit_bytes=VMEM_LIMIT)


def _mm(a, b, mode, out_dtype, name, bm=1024, bn=1024, bk=1024, add=None):
    if mode == "nn":
        (M, K), (K2, N) = a.shape, b.shape
    elif mode == "nt":
        (M, K), (N, K2) = a.shape, b.shape
    else:
        (K, M), (K2, N) = a.shape, b.shape
    assert K == K2, (a.shape, b.shape, mode)
    bm, bn, bk = min(bm, M), min(bn, N), min(bk, K)
    assert M % bm == 0 and N % bn == 0 and K % bk == 0, (M, N, K, bm, bn, bk)
    nk = K // bk
    dims = {"nn": (((1,), (0,)), ((), ())), "nt": (((1,), (1,)), ((), ())), "tn": (((0,), (0,)), ((), ()))}[mode]
    has_add = add is not None

    def body(*refs):
        a_ref, b_ref = refs[0], refs[1]
        add_ref = refs[2] if has_add else None
        o_ref = refs[2 + has_add]
        p = lax.dot_general(a_ref[...].astype(BF16), b_ref[...].astype(BF16), dims, preferred_element_type=F32)
        if nk == 1:
            if has_add:
                p = p + add_ref[...]
            o_ref[...] = p.astype(o_ref.dtype)
        else:
            acc = refs[3 + has_add]
            k = pl.program_id(2)

            @pl.when(k == 0)
            def _():
                acc[...] = p

            @pl.when(k > 0)
            def _():
                acc[...] += p

            @pl.when(k == nk - 1)
            def _():
                r = acc[...]
                if has_add:
                    r = r + add_ref[...]
                o_ref[...] = r.astype(o_ref.dtype)

    if mode == "tn":
        a_spec = pl.BlockSpec((bk, bm), lambda i, j, k: (k, i))
    else:
        a_spec = pl.BlockSpec((bm, bk), lambda i, j, k: (i, k))
    if mode == "nt":
        b_spec = pl.BlockSpec((bn, bk), lambda i, j, k: (j, k))
    else:
        b_spec = pl.BlockSpec((bk, bn), lambda i, j, k: (k, j))
    o_spec = pl.BlockSpec((bm, bn), lambda i, j, k: (i, j))
    in_specs = [a_spec, b_spec] + ([o_spec] if has_add else [])
    args = (a, b) + ((add,) if has_add else ())
    return pl.pallas_call(
        body, name=name, grid=(M // bm, N // bn, nk),
        in_specs=in_specs, out_specs=o_spec,
        out_shape=jax.ShapeDtypeStruct((M, N), out_dtype),
        scratch_shapes=[pltpu.VMEM((bm, bn), F32)] if nk > 1 else [],
        compiler_params=_cp("parallel", "parallel", "arbitrary"),
    )(*args)


def _row_block(T, bm=512):
    bm = min(bm, T)
    assert T % bm == 0
    return bm


def _rmsnorm_fwd(x, g, name):
    T, D = x.shape
    bm = _row_block(T)

    def body(x_ref, g_ref, o_ref):
        xv = x_ref[...]
        inv = lax.rsqrt(jnp.mean(xv * xv, axis=-1, keepdims=True) + NORM_EPS)
        o_ref[...] = (xv * inv * g_ref[...]).astype(o_ref.dtype)

    return pl.pallas_call(
        body, name=name, grid=(T // bm,),
        in_specs=[pl.BlockSpec((bm, D), lambda i: (i, 0)), pl.BlockSpec((1, D), lambda i: (0, 0))],
        out_specs=pl.BlockSpec((bm, D), lambda i: (i, 0)),
        out_shape=jax.ShapeDtypeStruct((T, D), BF16),
        compiler_params=_cp("parallel"),
    )(x, g)


def _postnorm_residual(x, h, g, name):
    T, D = x.shape
    bm = _row_block(T)

    def body(x_ref, h_ref, g_ref, o_ref):
        hv = h_ref[...]
        inv = lax.rsqrt(jnp.mean(hv * hv, axis=-1, keepdims=True) + NORM_EPS)
        o_ref[...] = x_ref[...] + hv * inv * g_ref[...]

    return pl.pallas_call(
        body, name=name, grid=(T // bm,),
        in_specs=[pl.BlockSpec((bm, D), lambda i: (i, 0)), pl.BlockSpec((bm, D), lambda i: (i, 0)),
                  pl.BlockSpec((1, D), lambda i: (0, 0))],
        out_specs=pl.BlockSpec((bm, D), lambda i: (i, 0)),
        out_shape=jax.ShapeDtypeStruct((T, D), F32),
        compiler_params=_cp("parallel"),
    )(x, h, g)


def _postnorm_loss(x, h, g, target, name):
    T, D = x.shape
    bm = _row_block(T)

    def body(x_ref, h_ref, g_ref, t_ref, dy_ref, loss_ref):
        hv = h_ref[...]
        inv = lax.rsqrt(jnp.mean(hv * hv, axis=-1, keepdims=True) + NORM_EPS)
        err = x_ref[...] + hv * inv * g_ref[...] - t_ref[...]
        dy_ref[...] = err * (1.0 / D)
        part = jnp.sum(jnp.sum(err * err, axis=-1, keepdims=True), axis=0, keepdims=True) * (0.5 / D)

        @pl.when(pl.program_id(0) == 0)
        def _():
            loss_ref[...] = jnp.zeros_like(loss_ref)

        loss_ref[...] += jnp.broadcast_to(part, loss_ref.shape)

    row = pl.BlockSpec((bm, D), lambda i: (i, 0))
    return pl.pallas_call(
        body, name=name, grid=(T // bm,),
        in_specs=[row, row, pl.BlockSpec((1, D), lambda i: (0, 0)), row],
        out_specs=[row, pl.BlockSpec((1, LANES), lambda i: (0, 0))],
        out_shape=[jax.ShapeDtypeStruct((T, D), F32), jax.ShapeDtypeStruct((1, LANES), F32)],
        compiler_params=_cp("arbitrary"),
    )(x, h, g, target)


def _rmsnorm_bwd(dy, xin, g, name, res=None, out_dtype=F32):
    T, D = xin.shape
    bm = _row_block(T)
    has_res = res is not None

    def body(*refs):
        dy_ref, x_ref, g_ref = refs[:3]
        res_ref = refs[3] if has_res else None
        dx_ref, dg_ref = refs[3 + has_res], refs[4 + has_res]
        xv = x_ref[...]
        dyv = dy_ref[...].astype(F32)
        inv = lax.rsqrt(jnp.mean(xv * xv, axis=-1, keepdims=True) + NORM_EPS)
        xhat = xv * inv
        dyg = dyv * g_ref[...]
        c = jnp.mean(dyg * xhat, axis=-1, keepdims=True)
        dx = inv * (dyg - xhat * c)
        if has_res:
            dx = dx + res_ref[...]
        dx_ref[...] = dx.astype(dx_ref.dtype)

        @pl.when(pl.program_id(0) == 0)
        def _():
            dg_ref[...] = jnp.zeros_like(dg_ref)

        dg_ref[...] += jnp.sum(dyv * xhat, axis=0, keepdims=True)

    row = pl.BlockSpec((bm, D), lambda i: (i, 0))
    vec = pl.BlockSpec((1, D), lambda i: (0, 0))
    args = (dy, xin, g) + ((res,) if has_res else ())
    return pl.pallas_call(
        body, name=name, grid=(T // bm,),
        in_specs=[row, row, vec] + ([row] if has_res else []),
        out_specs=[row, vec],
        out_shape=[jax.ShapeDtypeStruct((T, D), out_dtype), jax.ShapeDtypeStruct((1, D), F32)],
        compiler_params=_cp("arbitrary"),
    )(*args)


_GELU_C = math.sqrt(2.0 / math.pi)


def _gelu(x):
    return 0.5 * x * (1.0 + jnp.tanh(_GELU_C * (x + 0.044715 * x * x * x)))


def _gelu_and_grad(x):
    t = jnp.tanh(_GELU_C * (x + 0.044715 * x * x * x))
    val = 0.5 * x * (1.0 + t)
    grad = 0.5 * (1.0 + t) + 0.5 * x * (1.0 - t * t) * _GELU_C * (1.0 + 3.0 * 0.044715 * x * x)
    return val, grad


def _sigmoid(x):
    return 1.0 / (1.0 + jnp.exp(-x))


def _shift_down(x, halo_prev, k):
    xe = jnp.concatenate([halo_prev, x], axis=0)
    return pltpu.roll(xe, k, 0)[SUBLANES:]


def _shift_up(x, halo_next, k):
    xe = jnp.concatenate([x, halo_next], axis=0)
    return pltpu.roll(xe, xe.shape[0] - k, 0)[:x.shape[0]]


def _halo_specs(T, bm, bc):
    r = bm // SUBLANES
    last = T // SUBLANES - 1
    prev = pl.BlockSpec((SUBLANES, bc), lambda j, i: (jnp.maximum(i * r - 1, 0), j))
    nxt = pl.BlockSpec((SUBLANES, bc), lambda j, i: (jnp.minimum((i + 1) * r, last), j))
    return prev, nxt


def _causal_conv(x, halo, w_ref, b_ref, width):
    out = x * w_ref[width - 1:width, :] + b_ref[...]
    for k in range(width - 1):
        out = out + _shift_down(x, halo, width - 1 - k) * w_ref[k:k + 1, :]
    return out


def _ffn_act_fwd(u_pre, conv_w, conv_b, name):
    T, C2 = u_pre.shape
    bm = _row_block(T, 1024)
    bc = 2 * FFN_TILE
    nj = C2 // bc
    prev, _ = _halo_specs(T, bm, bc)

    def body(u_ref, h_ref, w_ref, b_ref, o_ref):
        i = pl.program_id(1)
        halo = jnp.where(i > 0, h_ref[...], 0.0)
        u = _causal_conv(u_ref[...], halo, w_ref, b_ref, 3)
        o_ref[...] = (_gelu(u[:, :FFN_TILE]) * u[:, FFN_TILE:]).astype(o_ref.dtype)

    return pl.pallas_call(
        body, name=name, grid=(nj, T // bm),
        in_specs=[pl.BlockSpec((bm, bc), lambda j, i: (i, j)), prev,
                  pl.BlockSpec((SUBLANES, bc), lambda j, i: (0, j)), pl.BlockSpec((1, bc), lambda j, i: (0, j))],
        out_specs=pl.BlockSpec((bm, FFN_TILE), lambda j, i: (i, j)),
        out_shape=jax.ShapeDtypeStruct((T, C2 // 2), BF16),
        compiler_params=_cp("parallel", "parallel"),
    )(u_pre, u_pre, conv_w, conv_b)


def _ffn_act_bwd(u_pre, conv_w, conv_b, d_act, name):
    T, C2 = u_pre.shape
    bm = _row_block(T, 1024)
    bc = 2 * FFN_TILE
    nj = C2 // bc
    nt = T // bm
    r = bm // SUBLANES

    def body(u_ref, h_ref, w_ref, b_ref, da_ref, dx_ref, st_ref, later):
        i = pl.program_id(1)

        @pl.when(i == 0)
        def _():
            st_ref[...] = jnp.zeros_like(st_ref)
            later[...] = jnp.zeros_like(later)

        halo = jnp.where(i < nt - 1, h_ref[...], 0.0)
        x = u_ref[...]
        xm1 = _shift_down(x, halo, 1)
        xm2 = _shift_down(x, halo, 2)
        u = xm2 * w_ref[0:1, :] + xm1 * w_ref[1:2, :] + x * w_ref[2:3, :] + b_ref[...]
        gv, gg = _gelu_and_grad(u[:, :FFN_TILE])
        da = da_ref[...]
        du = jnp.concatenate([da * u[:, FFN_TILE:] * gg, da * gv], axis=1)
        nxt = later[...]
        dx = du * w_ref[2:3, :] + _shift_up(du, nxt, 1) * w_ref[1:2, :] + _shift_up(du, nxt, 2) * w_ref[0:1, :]
        dx_ref[...] = dx.astype(dx_ref.dtype)
        later[...] = du[:SUBLANES, :]
        rows = [jnp.sum(du * xm2, axis=0, keepdims=True), jnp.sum(du * xm1, axis=0, keepdims=True),
                jnp.sum(du * x, axis=0, keepdims=True), jnp.sum(du, axis=0, keepdims=True),
                jnp.zeros((4, bc), F32)]
        st_ref[...] += jnp.concatenate(rows, axis=0)

    rows_spec = pl.BlockSpec((bm, bc), lambda j, i: (nt - 1 - i, j))
    prev = pl.BlockSpec((SUBLANES, bc), lambda j, i: (jnp.maximum((nt - 1 - i) * r - 1, 0), j))
    return pl.pallas_call(
        body, name=name, grid=(nj, nt),
        in_specs=[rows_spec, prev,
                  pl.BlockSpec((SUBLANES, bc), lambda j, i: (0, j)), pl.BlockSpec((1, bc), lambda j, i: (0, j)),
                  pl.BlockSpec((bm, FFN_TILE), lambda j, i: (nt - 1 - i, j))],
        out_specs=[rows_spec, pl.BlockSpec((SUBLANES, bc), lambda j, i: (0, j))],
        out_shape=[jax.ShapeDtypeStruct((T, C2), BF16), jax.ShapeDtypeStruct((SUBLANES, C2), F32)],
        scratch_shapes=[pltpu.VMEM((SUBLANES, bc), F32)],
        compiler_params=_cp("parallel", "arbitrary"),
    )(u_pre, u_pre, conv_w, conv_b, d_act)


def _conv_transpose(dy, conv_w, width, name, bc):
    T, C = dy.shape
    bm = _row_block(T, 512)
    _, nxt = _halo_specs(T, bm, bc)
    nt = T // bm

    def body(y_ref, h_ref, w_ref, o_ref):
        i = pl.program_id(1)
        halo = jnp.where(i < nt - 1, h_ref[...], 0.0)
        y = y_ref[...]
        out = y * w_ref[width - 1:width, :]
        for k in range(width - 1):
            out = out + _shift_up(y, halo, width - 1 - k) * w_ref[k:k + 1, :]
        o_ref[...] = out.astype(o_ref.dtype)

    return pl.pallas_call(
        body, name=name, grid=(C // bc, nt),
        in_specs=[pl.BlockSpec((bm, bc), lambda j, i: (i, j)), nxt, pl.BlockSpec((SUBLANES, bc), lambda j, i: (0, j))],
        out_specs=pl.BlockSpec((bm, bc), lambda j, i: (i, j)),
        out_shape=jax.ShapeDtypeStruct((T, C), BF16),
        compiler_params=_cp("parallel", "parallel"),
    )(dy, dy, conv_w)


ATT_BK = 128


def _split_dot(x, uu):
    hi = x.astype(BF16)
    lo = (x - hi.astype(F32)).astype(BF16)
    return jnp.dot(jnp.concatenate([hi, lo], axis=1), uu, preferred_element_type=F32)


def _log_sigmoid(z):
    return jnp.minimum(z, 0.0) - jnp.log(1.0 + jnp.exp(-jnp.abs(z)))


ATT_TQ = 512
ATT_KS = 512
ATT_SUB = ATT_KS // ATT_BK


def _suffix_consts():
    r = lax.broadcasted_iota(jnp.int32, (ATT_BK, 2 * ATT_BK), 0)
    c = lax.broadcasted_iota(jnp.int32, (ATT_BK, 2 * ATT_BK), 1)
    m = jnp.where((r > c) | (c >= ATT_BK), 1.0, 0.0).astype(BF16)
    return jnp.concatenate([m, m], axis=0)


def _suffix_sums(x, uu):
    ext = _split_dot(x, uu)
    return ext[:, :ATT_BK], ext[:, ATT_BK:]


def _masked_heads(x_f32, first):
    return (jnp.where(first, x_f32, 0.0).astype(BF16), jnp.where(first, 0.0, x_f32).astype(BF16))


def _unmasked_blocks(block, nfull, ks, T, carry):
    if 2 * ks > T:
        return lax.fori_loop(0, nfull, lambda i, c: block(nfull - 1 - i, ks, c, False), carry)
    carry = lax.fori_loop(0, jnp.bitwise_and(nfull, 1), lambda i, c: block(nfull - 1, ks, c, False), carry)
    pairs = nfull // 2
    return lax.fori_loop(0, pairs, lambda i, c: block(2 * (pairs - 1 - i), 2 * ks, c, False), carry)


def _sb_attn_fwd(qkv, name):
    T = qkv.shape[0]
    tq, ks = min(ATT_TQ, T), min(ATT_KS, T)
    assert T % tq == 0 and T % ks == 0 and ks % tq == 0
    nq = T // tq
    nsub = ks // ATT_BK
    nhp = N_HEADS // 2
    nt = (((1,), (1,)), ((), ()))


    def body(q_ref, k_ref, v_ref, o_ref):
        qi = pl.program_id(1)
        first = lax.broadcasted_iota(jnp.int32, (1, LANES), 1) < HEAD_DIM
        qs = _masked_heads(q_ref[...].astype(F32) * HEAD_DIM ** -0.5, first)
        uu = _suffix_consts()
        tpos = qi * tq + lax.broadcasted_iota(jnp.int32, (tq, ATT_BK), 0)
        col = lax.broadcasted_iota(jnp.int32, (tq, ATT_BK), 1)

        def block(kb, width, carry, masked):
            nsub = width // ATT_BK
            off = pl.multiple_of(kb * ks, ks)
            kblk = k_ref[pl.ds(off, width), :]
            vblk = v_ref[pl.ds(off, width), :]
            out = []
            for h in range(2):
                acc, run = carry[2 * h], carry[2 * h + 1]
                z = lax.dot_general(qs[h], kblk, nt, preferred_element_type=F32)
                lb = _log_sigmoid(z)
                l1m = lb - z
                ws = [None] * nsub
                for j in reversed(range(nsub)):
                    sl = slice(j * ATT_BK, (j + 1) * ATT_BK)
                    l1 = l1m[:, sl]
                    if masked:
                        causal = (off + j * ATT_BK + col) < tpos
                        l1 = jnp.where(causal, l1, 0.0)
                    cs, tot = _suffix_sums(l1, uu)
                    w = jnp.exp(lb[:, sl] + cs + run)
                    ws[j] = jnp.where(causal, w, 0.0) if masked else w
                    run = run + tot
                w = jnp.concatenate(ws, axis=1)
                acc = acc + jnp.dot(w.astype(BF16), vblk, preferred_element_type=F32)
                out += [acc, run]
            return tuple(out)

        nfull = (qi * tq) // ks
        carry = (jnp.zeros((tq, LANES), F32),) * 4
        carry = block(nfull, ks, carry, True)
        carry = _unmasked_blocks(block, nfull, ks, T, carry)
        o_ref[...] = jnp.where(first, carry[0], carry[2])

    return pl.pallas_call(
        body, name=name, grid=(nhp, nq),
        in_specs=[pl.BlockSpec((tq, LANES), lambda p, i: (i, p)),
                  pl.BlockSpec((T, LANES), lambda p, i: (0, nhp + p)),
                  pl.BlockSpec((T, LANES), lambda p, i: (0, 2 * nhp + p))],
        out_specs=pl.BlockSpec((tq, LANES), lambda p, i: (i, p)),
        out_shape=jax.ShapeDtypeStruct((T, D_MODEL), F32),
        compiler_params=_cp("parallel", "parallel"),
    )(qkv, qkv, qkv)


def _sb_attn_bwd(qkv, o, d_o, name):
    T = qkv.shape[0]
    tq, ks = min(ATT_TQ, T), min(ATT_KS, T)
    assert T % tq == 0 and T % ks == 0 and ks % tq == 0
    nq = T // tq
    nks = T // ks
    nsub = ks // ATT_BK
    nhp = N_HEADS // 2
    nt = (((1,), (1,)), ((), ()))

    def body(q_ref, k_ref, v_ref, o_ref, do_ref, dq_ref, dk_ref, dv_ref, dkt_acc, dvt_acc):
        qi = pl.program_id(1)

        @pl.when(qi == 0)
        def _():
            dkt_acc[...] = jnp.zeros_like(dkt_acc)
            dvt_acc[...] = jnp.zeros_like(dvt_acc)

        first = lax.broadcasted_iota(jnp.int32, (1, LANES), 1) < HEAD_DIM
        first_t = lax.broadcasted_iota(jnp.int32, (LANES, 1), 0) < HEAD_DIM
        qf = q_ref[...].astype(F32) * HEAD_DIM ** -0.5
        qs = _masked_heads(qf, first)
        qts = _masked_heads(qf.T, first_t)
        dof = do_ref[...].astype(BF16).astype(F32)
        dos = _masked_heads(dof, first)
        dots = _masked_heads(dof.T, first_t)
        prod = dof * o_ref[...]
        deltas = (jnp.broadcast_to(jnp.sum(jnp.where(first, prod, 0.0), axis=1, keepdims=True), (tq, LANES)),
                  jnp.broadcast_to(jnp.sum(jnp.where(first, 0.0, prod), axis=1, keepdims=True), (tq, LANES)))
        uu = _suffix_consts()
        tpos = qi * tq + lax.broadcasted_iota(jnp.int32, (tq, ATT_BK), 0)
        col = lax.broadcasted_iota(jnp.int32, (tq, ATT_BK), 1)

        def block(kb, width, carry, masked):
            nsub = width // ATT_BK
            off = pl.multiple_of(kb * ks, ks)
            kblk = k_ref[pl.ds(off, width), :]
            vblk = v_ref[pl.ds(off, width), :]
            out = []
            dkt = jnp.zeros((LANES, width), F32)
            dvt = jnp.zeros((LANES, width), F32)
            for h in range(2):
                dq, run, grun = carry[3 * h], carry[3 * h + 1], carry[3 * h + 2]
                z = lax.dot_general(qs[h], kblk, nt, preferred_element_type=F32)
                lb = _log_sigmoid(z)
                l1m = lb - z
                dw = lax.dot_general(dos[h], vblk, nt, preferred_element_type=F32)
                wbs, dzs = [None] * nsub, [None] * nsub
                for j in reversed(range(nsub)):
                    sl = slice(j * ATT_BK, (j + 1) * ATT_BK)
                    l1 = l1m[:, sl]
                    if masked:
                        causal = (off + j * ATT_BK + col) < tpos
                        l1 = jnp.where(causal, l1, 0.0)
                    cs, tot = _suffix_sums(l1, uu)
                    w = jnp.exp(lb[:, sl] + cs + run)
                    if masked:
                        w = jnp.where(causal, w, 0.0)
                    wbs[j] = w.astype(BF16)
                    g = wbs[j].astype(F32) * dw[:, sl]
                    gcs, gtot = _suffix_sums(g, uu)
                    gpre = deltas[h] - grun - gcs - g
                    dz = g - jnp.exp(lb[:, sl]) * (g + gpre)
                    if masked:
                        dz = jnp.where(causal, dz, 0.0)
                    dzs[j] = dz.astype(BF16)
                    run = run + tot
                    grun = grun + gtot
                wb = jnp.concatenate(wbs, axis=1)
                dzb = jnp.concatenate(dzs, axis=1)
                dq = dq + jnp.dot(dzb, kblk, preferred_element_type=F32)
                dkt = dkt + jnp.dot(qts[h], dzb, preferred_element_type=F32)
                dvt = dvt + jnp.dot(dots[h], wb, preferred_element_type=F32)
                out += [dq, run, grun]
            for part in range(width // ks):
                dkt_acc[kb + part] += dkt[:, part * ks:(part + 1) * ks]
                dvt_acc[kb + part] += dvt[:, part * ks:(part + 1) * ks]
            return tuple(out)

        nfull = (qi * tq) // ks
        carry = (jnp.zeros((tq, LANES), F32),) * 6
        carry = block(nfull, ks, carry, True)
        carry = _unmasked_blocks(block, nfull, ks, T, carry)
        dq_ref[...] = (jnp.where(first, carry[0], carry[3]) * HEAD_DIM ** -0.5).astype(dq_ref.dtype)

        @pl.when(qi == nq - 1)
        def _():
            for kb in range(nks):
                dk_ref[kb * ks:(kb + 1) * ks, :] = dkt_acc[kb].T.astype(dk_ref.dtype)
                dv_ref[kb * ks:(kb + 1) * ks, :] = dvt_acc[kb].T.astype(dv_ref.dtype)

    qspec = pl.BlockSpec((tq, LANES), lambda p, i: (i, p))
    full = pl.BlockSpec((T, LANES), lambda p, i: (0, p))
    return pl.pallas_call(
        body, name=name, grid=(nhp, nq),
        in_specs=[qspec, pl.BlockSpec((T, LANES), lambda p, i: (0, nhp + p)),
                  pl.BlockSpec((T, LANES), lambda p, i: (0, 2 * nhp + p)), qspec, qspec],
        out_specs=[qspec, full, full],
        out_shape=[jax.ShapeDtypeStruct((T, D_MODEL), BF16)] * 3,
        scratch_shapes=[pltpu.VMEM((nks, LANES, ks), F32), pltpu.VMEM((nks, LANES, ks), F32)],
        compiler_params=_cp("parallel", "arbitrary"),
    )(qkv, qkv, qkv, o, d_o)


def _rg_conv_fwd(x, conv_w, conv_b, name):
    T, C = x.shape
    bm = _row_block(T, 512)
    bc = C
    prev, _ = _halo_specs(T, bm, bc)

    def body(x_ref, h_ref, w_ref, b_ref, o_ref):
        halo = jnp.where(pl.program_id(1) > 0, h_ref[...], 0.0)
        o_ref[...] = _causal_conv(x_ref[...], halo, w_ref, b_ref, 4)

    return pl.pallas_call(
        body, name=name, grid=(C // bc, T // bm),
        in_specs=[pl.BlockSpec((bm, bc), lambda j, i: (i, j)), prev,
                  pl.BlockSpec((SUBLANES, bc), lambda j, i: (0, j)), pl.BlockSpec((1, bc), lambda j, i: (0, j))],
        out_specs=pl.BlockSpec((bm, bc), lambda j, i: (i, j)),
        out_shape=jax.ShapeDtypeStruct((T, C), F32),
        compiler_params=_cp("parallel", "parallel"),
    )(x, x, conv_w, conv_b)


def _rg_gate_math(ra, rx, rec, vec_ref):
    r = _sigmoid(ra + vec_ref[0:1, :])
    ig = _sigmoid(rx + vec_ref[1:2, :])
    lam = vec_ref[2:3, :]
    sp = jnp.maximum(-lam, 0.0) + jnp.log(1.0 + jnp.exp(-jnp.abs(lam)))
    log_a = -RG_C * r * sp
    a = jnp.exp(log_a)
    th = jnp.tanh(log_a)
    mult = jnp.sqrt(-2.0 * th / (1.0 - th))
    return r, ig, sp, a, mult


def _rg_gates_fwd(rax, rec, vecs, name):
    T, C = rec.shape
    bm = _row_block(T, 512)

    def body(ra_ref, rx_ref, rec_ref, vec_ref, a_ref, u_ref):
        recv = rec_ref[...]
        _, ig, _, a, mult = _rg_gate_math(ra_ref[...], rx_ref[...], recv, vec_ref)
        a_ref[...] = a
        u_ref[...] = mult * ig * recv

    blk = pl.BlockSpec((bm, C), lambda i: (i, 0))
    return pl.pallas_call(
        body, name=name, grid=(T // bm,),
        in_specs=[blk, pl.BlockSpec((bm, C), lambda i: (i, 1)), blk, pl.BlockSpec((SUBLANES, C), lambda i: (0, 0))],
        out_specs=[blk, blk],
        out_shape=[jax.ShapeDtypeStruct((T, C), F32)] * 2,
        compiler_params=_cp("parallel"),
    )(rax, rax, rec, vecs)


def _linear_scan(a, b, name, reverse=False, shift_coef=False):
    T, C = a.shape
    bm = _row_block(T, 256)
    bc = C
    nt = T // bm
    _, nxt = _halo_specs(T, bm, bc)
    groups = bm // SUBLANES

    def body(*refs):
        if shift_coef:
            a_ref, an_ref, b_ref, h_ref, carry = refs
        else:
            a_ref, b_ref, h_ref, carry = refs
        i = pl.program_id(1)

        @pl.when(i == 0)
        def _():
            carry[...] = jnp.zeros_like(carry)

        coef = a_ref[...]
        if shift_coef:
            coef = _shift_up(coef, an_ref[...], 1)
        val = b_ref[...]
        sub = jnp.bitwise_and(lax.broadcasted_iota(jnp.int32, (bm, bc), 0), SUBLANES - 1)
        for s in (1, 2, 4):
            if reverse:
                keep = sub < SUBLANES - s
                cs, vs = pltpu.roll(coef, bm - s, 0), pltpu.roll(val, bm - s, 0)
            else:
                keep = sub >= s
                cs, vs = pltpu.roll(coef, s, 0), pltpu.roll(val, s, 0)
            val = jnp.where(keep, coef * vs + val, val)
            coef = jnp.where(keep, coef * cs, coef)
        edge = carry[0:1, :]
        pieces = [None] * groups
        for g in (reversed(range(groups)) if reverse else range(groups)):
            rows = slice(g * SUBLANES, (g + 1) * SUBLANES)
            hg = val[rows] + coef[rows] * edge
            pieces[g] = hg
            edge = hg[0:1, :] if reverse else hg[SUBLANES - 1:SUBLANES, :]
        h_ref[...] = jnp.concatenate(pieces, axis=0)
        carry[...] = jnp.broadcast_to(edge, carry.shape)

    if reverse:
        blk = pl.BlockSpec((bm, bc), lambda j, i: (nt - 1 - i, j))
        r = bm // SUBLANES
        last = T // SUBLANES - 1
        nxt = pl.BlockSpec((SUBLANES, bc), lambda j, i: (jnp.minimum((nt - i) * r, last), j))
    else:
        blk = pl.BlockSpec((bm, bc), lambda j, i: (i, j))
    in_specs = [blk] + ([nxt] if shift_coef else []) + [blk]
    args = (a,) + ((a,) if shift_coef else ()) + (b,)
    return pl.pallas_call(
        body, name=name, grid=(C // bc, nt),
        in_specs=in_specs, out_specs=blk,
        out_shape=jax.ShapeDtypeStruct((T, C), F32),
        scratch_shapes=[pltpu.VMEM((SUBLANES, bc), F32)],
        compiler_params=_cp("parallel", "arbitrary"),
    )(*args)


def _rg_out_fwd(gate_branch, h, name):
    T, C = h.shape
    bm = _row_block(T, 512)

    def body(g_ref, h_ref, o_ref):
        o_ref[...] = (_gelu(g_ref[...]) * h_ref[...]).astype(o_ref.dtype)

    blk = pl.BlockSpec((bm, C), lambda i: (i, 0))
    return pl.pallas_call(
        body, name=name, grid=(T // bm,), in_specs=[blk, blk], out_specs=blk,
        out_shape=jax.ShapeDtypeStruct((T, C), BF16), compiler_params=_cp("parallel"),
    )(gate_branch, h)


def _rg_out_bwd(d_gh, gate_branch, h, name):
    T, C = h.shape
    bm = _row_block(T, 512)

    def body(d_ref, g_ref, h_ref, dg_ref, dh_ref):
        gv, gg = _gelu_and_grad(g_ref[...])
        d = d_ref[...]
        dg_ref[...] = (d * h_ref[...] * gg).astype(dg_ref.dtype)
        dh_ref[...] = d * gv

    blk = pl.BlockSpec((bm, C), lambda i: (i, 0))
    return pl.pallas_call(
        body, name=name, grid=(T // bm,), in_specs=[blk, blk, blk], out_specs=[blk, blk],
        out_shape=[jax.ShapeDtypeStruct((T, C), BF16), jax.ShapeDtypeStruct((T, C), F32)],
        compiler_params=_cp("parallel"),
    )(d_gh, gate_branch, h)


def _rg_gates_bwd(dhs, h, rax, rec, vecs, name):
    T, C = rec.shape
    bm = _row_block(T, 1024)
    bc = LANES
    prev, _ = _halo_specs(T, bm, bc)
    nc = C // bc

    def body(d_ref, h_ref, hh_ref, ra_ref, rx_ref, rec_ref, vec_ref, dra_ref, drx_ref, drec_ref, st_ref):
        i = pl.program_id(1)
        halo = jnp.where(i > 0, hh_ref[...], 0.0)
        recv = rec_ref[...]
        du = d_ref[...]
        r, ig, sp, a, mult = _rg_gate_math(ra_ref[...], rx_ref[...], recv, vec_ref)
        da = du * _shift_down(h_ref[...], halo, 1)
        d_i = du * mult * recv
        drec_ref[...] = du * mult * ig
        d_mult = du * ig * recv
        d_log_a = da * a - d_mult * (a * a) / mult
        d_ra = d_log_a * (-RG_C) * sp * r * (1.0 - r)
        d_rx = d_i * ig * (1.0 - ig)
        dra_ref[...] = d_ra.astype(dra_ref.dtype)
        drx_ref[...] = d_rx.astype(drx_ref.dtype)
        lam = vec_ref[2:3, :]
        d_sp = jnp.sum(d_log_a * (-RG_C) * r, axis=0, keepdims=True)
        rows = [jnp.sum(d_ra, axis=0, keepdims=True), jnp.sum(d_rx, axis=0, keepdims=True),
                d_sp * (_sigmoid(lam) - 1.0), jnp.zeros((5, bc), F32)]

        @pl.when(i == 0)
        def _():
            st_ref[...] = jnp.zeros_like(st_ref)

        st_ref[...] += jnp.concatenate(rows, axis=0)

    blk = pl.BlockSpec((bm, bc), lambda j, i: (i, j))
    blk_x = pl.BlockSpec((bm, bc), lambda j, i: (i, nc + j))
    vec = pl.BlockSpec((SUBLANES, bc), lambda j, i: (0, j))
    d_ra, d_rx, d_rec, stats = pl.pallas_call(
        body, name=name, grid=(nc, T // bm),
        in_specs=[blk, blk, prev, blk, blk_x, blk, vec],
        out_specs=[blk, blk, blk, vec],
        out_shape=[jax.ShapeDtypeStruct((T, C), BF16), jax.ShapeDtypeStruct((T, C), BF16),
                   jax.ShapeDtypeStruct((T, C), F32), jax.ShapeDtypeStruct((SUBLANES, C), F32)],
        compiler_params=_cp("parallel", "arbitrary"),
    )(dhs, h, h, rax, rax, rec, vecs)
    return d_ra, d_rx, d_rec, stats


def _rg_conv_bwd_stats(d_rec, x, name):
    T, C = x.shape
    bm = _row_block(T, 1024)
    bc = LANES
    prev, _ = _halo_specs(T, bm, bc)

    def body(d_ref, x_ref, h_ref, st_ref):
        i = pl.program_id(1)
        halo = jnp.where(i > 0, h_ref[...], 0.0)
        d = d_ref[...]
        xv = x_ref[...]
        rows = [jnp.sum(d * _shift_down(xv, halo, 3 - k), axis=0, keepdims=True) for k in range(3)]
        rows += [jnp.sum(d * xv, axis=0, keepdims=True), jnp.sum(d, axis=0, keepdims=True), jnp.zeros((3, bc), F32)]

        @pl.when(i == 0)
        def _():
            st_ref[...] = jnp.zeros_like(st_ref)

        st_ref[...] += jnp.concatenate(rows, axis=0)

    blk = pl.BlockSpec((bm, bc), lambda j, i: (i, j))
    vec = pl.BlockSpec((SUBLANES, bc), lambda j, i: (0, j))
    return pl.pallas_call(
        body, name=name, grid=(C // bc, T // bm),
        in_specs=[blk, blk, prev], out_specs=vec,
        out_shape=jax.ShapeDtypeStruct((SUBLANES, C), F32),
        compiler_params=_cp("parallel", "arbitrary"),
    )(d_rec, x, x)


def _adamw(w, g, m, v, name):
    R, W = w.shape
    bm = R
    for cand in (512, 256, 128, 64, 32, 16, 8):
        if R % cand == 0:
            bm = cand
            break
    bc1 = 1.0 - ADAM_B1 ** ADAM_STEP
    bc2 = 1.0 - ADAM_B2 ** ADAM_STEP

    def body(w_ref, g_ref, m_ref, v_ref, d_ref, nm_ref, nv_ref):
        gv = g_ref[...]
        nm = ADAM_B1 * m_ref[...] + (1.0 - ADAM_B1) * gv
        nv = ADAM_B2 * v_ref[...] + (1.0 - ADAM_B2) * (gv * gv)
        m_hat = nm / bc1
        v_hat = nv / bc2
        d_ref[...] = -ADAM_LR * (m_hat / (jnp.sqrt(v_hat) + ADAM_EPS) + ADAM_WD * w_ref[...])
        nm_ref[...] = nm
        nv_ref[...] = nv

    blk = pl.BlockSpec((bm, W), lambda i: (i, 0))
    return pl.pallas_call(
        body, name=name, grid=(R // bm,), in_specs=[blk] * 4, out_specs=[blk] * 3,
        out_shape=[jax.ShapeDtypeStruct((R, W), F32)] * 3, compiler_params=_cp("parallel"),
    )(w, g, m, v)


_ANY = pl.BlockSpec(memory_space=pl.ANY)


def _me():
    return lax.axis_index("x"), lax.axis_index("y"), lax.axis_index("c")


def _other_chips(x, y):
    return [(1 - x, y), (x, 1 - y), (1 - x, 1 - y)]


def _remote(src, dst, send_sem, recv_sem, to):
    return pltpu.make_async_remote_copy(src_ref=src, dst_ref=dst, send_sem=send_sem, recv_sem=recv_sem,
                                        device_id=to, device_id_type=MESH)


def _half_rows(c, rows_half, align):
    return pl.ds(pl.multiple_of(c * rows_half, align), rows_half)


def _divisor_block(rows, cap=512):
    if rows <= cap:
        return rows
    best = None
    for cand in range(16, cap + 1, 16):
        if rows % cand == 0:
            best = cand
    assert best is not None, rows
    return best


def _all_gather_chips(shards, name):
    n = len(shards)
    halves = [s.shape[0] // 2 for s in shards]
    for s, rh in zip(shards, halves):
        assert 2 * rh == s.shape[0] and rh % 16 == 0, s.shape

    def body(*refs):
        x_refs, out_refs = refs[:n], refs[n:2 * n]
        send_sems, recv_sems = refs[2 * n:]
        x, y, c = _me()
        s_me = 2 * x + y
        chips = _other_chips(x, y)
        sends = []
        for a in range(n):
            half = _half_rows(c, halves[a], 16)
            for j, (cx, cy) in enumerate(chips):
                cp = _remote(x_refs[a].at[half], out_refs[a].at[s_me, half], send_sems.at[6 * a + j],
                             recv_sems.at[6 * a + j], (cx, cy, c))
                cp.start()
                sends.append(cp)
        for a in range(n):
            half = _half_rows(c, halves[a], 16)
            for j, (cx, cy) in enumerate(chips):
                landed = out_refs[a].at[2 * cx + cy, half]
                _remote(landed, landed, send_sems.at[6 * a + j], recv_sems.at[6 * a + j], (cx, cy, c)).wait_recv()
                fwd = _remote(landed, landed, send_sems.at[6 * a + 3 + j], recv_sems.at[6 * a + 3 + j], (x, y, 1 - c))
                fwd.start()
                sends.append(fwd)
        for a in range(n):
            other_half = _half_rows(1 - c, halves[a], 16)
            for j, (cx, cy) in enumerate(chips):
                landed = out_refs[a].at[2 * cx + cy, other_half]
                _remote(landed, landed, send_sems.at[6 * a + 3 + j], recv_sems.at[6 * a + 3 + j], (x, y, 1 - c)).wait_recv()
        for cp in sends:
            cp.wait_send()

    outs = pl.pallas_call(
        body, name=name, in_specs=[_ANY] * n, out_specs=[_ANY] * n,
        out_shape=[jax.ShapeDtypeStruct((N_CHIPS,) + s.shape, s.dtype) for s in shards],
        scratch_shapes=[pltpu.SemaphoreType.DMA((6 * n,)), pltpu.SemaphoreType.DMA((6 * n,))],
    )(*shards)
    s_me = 2 * lax.axis_index("x") + lax.axis_index("y")
    return [lax.dynamic_update_slice(o, s[None], (s_me, 0, 0)) for o, s in zip(outs, shards)]


def _pair_send_other_halves(gs, name):
    n = len(gs)
    halves = [g.shape[1] // 2 for g in gs]
    for g, rh in zip(gs, halves):
        assert 2 * rh == g.shape[1] and rh % 8 == 0, g.shape

    def body(*refs):
        g_refs, land_refs = refs[:n], refs[n:2 * n]
        send_sems, recv_sems = refs[2 * n:]
        x, y, c = _me()
        copies = []
        for a in range(n):
            src = g_refs[a].at[:, _half_rows(1 - c, halves[a], 8), :]
            cp = _remote(src, land_refs[a], send_sems.at[a], recv_sems.at[a], (x, y, 1 - c))
            cp.start()
            copies.append(cp)
        for cp in copies:
            cp.wait()

    return pl.pallas_call(
        body, name=name, in_specs=[_ANY] * n, out_specs=[_ANY] * n,
        out_shape=[jax.ShapeDtypeStruct((g.shape[0], rh, g.shape[2]), g.dtype) for g, rh in zip(gs, halves)],
        scratch_shapes=[pltpu.SemaphoreType.DMA((n,)), pltpu.SemaphoreType.DMA((n,))],
    )(*gs)


def _add_own_half(g, landed, name, out_dtype):
    n, R, W = g.shape
    Rh = R // 2
    bm = _divisor_block(Rh)
    nb = Rh // bm
    c_arr = lax.axis_index("c").astype(jnp.int32).reshape((1,))

    def body(c_ref, g_ref, l_ref, o_ref):
        o_ref[...] = (g_ref[...] + l_ref[...]).astype(o_ref.dtype)

    grid_spec = pltpu.PrefetchScalarGridSpec(
        num_scalar_prefetch=1, grid=(n, nb),
        in_specs=[pl.BlockSpec((1, bm, W), lambda s, i, c_ref: (s, c_ref[0] * nb + i, 0)),
                  pl.BlockSpec((1, bm, W), lambda s, i, c_ref: (s, i, 0))],
        out_specs=pl.BlockSpec((1, bm, W), lambda s, i, c_ref: (s, i, 0)))
    return pl.pallas_call(
        body, name=name, grid_spec=grid_spec, out_shape=jax.ShapeDtypeStruct((n, Rh, W), out_dtype),
        compiler_params=_cp("parallel", "parallel"),
    )(c_arr, g, landed)


def _scatter_to_chips(ps, name):
    n = len(ps)

    def body(*refs):
        p_refs, land_refs = refs[:n], refs[n:2 * n]
        send_sems, recv_sems = refs[2 * n:]
        x, y, c = _me()
        s_me = 2 * x + y
        chips = _other_chips(x, y)
        sends = []
        for a in range(n):
            for j, (cx, cy) in enumerate(chips):
                cp = _remote(p_refs[a].at[2 * cx + cy], land_refs[a].at[s_me], send_sems.at[3 * a + j],
                             recv_sems.at[3 * a + j], (cx, cy, c))
                cp.start()
                sends.append(cp)
        for a in range(n):
            for j, (cx, cy) in enumerate(chips):
                slot = land_refs[a].at[2 * cx + cy]
                _remote(slot, slot, send_sems.at[3 * a + j], recv_sems.at[3 * a + j], (cx, cy, c)).wait_recv()
        for cp in sends:
            cp.wait_send()

    lands = pl.pallas_call(
        body, name=name, in_specs=[_ANY] * n, out_specs=[_ANY] * n,
        out_shape=[jax.ShapeDtypeStruct(p.shape, p.dtype) for p in ps],
        scratch_shapes=[pltpu.SemaphoreType.DMA((3 * n,)), pltpu.SemaphoreType.DMA((3 * n,))],
    )(*ps)
    s_me = 2 * lax.axis_index("x") + lax.axis_index("y")
    return [lax.dynamic_update_slice(l, lax.dynamic_slice(p, (s_me, 0, 0), (1,) + p.shape[1:]), (s_me, 0, 0))
            for l, p in zip(lands, ps)]


def _sum_slots(l, name):
    n, Rh, W = l.shape
    bm = _divisor_block(Rh, 256)
    c_arr = lax.axis_index("c").astype(jnp.int32).reshape((1,))

    def body(c_ref, l_ref, o_ref):
        o_ref[0] = ((l_ref[0].astype(F32) + l_ref[1].astype(F32)) + l_ref[2].astype(F32)) + l_ref[3].astype(F32)

    grid_spec = pltpu.PrefetchScalarGridSpec(
        num_scalar_prefetch=1, grid=(Rh // bm,),
        in_specs=[pl.BlockSpec((n, bm, W), lambda i, c_ref: (0, i, 0))],
        out_specs=pl.BlockSpec((1, bm, W), lambda i, c_ref: (c_ref[0], i, 0)))
    return pl.pallas_call(
        body, name=name, grid_spec=grid_spec,
        out_shape=jax.ShapeDtypeStruct((2, Rh, W), F32), compiler_params=_cp("parallel"),
    )(c_arr, l)


def _pair_gather(fs, name):
    n = len(fs)

    def body(*refs):
        f_refs, out_refs = refs[:n], refs[n:2 * n]
        send_sems, recv_sems = refs[2 * n:]
        x, y, c = _me()
        sends = []
        for a in range(n):
            cp = _remote(f_refs[a].at[c], out_refs[a].at[c], send_sems.at[a], recv_sems.at[a], (x, y, 1 - c))
            cp.start()
            sends.append(cp)
        for a in range(n):
            sib = out_refs[a].at[1 - c]
            _remote(sib, sib, send_sems.at[a], recv_sems.at[a], (x, y, 1 - c)).wait_recv()
        for cp in sends:
            cp.wait_send()

    return pl.pallas_call(
        body, name=name, in_specs=[_ANY] * n, out_specs=[_ANY] * n,
        out_shape=[jax.ShapeDtypeStruct(f.shape, f.dtype) for f in fs],
        input_output_aliases={a: a for a in range(n)},
        scratch_shapes=[pltpu.SemaphoreType.DMA((n,)), pltpu.SemaphoreType.DMA((n,))],
    )(*fs)


def _pack(arrays, rows_multiple, dtype):
    flat = jnp.concatenate([a.reshape(-1).astype(dtype) for a in arrays])
    n = flat.shape[0]
    rows = -(-n // FLAT_W)
    rows = -(-rows // rows_multiple) * rows_multiple
    return jnp.pad(flat, (0, rows * FLAT_W - n)).reshape(rows, FLAT_W)


def _unpack(flat, shapes):
    flat = flat.reshape(-1)
    out, off = [], 0
    for shp in shapes:
        n = math.prod(shp)
        out.append(flat[off:off + n].reshape(shp))
        off += n
    return out


def _pad_cols(a, width):
    return jnp.pad(a, [(0, 0)] * (a.ndim - 1) + [(0, width - a.shape[-1])])


def _rows8(a):
    return jnp.pad(a, ((0, SUBLANES - a.shape[0]), (0, 0)))


def _ffn_interleave(a):
    lead = a.shape[:-1]
    n = D_FF // FFN_TILE
    return jnp.swapaxes(a.reshape(lead + (2, n, FFN_TILE)), -3, -2).reshape(lead + (2 * D_FF,))


def _ffn_deinterleave(a):
    lead = a.shape[:-1]
    n = D_FF // FFN_TILE
    return jnp.swapaxes(a.reshape(lead + (n, 2, FFN_TILE)), -3, -2).reshape(lead + (2 * D_FF,))


def _block_diag_dense(w):
    eye = jnp.eye(RG_BLOCKS, dtype=w.dtype)
    dense = jnp.einsum("ncd,nm->ncmd", w, eye).reshape(D_RNN, D_RNN)
    return jnp.pad(dense, ((0, D_RNN_PAD - D_RNN), (0, D_RNN_PAD - D_RNN)))


def _block_diag_extract(dense):
    d = dense[:D_RNN, :D_RNN].reshape(RG_BLOCKS, RG_BLOCK_W, RG_BLOCKS, RG_BLOCK_W)
    return jnp.stack([d[n, :, n, :] for n in range(RG_BLOCKS)])


def _ffn_fwd(x, l, P, tag):
    n = _rmsnorm_fwd(x, P["ffn_pre_g"][l], f"{tag}_prenorm")
    u_pre = _mm(n, P["ffn_w_up"][l], "nn", F32, f"{tag}_up", bn=2 * D_FF // 4)
    act = _ffn_act_fwd(u_pre, P["ffn_conv_w"][l], P["ffn_conv_b"][l], f"{tag}_act")
    f = _mm(act, P["ffn_w_down"][l], "nn", F32, f"{tag}_down", bk=D_FF)
    return f, (n, u_pre, act)


def _ffn_bwd(df, x, saved, l, P, tag):
    n, u_pre, act = saved
    g_down = _mm(act, df, "tn", F32, f"{tag}_dwdown", bm=D_FF // 2, bk=1024)
    d_act = _mm(df, P["ffn_w_down"][l], "nt", F32, f"{tag}_dact", bn=D_FF // 2)
    du_pre, stats = _ffn_act_bwd(u_pre, P["ffn_conv_w"][l], P["ffn_conv_b"][l], d_act, f"{tag}_actbwd")
    g_up = _mm(n, du_pre, "tn", F32, f"{tag}_dwup", bn=2 * D_FF // 4, bk=1024)
    dn = _mm(du_pre, P["ffn_w_up"][l], "nt", F32, f"{tag}_dn", bk=2 * D_FF // 4)
    return dn, dict(ffn_w_down=g_down, ffn_w_up=g_up, ffn_conv_w=stats[0:3], ffn_conv_b=stats[3])


def _local_step(x, target, P):
    G = {}
    n0 = _rmsnorm_fwd(x, P["mix_pre_g"][0], "l0_prenorm")
    qkv = _mm(n0, P["attn_w_qkv"], "nn", BF16, "l0_qkv")
    o = _sb_attn_fwd(qkv, "l0_attn")
    a0 = _mm(o, P["attn_w_o"], "nn", F32, "l0_wo")
    x1 = _postnorm_residual(x, a0, P["mix_post_g"][0], "l0_postnorm")
    f0, ffn0 = _ffn_fwd(x1, 0, P, "f0")
    x2 = _postnorm_residual(x1, f0, P["ffn_post_g"][0], "f0_postnorm")
    n1 = _rmsnorm_fwd(x2, P["mix_pre_g"][1], "l1_prenorm")
    gate_b = _mm(n1, P["rg_w_in_gate"], "nn", F32, "l1_in_gate", bn=D_RNN_PAD)
    rec_b = _mm(n1, P["rg_w_in_rec"], "nn", F32, "l1_in_rec", bn=D_RNN_PAD)
    rec = _rg_conv_fwd(rec_b, P["rg_conv_w"], P["rg_conv_b"], "l1_conv")
    rax = _mm(rec, P["rg_w_ax"], "nn", F32, "l1_gates_mm", bn=D_RNN_PAD, bk=D_RNN_PAD)
    a, u = _rg_gates_fwd(rax, rec, P["rg_vecs"], "l1_gates")
    h = _linear_scan(a, u, "l1_scan")
    gh = _rg_out_fwd(gate_b, h, "l1_gate_out")
    y1 = _mm(gh, P["rg_w_out"], "nn", F32, "l1_wout", bk=D_RNN_PAD)
    x3 = _postnorm_residual(x2, y1, P["mix_post_g"][1], "l1_postnorm")
    f1, ffn1 = _ffn_fwd(x3, 1, P, "f1")
    dx, loss = _postnorm_loss(x3, f1, P["ffn_post_g"][1], target, "loss")

    df, G["ffn_post_g1"] = _rmsnorm_bwd(dx, f1, P["ffn_post_g"][1], "f1_postnorm_bwd", out_dtype=BF16)
    dn, g = _ffn_bwd(df, x3, ffn1, 1, P, "f1")
    G.update({k + "1": v for k, v in g.items()})
    dx, G["ffn_pre_g1"] = _rmsnorm_bwd(dn, x3, P["ffn_pre_g"][1], "f1_prenorm_bwd", res=dx)
    dy, G["mix_post_g1"] = _rmsnorm_bwd(dx, y1, P["mix_post_g"][1], "l1_postnorm_bwd", out_dtype=BF16)
    G["rg_w_out"] = _mm(gh, dy, "tn", F32, "l1_dwout", bm=D_RNN_PAD, bk=1024)
    d_gh = _mm(dy, P["rg_w_out"], "nt", F32, "l1_dgh", bn=D_RNN_PAD)
    d_gate, d_h = _rg_out_bwd(d_gh, gate_b, h, "l1_gate_out_bwd")
    dhs = _linear_scan(a, d_h, "l1_scan_bwd", reverse=True, shift_coef=True)
    d_ra, d_rx, d_rec_direct, st = _rg_gates_bwd(dhs, h, rax, rec, P["rg_vecs"], "l1_gates_bwd")
    G["rg_b_a"], G["rg_b_x"], G["rg_lambda"] = st[0], st[1], st[2]
    d_rax = jnp.concatenate([d_ra, d_rx], axis=1)
    G["rg_w_ax"] = _mm(rec, d_rax, "tn", F32, "l1_dwax", bm=D_RNN_PAD, bn=D_RNN_PAD, bk=1024)
    d_rec = _mm(d_rax, P["rg_w_ax"], "nt", F32, "l1_drec", bm=512, bn=D_RNN_PAD, bk=2 * D_RNN_PAD, add=d_rec_direct)
    st = _rg_conv_bwd_stats(d_rec, rec_b, "l1_conv_stats")
    G["rg_conv_w"], G["rg_conv_b"] = st[0:4], st[4]
    d_rec_b = _conv_transpose(d_rec, P["rg_conv_w"], 4, "l1_convT", D_RNN_PAD)
    G["rg_w_in_gate"] = _mm(n1, d_gate, "tn", F32, "l1_dwin_gate", bn=D_RNN_PAD, bk=1024)
    G["rg_w_in_rec"] = _mm(n1, d_rec_b, "tn", F32, "l1_dwin_rec", bn=D_RNN_PAD, bk=1024)
    dn = _mm(d_gate, P["rg_w_in_gate"], "nt", F32, "l1_dn_gate", bk=D_RNN_PAD)
    dn = _mm(d_rec_b, P["rg_w_in_rec"], "nt", F32, "l1_dn_rec", bk=D_RNN_PAD, add=dn)
    dx, G["mix_pre_g1"] = _rmsnorm_bwd(dn, x2, P["mix_pre_g"][1], "l1_prenorm_bwd", res=dx)
    df, G["ffn_post_g0"] = _rmsnorm_bwd(dx, f0, P["ffn_post_g"][0], "f0_postnorm_bwd", out_dtype=BF16)
    dn, g = _ffn_bwd(df, x1, ffn0, 0, P, "f0")
    G.update({k + "0": v for k, v in g.items()})
    dx, G["ffn_pre_g0"] = _rmsnorm_bwd(dn, x1, P["ffn_pre_g"][0], "f0_prenorm_bwd", res=dx)
    da, G["mix_post_g0"] = _rmsnorm_bwd(dx, a0, P["mix_post_g"][0], "l0_postnorm_bwd", out_dtype=BF16)
    G["attn_w_o"] = _mm(o, da, "tn", F32, "l0_dwo", bk=1024)
    d_o = _mm(da, P["attn_w_o"], "nt", F32, "l0_do")
    dq, dk, dv = _sb_attn_bwd(qkv, o, d_o, "l0_attn_bwd")
    d_qkv = jnp.concatenate([dq, dk, dv], axis=1)
    G["attn_w_qkv"] = _mm(n0, d_qkv, "tn", F32, "l0_dwqkv", bk=1024)
    dn = _mm(d_qkv, P["attn_w_qkv"], "nt", F32, "l0_dn")
    dx, G["mix_pre_g0"] = _rmsnorm_bwd(dn, x, P["mix_pre_g"][0], "l0_prenorm_bwd", res=dx)
    return loss[0, 0], dx, G


_BIG = [("attn_w_qkv", 2), ("attn_w_o", 1), ("rg_w_in", 2), ("rg_w_out", 1), ("ffn_w_up", 2), ("ffn_w_down", 1)]
_SMALL_SHARDED = [("rg_conv_w", 2), ("rg_conv_b", 1), ("rg_b_a", 1), ("rg_b_x", 1), ("rg_lambda", 1), ("ffn_conv_w", 2)]
_REPLICATED = ["rg_w_a", "rg_w_x", "ffn_conv_b", "mix_pre_g", "mix_post_g", "ffn_pre_g", "ffn_post_g"]
_ORDER = ["attn_w_qkv", "attn_w_o", "rg_w_in", "rg_conv_w", "rg_conv_b", "rg_w_a", "rg_b_a", "rg_w_x", "rg_b_x",
          "rg_lambda", "rg_w_out", "ffn_w_up", "ffn_conv_w", "ffn_conv_b", "ffn_w_down", "mix_pre_g", "mix_post_g",
          "ffn_pre_g", "ffn_post_g"]


def _join(gathered, shapes, axes):
    per_chip = [_unpack(gathered[s], shapes) for s in range(N_CHIPS)]
    return [jnp.concatenate([per_chip[s][i] for s in range(N_CHIPS)], axis=ax) for i, ax in enumerate(axes)]


def _split(full, axis):
    return jnp.split(full, N_CHIPS, axis=axis)


RG_OUT_SHARD_ROWS = D_RNN // N_CHIPS
RG_OUT_SHARD_ROWS_PAD = 352
D_FF_SHARD = D_FF // N_CHIPS


def _two_d(a):
    return a.reshape(-1, a.shape[-1])


def _gather_params(W):
    big = {n: _two_d(W[n]).astype(BF16) for n, _ in _BIG}
    big["rg_w_out"] = jnp.pad(big["rg_w_out"], ((0, RG_OUT_SHARD_ROWS_PAD - RG_OUT_SHARD_ROWS), (0, 0)))
    small_shapes = [W[n].shape for n, _ in _SMALL_SHARDED]
    small_flat = _pack([W[n] for n, _ in _SMALL_SHARDED], 32, F32)
    got = _all_gather_chips([big[n] for n, _ in _BIG] + [small_flat], "gather_weights")
    g = dict(zip([n for n, _ in _BIG], got[:-1]))
    full = {}
    full["attn_w_qkv"] = jnp.swapaxes(g["attn_w_qkv"], 0, 1).reshape(1, D_MODEL, 3 * D_MODEL)
    full["attn_w_o"] = g["attn_w_o"].reshape(1, D_MODEL, D_MODEL)
    full["rg_w_in"] = jnp.swapaxes(g["rg_w_in"], 0, 1).reshape(1, D_MODEL, 2 * D_RNN)
    full["rg_w_out"] = g["rg_w_out"][:, :RG_OUT_SHARD_ROWS].reshape(1, D_RNN, D_MODEL)
    up = g["ffn_w_up"].reshape(N_CHIPS, 2, D_MODEL, 2 * D_FF // N_CHIPS)
    full["ffn_w_up"] = jnp.transpose(up, (1, 2, 0, 3)).reshape(2, D_MODEL, 2 * D_FF)
    down = g["ffn_w_down"].reshape(N_CHIPS, 2, D_FF_SHARD, D_MODEL)
    full["ffn_w_down"] = jnp.swapaxes(down, 0, 1).reshape(2, D_FF, D_MODEL)
    full.update(zip([n for n, _ in _SMALL_SHARDED], _join(got[-1], small_shapes, [ax for _, ax in _SMALL_SHARDED])))
    for n in _REPLICATED:
        full[n] = W[n]
    return full


def _layout_params(full):
    P = {}
    P["attn_w_qkv"] = full["attn_w_qkv"][0]
    P["attn_w_o"] = full["attn_w_o"][0]
    w_in = full["rg_w_in"][0]
    P["rg_w_in_gate"] = _pad_cols(w_in[:, :D_RNN], D_RNN_PAD)
    P["rg_w_in_rec"] = _pad_cols(w_in[:, D_RNN:], D_RNN_PAD)
    P["rg_w_out"] = jnp.pad(full["rg_w_out"][0], ((0, D_RNN_PAD - D_RNN), (0, 0)))
    P["rg_conv_w"] = _rows8(_pad_cols(full["rg_conv_w"][0], D_RNN_PAD))
    P["rg_conv_b"] = _pad_cols(full["rg_conv_b"], D_RNN_PAD)
    P["rg_vecs"] = _rows8(_pad_cols(jnp.concatenate([full["rg_b_a"], full["rg_b_x"], full["rg_lambda"]], axis=0), D_RNN_PAD))
    P["rg_w_ax"] = jnp.concatenate([_block_diag_dense(full["rg_w_a"][0]), _block_diag_dense(full["rg_w_x"][0])], axis=1).astype(BF16)
    P["ffn_w_up"] = [_ffn_interleave(full["ffn_w_up"][l]) for l in range(2)]
    P["ffn_conv_w"] = [_rows8(_ffn_interleave(full["ffn_conv_w"][l])) for l in range(2)]
    P["ffn_conv_b"] = [_ffn_interleave(full["ffn_conv_b"][l:l + 1]) for l in range(2)]
    P["ffn_w_down"] = [full["ffn_w_down"][l] for l in range(2)]
    for n in ("mix_pre_g", "mix_post_g", "ffn_pre_g", "ffn_post_g"):
        P[n] = [full[n][l:l + 1] for l in range(2)]
    return P


def _assemble_grads(G):
    out = {}
    out["attn_w_qkv"] = G["attn_w_qkv"][None]
    out["attn_w_o"] = G["attn_w_o"][None]
    out["rg_w_in"] = jnp.concatenate([G["rg_w_in_gate"][:, :D_RNN], G["rg_w_in_rec"][:, :D_RNN]], axis=1)[None]
    out["rg_conv_w"] = G["rg_conv_w"][:, :D_RNN][None]
    out["rg_conv_b"] = G["rg_conv_b"][:D_RNN][None]
    out["rg_w_a"] = _block_diag_extract(G["rg_w_ax"][:, :D_RNN_PAD])[None]
    out["rg_w_x"] = _block_diag_extract(G["rg_w_ax"][:, D_RNN_PAD:])[None]
    out["rg_b_a"] = G["rg_b_a"][:D_RNN][None]
    out["rg_b_x"] = G["rg_b_x"][:D_RNN][None]
    out["rg_lambda"] = G["rg_lambda"][:D_RNN][None]
    out["rg_w_out"] = G["rg_w_out"][:D_RNN][None]
    out["ffn_w_up"] = jnp.stack([_ffn_deinterleave(G[f"ffn_w_up{l}"]) for l in range(2)])
    out["ffn_conv_w"] = jnp.stack([_ffn_deinterleave(G[f"ffn_conv_w{l}"]) for l in range(2)])
    out["ffn_conv_b"] = jnp.stack([_ffn_deinterleave(G[f"ffn_conv_b{l}"]) for l in range(2)])
    out["ffn_w_down"] = jnp.stack([G[f"ffn_w_down{l}"] for l in range(2)])
    for n in ("mix_pre_g", "mix_post_g", "ffn_pre_g", "ffn_post_g"):
        out[n] = jnp.concatenate([G[n + "0"], G[n + "1"]], axis=0)
    return out


def _big_grad_stacks(G):
    st = {}
    st["attn_w_qkv"] = jnp.swapaxes(G["attn_w_qkv"].reshape(D_MODEL, N_CHIPS, -1), 0, 1)
    st["attn_w_o"] = G["attn_w_o"].reshape(N_CHIPS, -1, D_MODEL)
    w_in = jnp.concatenate([G["rg_w_in_gate"][:, :D_RNN], G["rg_w_in_rec"][:, :D_RNN]], axis=1)
    st["rg_w_in"] = jnp.swapaxes(w_in.reshape(D_MODEL, N_CHIPS, -1), 0, 1)
    st["rg_w_out"] = G["rg_w_out"][:D_RNN].reshape(N_CHIPS, RG_OUT_SHARD_ROWS, D_MODEL)
    up = jnp.stack([_ffn_deinterleave(G[f"ffn_w_up{l}"]) for l in range(2)]).reshape(2, D_MODEL, N_CHIPS, -1)
    st["ffn_w_up"] = jnp.transpose(up, (2, 0, 1, 3)).reshape(N_CHIPS, 2 * D_MODEL, -1)
    down = jnp.stack([G[f"ffn_w_down{l}"] for l in range(2)]).reshape(2, N_CHIPS, D_FF_SHARD, D_MODEL)
    st["ffn_w_down"] = jnp.swapaxes(down, 0, 1).reshape(N_CHIPS, 2 * D_FF_SHARD, D_MODEL)
    return st


def _reduce_grads(G, W):
    big_names = [n for n, _ in _BIG]
    stacks = _big_grad_stacks(G)
    small_grads = _assemble_grads(G)
    rep_flat = jnp.concatenate([small_grads[n].reshape(-1) for n in _REPLICATED])
    n_rep = rep_flat.shape[0]
    piece = -(-n_rep // (N_CHIPS * FLAT_W)) * FLAT_W
    rep_flat = jnp.pad(rep_flat, (0, N_CHIPS * piece - n_rep)).reshape(N_CHIPS, piece)
    small_blocks = []
    for s in range(N_CHIPS):
        parts = [_split(small_grads[n], ax)[s] for n, ax in _SMALL_SHARDED] + [rep_flat[s]]
        small_blocks.append(_pack(parts, 32, F32))
    gs = [stacks[n] for n in big_names] + [jnp.stack(small_blocks)]
    tags = big_names + ["small"]
    landed = _pair_send_other_halves(gs, "rs_pair_send")
    chip_sums = [_add_own_half(g, l, f"rs_pair_add_{t}", F32 if t == "small" else BF16) for g, l, t in zip(gs, landed, tags)]
    slots = _scatter_to_chips(chip_sums, "rs_scatter")
    halves = [_sum_slots(s, f"rs_sum_{t}") for s, t in zip(slots, tags)]
    mine = _pair_gather(halves, "rs_pair_gather")
    out = {n: m.reshape(W[n].shape) for n, m in zip(big_names, mine[:-1])}
    shapes = [W[n].shape for n, _ in _SMALL_SHARDED] + [(piece,)]
    parts = _unpack(mine[-1], shapes)
    out.update(zip([n for n, _ in _SMALL_SHARDED], parts[:-1]))
    rep_rows = piece // FLAT_W
    rep_pad = -(-rep_rows // 32) * 32
    rep_piece = jnp.pad(parts[-1].reshape(rep_rows, FLAT_W), ((0, rep_pad - rep_rows), (0, 0)))
    rep_all = _all_gather_chips([rep_piece], "gather_replicated_grads")[0][:, :rep_rows].reshape(-1)
    out.update(zip(_REPLICATED, _unpack(rep_all, [W[n].shape for n in _REPLICATED])))
    return out


def _update(W, M, V, grads):
    delta, new_m, new_v = {}, {}, {}
    small = [n for n in _ORDER if n not in dict(_BIG)]
    for n, _ in _BIG:
        shp = W[n].shape
        two_d = (-1, shp[-1])
        d, m, v = _adamw(W[n].reshape(two_d), grads[n].reshape(two_d), M[n].reshape(two_d), V[n].reshape(two_d), f"adamw_{n}")
        delta[n], new_m[n], new_v[n] = d.reshape(shp), m.reshape(shp), v.reshape(shp)
    packed = [_pack([src[n] for n in small], 8, F32) for src in (W, grads, M, V)]
    outs = _adamw(*packed, "adamw_small")
    shapes = [W[n].shape for n in small]
    for tgt, flat in zip((delta, new_m, new_v), outs):
        tgt.update(zip(small, _unpack(flat, shapes)))
    return delta, new_m, new_v


def kernel(x, attn_w_qkv, attn_w_o, rg_w_in, rg_conv_w, rg_conv_b, rg_w_a, rg_b_a, rg_w_x, rg_b_x, rg_lambda, rg_w_out, ffn_w_up, ffn_conv_w, ffn_conv_b, ffn_w_down, mix_pre_g, mix_post_g, ffn_pre_g, ffn_post_g, loss_target, m_attn_w_qkv, m_attn_w_o, m_rg_w_in, m_rg_conv_w, m_rg_conv_b, m_rg_w_a, m_rg_b_a, m_rg_w_x, m_rg_b_x, m_rg_lambda, m_rg_w_out, m_ffn_w_up, m_ffn_conv_w, m_ffn_conv_b, m_ffn_w_down, m_mix_pre_g, m_mix_post_g, m_ffn_pre_g, m_ffn_post_g, v_attn_w_qkv, v_attn_w_o, v_rg_w_in, v_rg_conv_w, v_rg_conv_b, v_rg_w_a, v_rg_b_a, v_rg_w_x, v_rg_b_x, v_rg_lambda, v_rg_w_out, v_ffn_w_up, v_ffn_conv_w, v_ffn_conv_b, v_ffn_w_down, v_mix_pre_g, v_mix_post_g, v_ffn_pre_g, v_ffn_post_g):
    W = dict(attn_w_qkv=attn_w_qkv, attn_w_o=attn_w_o, rg_w_in=rg_w_in, rg_conv_w=rg_conv_w, rg_conv_b=rg_conv_b,
             rg_w_a=rg_w_a, rg_b_a=rg_b_a, rg_w_x=rg_w_x, rg_b_x=rg_b_x, rg_lambda=rg_lambda, rg_w_out=rg_w_out,
             ffn_w_up=ffn_w_up, ffn_conv_w=ffn_conv_w, ffn_conv_b=ffn_conv_b, ffn_w_down=ffn_w_down,
             mix_pre_g=mix_pre_g, mix_post_g=mix_post_g, ffn_pre_g=ffn_pre_g, ffn_post_g=ffn_post_g)
    M = dict(attn_w_qkv=m_attn_w_qkv, attn_w_o=m_attn_w_o, rg_w_in=m_rg_w_in, rg_conv_w=m_rg_conv_w, rg_conv_b=m_rg_conv_b,
             rg_w_a=m_rg_w_a, rg_b_a=m_rg_b_a, rg_w_x=m_rg_w_x, rg_b_x=m_rg_b_x, rg_lambda=m_rg_lambda, rg_w_out=m_rg_w_out,
             ffn_w_up=m_ffn_w_up, ffn_conv_w=m_ffn_conv_w, ffn_conv_b=m_ffn_conv_b, ffn_w_down=m_ffn_w_down,
             mix_pre_g=m_mix_pre_g, mix_post_g=m_mix_post_g, ffn_pre_g=m_ffn_pre_g, ffn_post_g=m_ffn_post_g)
    V = dict(attn_w_qkv=v_attn_w_qkv, attn_w_o=v_attn_w_o, rg_w_in=v_rg_w_in, rg_conv_w=v_rg_conv_w, rg_conv_b=v_rg_conv_b,
             rg_w_a=v_rg_w_a, rg_b_a=v_rg_b_a, rg_w_x=v_rg_w_x, rg_b_x=v_rg_b_x, rg_lambda=v_rg_lambda, rg_w_out=v_rg_w_out,
             ffn_w_up=v_ffn_w_up, ffn_conv_w=v_ffn_conv_w, ffn_conv_b=v_ffn_conv_b, ffn_w_down=v_ffn_w_down,
             mix_pre_g=v_mix_pre_g, mix_post_g=v_mix_post_g, ffn_pre_g=v_ffn_pre_g, ffn_post_g=v_ffn_post_g)
    P = _layout_params(_gather_params(W))
    loss_part, grad_x, G = _local_step(x[0], loss_target[0], P)
    loss = lax.psum(loss_part, ("x", "y", "c"))
    grads = _reduce_grads(G, W)
    delta, new_m, new_v = _update(W, M, V, grads)
    return (loss, grad_x[None], *[grads[n] for n in _ORDER], *[delta[n] for n in _ORDER],
            *[new_m[n] for n in _ORDER], *[new_v[n] for n in _ORDER])
```

```python
import functools
import math

import jax
import jax.numpy as jnp
from jax import lax
from jax.experimental import pallas as pl
from jax.experimental.pallas import tpu as pltpu

D_MODEL = 1024
N_HEADS = 16
HEAD_DIM = 64
D_RNN = 1344
D_RNN_PAD = 1408
RG_BLOCKS = 16
RG_BLOCK_W = 84
D_FF = 2816
FFN_TILE = 256
NORM_EPS = 1e-6
RG_C = 8.0

ADAM_LR = 0.001
ADAM_B1 = 0.9
ADAM_B2 = 0.999
ADAM_EPS = 1e-08
ADAM_WD = 0.01
ADAM_STEP = 10

LANES = 128
SUBLANES = 8
VMEM_LIMIT = 56 * 1024 * 1024
FLAT_W = 1024
N_CHIPS = 4

F32 = jnp.float32
BF16 = jnp.bfloat16
MESH = pl.DeviceIdType.MESH


def _cp(*sem):
    return pltpu.CompilerParams(dimension_semantics=tuple(sem), vmem_limit_bytes=VMEM_LIMIT)


def _mm(a, b, mode, out_dtype, name, bm=1024, bn=1024, bk=1024, add=None):
    if mode == "nn":
        (M, K), (K2, N) = a.shape, b.shape
    elif mode == "nt":
        (M, K), (N, K2) = a.shape, b.shape
    else:
        (K, M), (K2, N) = a.shape, b.shape
    assert K == K2, (a.shape, b.shape, mode)
    bm, bn, bk = min(bm, M), min(bn, N), min(bk, K)
    assert M % bm == 0 and N % bn == 0 and K % bk == 0, (M, N, K, bm, bn, bk)
    nk = K // bk
    dims = {"nn": (((1,), (0,)), ((), ())), "nt": (((1,), (1,)), ((), ())), "tn": (((0,), (0,)), ((), ()))}[mode]
    has_add = add is not None

    def body(*refs):
        a_ref, b_ref = refs[0], refs[1]
        add_ref = refs[2] if has_add else None
        o_ref = refs[2 + has_add]
        p = lax.dot_general(a_ref[...].astype(BF16), b_ref[...].astype(BF16), dims, preferred_element_type=F32)
        if nk == 1:
            if has_add:
                p = p + add_ref[...]
            o_ref[...] = p.astype(o_ref.dtype)
        else:
            acc = refs[3 + has_add]
            k = pl.program_id(2)

            @pl.when(k == 0)
            def _():
                acc[...] = p

            @pl.when(k > 0)
            def _():
                acc[...] += p

            @pl.when(k == nk - 1)
            def _():
                r = acc[...]
                if has_add:
                    r = r + add_ref[...]
                o_ref[...] = r.astype(o_ref.dtype)

    if mode == "tn":
        a_spec = pl.BlockSpec((bk, bm), lambda i, j, k: (k, i))
    else:
        a_spec = pl.BlockSpec((bm, bk), lambda i, j, k: (i, k))
    if mode == "nt":
        b_spec = pl.BlockSpec((bn, bk), lambda i, j, k: (j, k))
    else:
        b_spec = pl.BlockSpec((bk, bn), lambda i, j, k: (k, j))
    o_spec = pl.BlockSpec((bm, bn), lambda i, j, k: (i, j))
    in_specs = [a_spec, b_spec] + ([o_spec] if has_add else [])
    args = (a, b) + ((add,) if has_add else ())
    return pl.pallas_call(
        body, name=name, grid=(M // bm, N // bn, nk),
        in_specs=in_specs, out_specs=o_spec,
        out_shape=jax.ShapeDtypeStruct((M, N), out_dtype),
        scratch_shapes=[pltpu.VMEM((bm, bn), F32)] if nk > 1 else [],
        compiler_params=_cp("parallel", "parallel", "arbitrary"),
    )(*args)


def _row_block(T, bm=512):
    bm = min(bm, T)
    assert T % bm == 0
    return bm


def _rmsnorm_fwd(x, g, name):
    T, D = x.shape
    bm = _row_block(T)

    def body(x_ref, g_ref, o_ref):
        xv = x_ref[...]
        inv = lax.rsqrt(jnp.mean(xv * xv, axis=-1, keepdims=True) + NORM_EPS)
        o_ref[...] = (xv * inv * g_ref[...]).astype(o_ref.dtype)

    return pl.pallas_call(
        body, name=name, grid=(T // bm,),
        in_specs=[pl.BlockSpec((bm, D), lambda i: (i, 0)), pl.BlockSpec((1, D), lambda i: (0, 0))],
        out_specs=pl.BlockSpec((bm, D), lambda i: (i, 0)),
        out_shape=jax.ShapeDtypeStruct((T, D), BF16),
        compiler_params=_cp("parallel"),
    )(x, g)


def _postnorm_residual(x, h, g, name):
    T, D = x.shape
    bm = _row_block(T)

    def body(x_ref, h_ref, g_ref, o_ref):
        hv = h_ref[...]
        inv = lax.rsqrt(jnp.mean(hv * hv, axis=-1, keepdims=True) + NORM_EPS)
        o_ref[...] = x_ref[...] + hv * inv * g_ref[...]

    return pl.pallas_call(
        body, name=name, grid=(T // bm,),
        in_specs=[pl.BlockSpec((bm, D), lambda i: (i, 0)), pl.BlockSpec((bm, D), lambda i: (i, 0)),
                  pl.BlockSpec((1, D), lambda i: (0, 0))],
        out_specs=pl.BlockSpec((bm, D), lambda i: (i, 0)),
        out_shape=jax.ShapeDtypeStruct((T, D), F32),
        compiler_params=_cp("parallel"),
    )(x, h, g)


def _postnorm_loss(x, h, g, target, name):
    T, D = x.shape
    bm = _row_block(T)

    def body(x_ref, h_ref, g_ref, t_ref, dy_ref, loss_ref):
        hv = h_ref[...]
        inv = lax.rsqrt(jnp.mean(hv * hv, axis=-1, keepdims=True) + NORM_EPS)
        err = x_ref[...] + hv * inv * g_ref[...] - t_ref[...]
        dy_ref[...] = err * (1.0 / D)
        part = jnp.sum(jnp.sum(err * err, axis=-1, keepdims=True), axis=0, keepdims=True) * (0.5 / D)

        @pl.when(pl.program_id(0) == 0)
        def _():
            loss_ref[...] = jnp.zeros_like(loss_ref)

        loss_ref[...] += jnp.broadcast_to(part, loss_ref.shape)

    row = pl.BlockSpec((bm, D), lambda i: (i, 0))
    return pl.pallas_call(
        body, name=name, grid=(T // bm,),
        in_specs=[row, row, pl.BlockSpec((1, D), lambda i: (0, 0)), row],
        out_specs=[row, pl.BlockSpec((1, LANES), lambda i: (0, 0))],
        out_shape=[jax.ShapeDtypeStruct((T, D), F32), jax.ShapeDtypeStruct((1, LANES), F32)],
        compiler_params=_cp("arbitrary"),
    )(x, h, g, target)


def _rmsnorm_bwd(dy, xin, g, name, res=None, out_dtype=F32):
    T, D = xin.shape
    bm = _row_block(T)
    has_res = res is not None

    def body(*refs):
        dy_ref, x_ref, g_ref = refs[:3]
        res_ref = refs[3] if has_res else None
        dx_ref, dg_ref = refs[3 + has_res], refs[4 + has_res]
        xv = x_ref[...]
        dyv = dy_ref[...].astype(F32)
        inv = lax.rsqrt(jnp.mean(xv * xv, axis=-1, keepdims=True) + NORM_EPS)
        xhat = xv * inv
        dyg = dyv * g_ref[...]
        c = jnp.mean(dyg * xhat, axis=-1, keepdims=True)
        dx = inv * (dyg - xhat * c)
        if has_res:
            dx = dx + res_ref[...]
        dx_ref[...] = dx.astype(dx_ref.dtype)

        @pl.when(pl.program_id(0) == 0)
        def _():
            dg_ref[...] = jnp.zeros_like(dg_ref)

        dg_ref[...] += jnp.sum(dyv * xhat, axis=0, keepdims=True)

    row = pl.BlockSpec((bm, D), lambda i: (i, 0))
    vec = pl.BlockSpec((1, D), lambda i: (0, 0))
    args = (dy, xin, g) + ((res,) if has_res else ())
    return pl.pallas_call(
        body, name=name, grid=(T // bm,),
        in_specs=[row, row, vec] + ([row] if has_res else []),
        out_specs=[row, vec],
        out_shape=[jax.ShapeDtypeStruct((T, D), out_dtype), jax.ShapeDtypeStruct((1, D), F32)],
        compiler_params=_cp("arbitrary"),
    )(*args)


_GELU_C = math.sqrt(2.0 / math.pi)


def _gelu(x):
    return 0.5 * x * (1.0 + jnp.tanh(_GELU_C * (x + 0.044715 * x * x * x)))


def _gelu_and_grad(x):
    t = jnp.tanh(_GELU_C * (x + 0.044715 * x * x * x))
    val = 0.5 * x * (1.0 + t)
    grad = 0.5 * (1.0 + t) + 0.5 * x * (1.0 - t * t) * _GELU_C * (1.0 + 3.0 * 0.044715 * x * x)
    return val, grad


def _sigmoid(x):
    return 1.0 / (1.0 + jnp.exp(-x))


def _shift_down(x, halo_prev, k):
    xe = jnp.concatenate([halo_prev, x], axis=0)
    return pltpu.roll(xe, k, 0)[SUBLANES:]


def _shift_up(x, halo_next, k):
    xe = jnp.concatenate([x, halo_next], axis=0)
    return pltpu.roll(xe, xe.shape[0] - k, 0)[:x.shape[0]]


def _halo_specs(T, bm, bc):
    r = bm // SUBLANES
    last = T // SUBLANES - 1
    prev = pl.BlockSpec((SUBLANES, bc), lambda j, i: (jnp.maximum(i * r - 1, 0), j))
    nxt = pl.BlockSpec((SUBLANES, bc), lambda j, i: (jnp.minimum((i + 1) * r, last), j))
    return prev, nxt


def _causal_conv(x, halo, w_ref, b_ref, width):
    out = x * w_ref[width - 1:width, :] + b_ref[...]
    for k in range(width - 1):
        out = out + _shift_down(x, halo, width - 1 - k) * w_ref[k:k + 1, :]
    return out


def _gate_value_specs(shape, row_map, nj):
    return [pl.BlockSpec(shape, lambda j, i: (row_map(i), j)), pl.BlockSpec(shape, lambda j, i: (row_map(i), j + nj))]


def _side_by_side(g_ref, v_ref):
    return jnp.concatenate([g_ref[...], v_ref[...]], axis=1)


def _ffn_act_fwd(u_pre, conv_w, conv_b, name):
    T, C2 = u_pre.shape
    bm = _row_block(T, 1024)
    nj = C2 // (2 * FFN_TILE)
    r = bm // SUBLANES

    def body(ug_ref, uv_ref, hg_ref, hv_ref, wg_ref, wv_ref, bg_ref, bv_ref, o_ref):
        i = pl.program_id(1)
        halo = jnp.where(i > 0, _side_by_side(hg_ref, hv_ref), 0.0)
        x = _side_by_side(ug_ref, uv_ref)
        w = _side_by_side(wg_ref, wv_ref)
        u = (_shift_down(x, halo, 2) * w[0:1, :] + _shift_down(x, halo, 1) * w[1:2, :] + x * w[2:3, :]
             + _side_by_side(bg_ref, bv_ref))
        o_ref[...] = (_gelu(u[:, :FFN_TILE]) * u[:, FFN_TILE:]).astype(o_ref.dtype)

    in_specs = (_gate_value_specs((bm, FFN_TILE), lambda i: i, nj)
                + _gate_value_specs((SUBLANES, FFN_TILE), lambda i: jnp.maximum(i * r - 1, 0), nj)
                + _gate_value_specs((SUBLANES, FFN_TILE), lambda i: 0, nj)
                + _gate_value_specs((1, FFN_TILE), lambda i: 0, nj))
    return pl.pallas_call(
        body, name=name, grid=(nj, T // bm), in_specs=in_specs,
        out_specs=pl.BlockSpec((bm, FFN_TILE), lambda j, i: (i, j)),
        out_shape=jax.ShapeDtypeStruct((T, C2 // 2), BF16),
        compiler_params=_cp("parallel", "parallel"),
    )(u_pre, u_pre, u_pre, u_pre, conv_w, conv_w, conv_b, conv_b)


def _ffn_act_bwd(u_pre, conv_w, conv_b, d_act, name):
    T, C2 = u_pre.shape
    bm = _row_block(T, 1024)
    bc = 2 * FFN_TILE
    nj = C2 // bc
    nt = T // bm
    r = bm // SUBLANES

    def body(ug_ref, uv_ref, hg_ref, hv_ref, wg_ref, wv_ref, bg_ref, bv_ref, da_ref, dxg_ref, dxv_ref, stg_ref, stv_ref, later):
        i = pl.program_id(1)

        @pl.when(i == 0)
        def _():
            stg_ref[...] = jnp.zeros_like(stg_ref)
            stv_ref[...] = jnp.zeros_like(stv_ref)
            later[...] = jnp.zeros_like(later)

        halo = jnp.where(i < nt - 1, _side_by_side(hg_ref, hv_ref), 0.0)
        x = _side_by_side(ug_ref, uv_ref)
        w = _side_by_side(wg_ref, wv_ref)
        xm1 = _shift_down(x, halo, 1)
        xm2 = _shift_down(x, halo, 2)
        u = xm2 * w[0:1, :] + xm1 * w[1:2, :] + x * w[2:3, :] + _side_by_side(bg_ref, bv_ref)
        gv, gg = _gelu_and_grad(u[:, :FFN_TILE])
        da = da_ref[...]
        du = jnp.concatenate([da * u[:, FFN_TILE:] * gg, da * gv], axis=1)
        nxt = later[...]
        dx = du * w[2:3, :] + _shift_up(du, nxt, 1) * w[1:2, :] + _shift_up(du, nxt, 2) * w[0:1, :]
        dxg_ref[...] = dx[:, :FFN_TILE].astype(dxg_ref.dtype)
        dxv_ref[...] = dx[:, FFN_TILE:].astype(dxv_ref.dtype)
        later[...] = du[:SUBLANES, :]
        rows = [jnp.sum(du * xm2, axis=0, keepdims=True), jnp.sum(du * xm1, axis=0, keepdims=True),
                jnp.sum(du * x, axis=0, keepdims=True), jnp.sum(du, axis=0, keepdims=True),
                jnp.zeros((4, bc), F32)]
        st = jnp.concatenate(rows, axis=0)
        stg_ref[...] += st[:, :FFN_TILE]
        stv_ref[...] += st[:, FFN_TILE:]

    rev = lambda i: nt - 1 - i
    in_specs = (_gate_value_specs((bm, FFN_TILE), rev, nj)
                + _gate_value_specs((SUBLANES, FFN_TILE), lambda i: jnp.maximum((nt - 1 - i) * r - 1, 0), nj)
                + _gate_value_specs((SUBLANES, FFN_TILE), lambda i: 0, nj)
                + _gate_value_specs((1, FFN_TILE), lambda i: 0, nj)
                + [pl.BlockSpec((bm, FFN_TILE), lambda j, i: (nt - 1 - i, j))])
    rows_spec = pl.BlockSpec((bm, FFN_TILE), lambda j, i: (nt - 1 - i, j))
    stat_spec = pl.BlockSpec((SUBLANES, FFN_TILE), lambda j, i: (0, j))
    dxg, dxv, stg, stv = pl.pallas_call(
        body, name=name, grid=(nj, nt), in_specs=in_specs,
        out_specs=[rows_spec, rows_spec, stat_spec, stat_spec],
        out_shape=[jax.ShapeDtypeStruct((T, C2 // 2), BF16)] * 2 + [jax.ShapeDtypeStruct((SUBLANES, C2 // 2), F32)] * 2,
        scratch_shapes=[pltpu.VMEM((SUBLANES, bc), F32)],
        compiler_params=_cp("parallel", "arbitrary"),
    )(u_pre, u_pre, u_pre, u_pre, conv_w, conv_w, conv_b, conv_b, d_act)
    return (dxg, dxv), jnp.concatenate([stg, stv], axis=1)


def _conv_transpose(dy, conv_w, width, name, bc):
    T, C = dy.shape
    bm = _row_block(T, 512)
    _, nxt = _halo_specs(T, bm, bc)
    nt = T // bm

    def body(y_ref, h_ref, w_ref, o_ref):
        i = pl.program_id(1)
        halo = jnp.where(i < nt - 1, h_ref[...], 0.0)
        y = y_ref[...]
        out = y * w_ref[width - 1:width, :]
        for k in range(width - 1):
            out = out + _shift_up(y, halo, width - 1 - k) * w_ref[k:k + 1, :]
        o_ref[...] = out.astype(o_ref.dtype)

    return pl.pallas_call(
        body, name=name, grid=(C // bc, nt),
        in_specs=[pl.BlockSpec((bm, bc), lambda j, i: (i, j)), nxt, pl.BlockSpec((SUBLANES, bc), lambda j, i: (0, j))],
        out_specs=pl.BlockSpec((bm, bc), lambda j, i: (i, j)),
        out_shape=jax.ShapeDtypeStruct((T, C), BF16),
        compiler_params=_cp("parallel", "parallel"),
    )(dy, dy, conv_w)


ATT_BK = 128


def _split_dot(x, uu):
    hi = x.astype(BF16)
    lo = (x - hi.astype(F32)).astype(BF16)
    return jnp.dot(jnp.concatenate([hi, lo], axis=1), uu, preferred_element_type=F32)


def _log_sigmoid(z):
    return jnp.minimum(z, 0.0) - jnp.log(1.0 + jnp.exp(-jnp.abs(z)))


ATT_TQ = 512
ATT_KS = 512
ATT_SUB = ATT_KS // ATT_BK


def _suffix_consts():
    r = lax.broadcasted_iota(jnp.int32, (ATT_BK, 2 * ATT_BK), 0)
    c = lax.broadcasted_iota(jnp.int32, (ATT_BK, 2 * ATT_BK), 1)
    m = jnp.where((r > c) | (c >= ATT_BK), 1.0, 0.0).astype(BF16)
    return jnp.concatenate([m, m], axis=0)


def _suffix_sums(x, uu):
    ext = _split_dot(x, uu)
    return ext[:, :ATT_BK], ext[:, ATT_BK:]


def _masked_heads(x_f32, first):
    return (jnp.where(first, x_f32, 0.0).astype(BF16), jnp.where(first, 0.0, x_f32).astype(BF16))


def _unmasked_blocks(block, nfull, ks, T, carry):
    if 2 * ks > T:
        return lax.fori_loop(0, nfull, lambda i, c: block(nfull - 1 - i, ks, c, False), carry)
    carry = lax.fori_loop(0, jnp.bitwise_and(nfull, 1), lambda i, c: block(nfull - 1, ks, c, False), carry)
    pairs = nfull // 2
    return lax.fori_loop(0, pairs, lambda i, c: block(2 * (pairs - 1 - i), 2 * ks, c, False), carry)


def _sb_attn_fwd(qkv, name):
    T = qkv.shape[0]
    tq, ks = min(ATT_TQ, T), min(ATT_KS, T)
    assert T % tq == 0 and T % ks == 0 and ks % tq == 0
    nq = T // tq
    nsub = ks // ATT_BK
    nhp = N_HEADS // 2
    nt = (((1,), (1,)), ((), ()))


    def body(q_ref, k_ref, v_ref, o_ref):
        qi = pl.program_id(1)
        first = lax.broadcasted_iota(jnp.int32, (1, LANES), 1) < HEAD_DIM
        qs = _masked_heads(q_ref[...].astype(F32) * HEAD_DIM ** -0.5, first)
        uu = _suffix_consts()
        tpos = qi * tq + lax.broadcasted_iota(jnp.int32, (tq, ATT_BK), 0)
        col = lax.broadcasted_iota(jnp.int32, (tq, ATT_BK), 1)

        def block(kb, width, carry, masked):
            nsub = width // ATT_BK
            off = pl.multiple_of(kb * ks, ks)
            kblk = k_ref[pl.ds(off, width), :]
            vblk = v_ref[pl.ds(off, width), :]
            out = []
            for h in range(2):
                acc, run = carry[2 * h], carry[2 * h + 1]
                z = lax.dot_general(qs[h], kblk, nt, preferred_element_type=F32)
                lb = _log_sigmoid(z)
                l1m = lb - z
                ws = [None] * nsub
                for j in reversed(range(nsub)):
                    sl = slice(j * ATT_BK, (j + 1) * ATT_BK)
                    l1 = l1m[:, sl]
                    if masked:
                        causal = (off + j * ATT_BK + col) < tpos
                        l1 = jnp.where(causal, l1, 0.0)
                    cs, tot = _suffix_sums(l1, uu)
                    w = jnp.exp(lb[:, sl] + cs + run)
                    ws[j] = jnp.where(causal, w, 0.0) if masked else w
                    run = run + tot
                w = jnp.concatenate(ws, axis=1)
                acc = acc + jnp.dot(w.astype(BF16), vblk, preferred_element_type=F32)
                out += [acc, run]
            return tuple(out)

        nfull = (qi * tq) // ks
        carry = (jnp.zeros((tq, LANES), F32),) * 4
        carry = block(nfull, ks, carry, True)
        carry = _unmasked_blocks(block, nfull, ks, T, carry)
        o_ref[...] = jnp.where(first, carry[0], carry[2])

    return pl.pallas_call(
        body, name=name, grid=(nhp, nq),
        in_specs=[pl.BlockSpec((tq, LANES), lambda p, i: (i, p)),
                  pl.BlockSpec((T, LANES), lambda p, i: (0, nhp + p)),
                  pl.BlockSpec((T, LANES), lambda p, i: (0, 2 * nhp + p))],
        out_specs=pl.BlockSpec((tq, LANES), lambda p, i: (i, p)),
        out_shape=jax.ShapeDtypeStruct((T, D_MODEL), F32),
        compiler_params=_cp("parallel", "parallel"),
    )(qkv, qkv, qkv)


def _sb_attn_bwd(qkv, o, d_o, name):
    T = qkv.shape[0]
    tq, ks = min(ATT_TQ, T), min(ATT_KS, T)
    assert T % tq == 0 and T % ks == 0 and ks % tq == 0
    nq = T // tq
    nks = T // ks
    nsub = ks // ATT_BK
    nhp = N_HEADS // 2
    nt = (((1,), (1,)), ((), ()))

    def body(q_ref, k_ref, v_ref, o_ref, do_ref, dq_ref, dk_ref, dv_ref, dkt_acc, dvt_acc):
        qi = pl.program_id(1)

        @pl.when(qi == 0)
        def _():
            dkt_acc[...] = jnp.zeros_like(dkt_acc)
            dvt_acc[...] = jnp.zeros_like(dvt_acc)

        first = lax.broadcasted_iota(jnp.int32, (1, LANES), 1) < HEAD_DIM
        first_t = lax.broadcasted_iota(jnp.int32, (LANES, 1), 0) < HEAD_DIM
        qf = q_ref[...].astype(F32) * HEAD_DIM ** -0.5
        qs = _masked_heads(qf, first)
        qts = _masked_heads(qf.T, first_t)
        dof = do_ref[...].astype(BF16).astype(F32)
        dos = _masked_heads(dof, first)
        dots = _masked_heads(dof.T, first_t)
        prod = dof * o_ref[...]
        deltas = (jnp.broadcast_to(jnp.sum(jnp.where(first, prod, 0.0), axis=1, keepdims=True), (tq, LANES)),
                  jnp.broadcast_to(jnp.sum(jnp.where(first, 0.0, prod), axis=1, keepdims=True), (tq, LANES)))
        uu = _suffix_consts()
        tpos = qi * tq + lax.broadcasted_iota(jnp.int32, (tq, ATT_BK), 0)
        col = lax.broadcasted_iota(jnp.int32, (tq, ATT_BK), 1)

        def block(kb, width, carry, masked):
            nsub = width // ATT_BK
            off = pl.multiple_of(kb * ks, ks)
            kblk = k_ref[pl.ds(off, width), :]
            vblk = v_ref[pl.ds(off, width), :]
            out = []
            dkt = jnp.zeros((LANES, width), F32)
            dvt = jnp.zeros((LANES, width), F32)
            for h in range(2):
                dq, run, grun = carry[3 * h], carry[3 * h + 1], carry[3 * h + 2]
                z = lax.dot_general(qs[h], kblk, nt, preferred_element_type=F32)
                lb = _log_sigmoid(z)
                l1m = lb - z
                dw = lax.dot_general(dos[h], vblk, nt, preferred_element_type=F32)
                wbs, dzs = [None] * nsub, [None] * nsub
                for j in reversed(range(nsub)):
                    sl = slice(j * ATT_BK, (j + 1) * ATT_BK)
                    l1 = l1m[:, sl]
                    if masked:
                        causal = (off + j * ATT_BK + col) < tpos
                        l1 = jnp.where(causal, l1, 0.0)
                    cs, tot = _suffix_sums(l1, uu)
                    w = jnp.exp(lb[:, sl] + cs + run)
                    if masked:
                        w = jnp.where(causal, w, 0.0)
                    wbs[j] = w.astype(BF16)
                    g = wbs[j].astype(F32) * dw[:, sl]
                    gcs, gtot = _suffix_sums(g, uu)
                    dz = g - jnp.exp(lb[:, sl]) * (deltas[h] - grun - gcs)
                    if masked:
                        dz = jnp.where(causal, dz, 0.0)
                    dzs[j] = dz.astype(BF16)
                    run = run + tot
                    grun = grun + gtot
                wb = jnp.concatenate(wbs, axis=1)
                dzb = jnp.concatenate(dzs, axis=1)
                dq = dq + jnp.dot(dzb, kblk, preferred_element_type=F32)
                dkt = dkt + jnp.dot(qts[h], dzb, preferred_element_type=F32)
                dvt = dvt + jnp.dot(dots[h], wb, preferred_element_type=F32)
                out += [dq, run, grun]
            for part in range(width // ks):
                dkt_acc[kb + part] += dkt[:, part * ks:(part + 1) * ks]
                dvt_acc[kb + part] += dvt[:, part * ks:(part + 1) * ks]
            return tuple(out)

        nfull = (qi * tq) // ks
        carry = (jnp.zeros((tq, LANES), F32),) * 6
        carry = block(nfull, ks, carry, True)
        carry = _unmasked_blocks(block, nfull, ks, T, carry)
        dq_ref[...] = (jnp.where(first, carry[0], carry[3]) * HEAD_DIM ** -0.5).astype(dq_ref.dtype)

        @pl.when(qi == nq - 1)
        def _():
            for kb in range(nks):
                dk_ref[kb * ks:(kb + 1) * ks, :] = dkt_acc[kb].T.astype(dk_ref.dtype)
                dv_ref[kb * ks:(kb + 1) * ks, :] = dvt_acc[kb].T.astype(dv_ref.dtype)

    qspec = pl.BlockSpec((tq, LANES), lambda p, i: (i, p))
    full = pl.BlockSpec((T, LANES), lambda p, i: (0, p))
    return pl.pallas_call(
        body, name=name, grid=(nhp, nq),
        in_specs=[qspec, pl.BlockSpec((T, LANES), lambda p, i: (0, nhp + p)),
                  pl.BlockSpec((T, LANES), lambda p, i: (0, 2 * nhp + p)), qspec, qspec],
        out_specs=[qspec, full, full],
        out_shape=[jax.ShapeDtypeStruct((T, D_MODEL), BF16)] * 3,
        scratch_shapes=[pltpu.VMEM((nks, LANES, ks), F32), pltpu.VMEM((nks, LANES, ks), F32)],
        compiler_params=_cp("parallel", "arbitrary"),
    )(qkv, qkv, qkv, o, d_o)


def _rg_conv_fwd(x, conv_w, conv_b, name):
    T, C = x.shape
    bm = _row_block(T, 512)
    bc = C
    prev, _ = _halo_specs(T, bm, bc)

    def body(x_ref, h_ref, w_ref, b_ref, o_ref):
        halo = jnp.where(pl.program_id(1) > 0, h_ref[...], 0.0)
        o_ref[...] = _causal_conv(x_ref[...], halo, w_ref, b_ref, 4)

    return pl.pallas_call(
        body, name=name, grid=(C // bc, T // bm),
        in_specs=[pl.BlockSpec((bm, bc), lambda j, i: (i, j)), prev,
                  pl.BlockSpec((SUBLANES, bc), lambda j, i: (0, j)), pl.BlockSpec((1, bc), lambda j, i: (0, j))],
        out_specs=pl.BlockSpec((bm, bc), lambda j, i: (i, j)),
        out_shape=jax.ShapeDtypeStruct((T, C), F32),
        compiler_params=_cp("parallel", "parallel"),
    )(x, x, conv_w, conv_b)


def _rg_gate_math(ra, rx, rec, vec_ref):
    r = _sigmoid(ra + vec_ref[0:1, :])
    ig = _sigmoid(rx + vec_ref[1:2, :])
    lam = vec_ref[2:3, :]
    sp = jnp.maximum(-lam, 0.0) + jnp.log(1.0 + jnp.exp(-jnp.abs(lam)))
    log_a = -RG_C * r * sp
    a = jnp.exp(log_a)
    th = jnp.tanh(log_a)
    mult = jnp.sqrt(-2.0 * th / (1.0 - th))
    return r, ig, sp, a, mult


def _rg_gates_fwd(rax, rec, vecs, name):
    T, C = rec.shape
    bm = _row_block(T, 512)

    def body(ra_ref, rx_ref, rec_ref, vec_ref, a_ref, u_ref):
        recv = rec_ref[...]
        _, ig, _, a, mult = _rg_gate_math(ra_ref[...], rx_ref[...], recv, vec_ref)
        a_ref[...] = a
        u_ref[...] = mult * ig * recv

    blk = pl.BlockSpec((bm, C), lambda i: (i, 0))
    return pl.pallas_call(
        body, name=name, grid=(T // bm,),
        in_specs=[blk, pl.BlockSpec((bm, C), lambda i: (i, 1)), blk, pl.BlockSpec((SUBLANES, C), lambda i: (0, 0))],
        out_specs=[blk, blk],
        out_shape=[jax.ShapeDtypeStruct((T, C), F32)] * 2,
        compiler_params=_cp("parallel"),
    )(rax, rax, rec, vecs)


def _linear_scan(a, b, name, reverse=False, shift_coef=False):
    T, C = a.shape
    bm = _row_block(T, 256)
    bc = C
    nt = T // bm
    _, nxt = _halo_specs(T, bm, bc)
    groups = bm // SUBLANES

    def body(*refs):
        if shift_coef:
            a_ref, an_ref, b_ref, h_ref, carry = refs
        else:
            a_ref, b_ref, h_ref, carry = refs
        i = pl.program_id(1)

        @pl.when(i == 0)
        def _():
            carry[...] = jnp.zeros_like(carry)

        coef = a_ref[...]
        if shift_coef:
            coef = _shift_up(coef, an_ref[...], 1)
        val = b_ref[...]
        sub = jnp.bitwise_and(lax.broadcasted_iota(jnp.int32, (bm, bc), 0), SUBLANES - 1)
        for s in (1, 2, 4):
            if reverse:
                keep = sub < SUBLANES - s
                cs, vs = pltpu.roll(coef, bm - s, 0), pltpu.roll(val, bm - s, 0)
            else:
                keep = sub >= s
                cs, vs = pltpu.roll(coef, s, 0), pltpu.roll(val, s, 0)
            val = jnp.where(keep, coef * vs + val, val)
            coef = jnp.where(keep, coef * cs, coef)
        edge = carry[0:1, :]
        pieces = [None] * groups
        for g in (reversed(range(groups)) if reverse else range(groups)):
            rows = slice(g * SUBLANES, (g + 1) * SUBLANES)
            hg = val[rows] + coef[rows] * edge
            pieces[g] = hg
            edge = hg[0:1, :] if reverse else hg[SUBLANES - 1:SUBLANES, :]
        h_ref[...] = jnp.concatenate(pieces, axis=0)
        carry[...] = jnp.broadcast_to(edge, carry.shape)

    if reverse:
        blk = pl.BlockSpec((bm, bc), lambda j, i: (nt - 1 - i, j))
        r = bm // SUBLANES
        last = T // SUBLANES - 1
        nxt = pl.BlockSpec((SUBLANES, bc), lambda j, i: (jnp.minimum((nt - i) * r, last), j))
    else:
        blk = pl.BlockSpec((bm, bc), lambda j, i: (i, j))
    in_specs = [blk] + ([nxt] if shift_coef else []) + [blk]
    args = (a,) + ((a,) if shift_coef else ()) + (b,)
    return pl.pallas_call(
        body, name=name, grid=(C // bc, nt),
        in_specs=in_specs, out_specs=blk,
        out_shape=jax.ShapeDtypeStruct((T, C), F32),
        scratch_shapes=[pltpu.VMEM((SUBLANES, bc), F32)],
        compiler_params=_cp("parallel", "arbitrary"),
    )(*args)


def _rg_out_fwd(gate_branch, h, name):
    T, C = h.shape
    bm = _row_block(T, 512)

    def body(g_ref, h_ref, o_ref):
        o_ref[...] = (_gelu(g_ref[...]) * h_ref[...]).astype(o_ref.dtype)

    blk = pl.BlockSpec((bm, C), lambda i: (i, 0))
    return pl.pallas_call(
        body, name=name, grid=(T // bm,), in_specs=[blk, blk], out_specs=blk,
        out_shape=jax.ShapeDtypeStruct((T, C), BF16), compiler_params=_cp("parallel"),
    )(gate_branch, h)


def _rg_out_bwd(d_gh, gate_branch, h, name):
    T, C = h.shape
    bm = _row_block(T, 512)

    def body(d_ref, g_ref, h_ref, dg_ref, dh_ref):
        gv, gg = _gelu_and_grad(g_ref[...])
        d = d_ref[...]
        dg_ref[...] = (d * h_ref[...] * gg).astype(dg_ref.dtype)
        dh_ref[...] = d * gv

    blk = pl.BlockSpec((bm, C), lambda i: (i, 0))
    return pl.pallas_call(
        body, name=name, grid=(T // bm,), in_specs=[blk, blk, blk], out_specs=[blk, blk],
        out_shape=[jax.ShapeDtypeStruct((T, C), BF16), jax.ShapeDtypeStruct((T, C), F32)],
        compiler_params=_cp("parallel"),
    )(d_gh, gate_branch, h)


def _rg_gates_bwd(dhs, h, rax, rec, vecs, name):
    T, C = rec.shape
    bm = _row_block(T, 1024)
    bc = LANES
    prev, _ = _halo_specs(T, bm, bc)
    nc = C // bc

    def body(d_ref, h_ref, hh_ref, ra_ref, rx_ref, rec_ref, vec_ref, dra_ref, drx_ref, drec_ref, st_ref):
        i = pl.program_id(1)
        halo = jnp.where(i > 0, hh_ref[...], 0.0)
        recv = rec_ref[...]
        du = d_ref[...]
        r, ig, sp, a, mult = _rg_gate_math(ra_ref[...], rx_ref[...], recv, vec_ref)
        da = du * _shift_down(h_ref[...], halo, 1)
        d_i = du * mult * recv
        drec_ref[...] = du * mult * ig
        d_mult = du * ig * recv
        d_log_a = da * a - d_mult * (a * a) / mult
        d_ra = d_log_a * (-RG_C) * sp * r * (1.0 - r)
        d_rx = d_i * ig * (1.0 - ig)
        dra_ref[...] = d_ra.astype(dra_ref.dtype)
        drx_ref[...] = d_rx.astype(drx_ref.dtype)
        lam = vec_ref[2:3, :]
        d_sp = jnp.sum(d_log_a * (-RG_C) * r, axis=0, keepdims=True)
        rows = [jnp.sum(d_ra, axis=0, keepdims=True), jnp.sum(d_rx, axis=0, keepdims=True),
                d_sp * (_sigmoid(lam) - 1.0), jnp.zeros((5, bc), F32)]

        @pl.when(i == 0)
        def _():
            st_ref[...] = jnp.zeros_like(st_ref)

        st_ref[...] += jnp.concatenate(rows, axis=0)

    blk = pl.BlockSpec((bm, bc), lambda j, i: (i, j))
    blk_x = pl.BlockSpec((bm, bc), lambda j, i: (i, nc + j))
    vec = pl.BlockSpec((SUBLANES, bc), lambda j, i: (0, j))
    d_ra, d_rx, d_rec, stats = pl.pallas_call(
        body, name=name, grid=(nc, T // bm),
        in_specs=[blk, blk, prev, blk, blk_x, blk, vec],
        out_specs=[blk, blk, blk, vec],
        out_shape=[jax.ShapeDtypeStruct((T, C), BF16), jax.ShapeDtypeStruct((T, C), BF16),
                   jax.ShapeDtypeStruct((T, C), F32), jax.ShapeDtypeStruct((SUBLANES, C), F32)],
        compiler_params=_cp("parallel", "arbitrary"),
    )(dhs, h, h, rax, rax, rec, vecs)
    return d_ra, d_rx, d_rec, stats


def _rg_conv_bwd_stats(d_rec, x, name):
    T, C = x.shape
    bm = _row_block(T, 1024)
    bc = LANES
    prev, _ = _halo_specs(T, bm, bc)

    def body(d_ref, x_ref, h_ref, st_ref):
        i = pl.program_id(1)
        halo = jnp.where(i > 0, h_ref[...], 0.0)
        d = d_ref[...]
        xv = x_ref[...]
        rows = [jnp.sum(d * _shift_down(xv, halo, 3 - k), axis=0, keepdims=True) for k in range(3)]
        rows += [jnp.sum(d * xv, axis=0, keepdims=True), jnp.sum(d, axis=0, keepdims=True), jnp.zeros((3, bc), F32)]

        @pl.when(i == 0)
        def _():
            st_ref[...] = jnp.zeros_like(st_ref)

        st_ref[...] += jnp.concatenate(rows, axis=0)

    blk = pl.BlockSpec((bm, bc), lambda j, i: (i, j))
    vec = pl.BlockSpec((SUBLANES, bc), lambda j, i: (0, j))
    return pl.pallas_call(
        body, name=name, grid=(C // bc, T // bm),
        in_specs=[blk, blk, prev], out_specs=vec,
        out_shape=jax.ShapeDtypeStruct((SUBLANES, C), F32),
        compiler_params=_cp("parallel", "arbitrary"),
    )(d_rec, x, x)


def _adamw(w, g, m, v, name):
    R, W = w.shape
    bm = R
    for cand in (512, 256, 128, 64, 32, 16, 8):
        if R % cand == 0:
            bm = cand
            break
    bc1 = 1.0 - ADAM_B1 ** ADAM_STEP
    bc2 = 1.0 - ADAM_B2 ** ADAM_STEP

    def body(w_ref, g_ref, m_ref, v_ref, d_ref, nm_ref, nv_ref):
        gv = g_ref[...]
        nm = ADAM_B1 * m_ref[...] + (1.0 - ADAM_B1) * gv
        nv = ADAM_B2 * v_ref[...] + (1.0 - ADAM_B2) * (gv * gv)
        m_hat = nm / bc1
        v_hat = nv / bc2
        d_ref[...] = -ADAM_LR * (m_hat / (jnp.sqrt(v_hat) + ADAM_EPS) + ADAM_WD * w_ref[...])
        nm_ref[...] = nm
        nv_ref[...] = nv

    blk = pl.BlockSpec((bm, W), lambda i: (i, 0))
    return pl.pallas_call(
        body, name=name, grid=(R // bm,), in_specs=[blk] * 4, out_specs=[blk] * 3,
        out_shape=[jax.ShapeDtypeStruct((R, W), F32)] * 3, compiler_params=_cp("parallel"),
    )(w, g, m, v)


_ANY = pl.BlockSpec(memory_space=pl.ANY)


def _me():
    return lax.axis_index("x"), lax.axis_index("y"), lax.axis_index("c")


def _other_chips(x, y):
    return [(1 - x, y), (x, 1 - y), (1 - x, 1 - y)]


def _remote(src, dst, send_sem, recv_sem, to):
    return pltpu.make_async_remote_copy(src_ref=src, dst_ref=dst, send_sem=send_sem, recv_sem=recv_sem,
                                        device_id=to, device_id_type=MESH)


def _half_rows(c, rows_half, align):
    return pl.ds(pl.multiple_of(c * rows_half, align), rows_half)


def _divisor_block(rows, cap=512):
    if rows <= cap:
        return rows
    best = None
    for cand in range(16, cap + 1, 16):
        if rows % cand == 0:
            best = cand
    assert best is not None, rows
    return best


def _all_gather_chips(shards, name):
    n = len(shards)
    halves = [s.shape[0] // 2 for s in shards]
    for s, rh in zip(shards, halves):
        assert 2 * rh == s.shape[0] and rh % 16 == 0, s.shape

    def body(*refs):
        x_refs, out_refs = refs[:n], refs[n:2 * n]
        send_sems, recv_sems = refs[2 * n:]
        x, y, c = _me()
        s_me = 2 * x + y
        chips = _other_chips(x, y)
        sends = []
        for a in range(n):
            half = _half_rows(c, halves[a], 16)
            for j, (cx, cy) in enumerate(chips):
                cp = _remote(x_refs[a].at[half], out_refs[a].at[s_me, half], send_sems.at[6 * a + j],
                             recv_sems.at[6 * a + j], (cx, cy, c))
                cp.start()
                sends.append(cp)
        for a in range(n):
            half = _half_rows(c, halves[a], 16)
            for j, (cx, cy) in enumerate(chips):
                landed = out_refs[a].at[2 * cx + cy, half]
                _remote(landed, landed, send_sems.at[6 * a + j], recv_sems.at[6 * a + j], (cx, cy, c)).wait_recv()
                fwd = _remote(landed, landed, send_sems.at[6 * a + 3 + j], recv_sems.at[6 * a + 3 + j], (x, y, 1 - c))
                fwd.start()
                sends.append(fwd)
        for a in range(n):
            other_half = _half_rows(1 - c, halves[a], 16)
            for j, (cx, cy) in enumerate(chips):
                landed = out_refs[a].at[2 * cx + cy, other_half]
                _remote(landed, landed, send_sems.at[6 * a + 3 + j], recv_sems.at[6 * a + 3 + j], (x, y, 1 - c)).wait_recv()
        for cp in sends:
            cp.wait_send()

    outs = pl.pallas_call(
        body, name=name, in_specs=[_ANY] * n, out_specs=[_ANY] * n,
        out_shape=[jax.ShapeDtypeStruct((N_CHIPS,) + s.shape, s.dtype) for s in shards],
        scratch_shapes=[pltpu.SemaphoreType.DMA((6 * n,)), pltpu.SemaphoreType.DMA((6 * n,))],
    )(*shards)
    s_me = 2 * lax.axis_index("x") + lax.axis_index("y")
    return [lax.dynamic_update_slice(o, s[None], (s_me, 0, 0)) for o, s in zip(outs, shards)]


def _pair_send_other_halves(gs, name):
    n = len(gs)
    halves = [g.shape[1] // 2 for g in gs]
    for g, rh in zip(gs, halves):
        assert 2 * rh == g.shape[1] and rh % 8 == 0, g.shape

    def body(*refs):
        g_refs, land_refs = refs[:n], refs[n:2 * n]
        send_sems, recv_sems = refs[2 * n:]
        x, y, c = _me()
        copies = []
        for a in range(n):
            src = g_refs[a].at[:, _half_rows(1 - c, halves[a], 8), :]
            cp = _remote(src, land_refs[a], send_sems.at[a], recv_sems.at[a], (x, y, 1 - c))
            cp.start()
            copies.append(cp)
        for cp in copies:
            cp.wait()

    return pl.pallas_call(
        body, name=name, in_specs=[_ANY] * n, out_specs=[_ANY] * n,
        out_shape=[jax.ShapeDtypeStruct((g.shape[0], rh, g.shape[2]), g.dtype) for g, rh in zip(gs, halves)],
        scratch_shapes=[pltpu.SemaphoreType.DMA((n,)), pltpu.SemaphoreType.DMA((n,))],
    )(*gs)


def _add_own_half(g, landed, name, out_dtype):
    n, R, W = g.shape
    Rh = R // 2
    bm = _divisor_block(Rh)
    nb = Rh // bm
    c_arr = lax.axis_index("c").astype(jnp.int32).reshape((1,))

    def body(c_ref, g_ref, l_ref, o_ref):
        o_ref[...] = (g_ref[...] + l_ref[...]).astype(o_ref.dtype)

    grid_spec = pltpu.PrefetchScalarGridSpec(
        num_scalar_prefetch=1, grid=(n, nb),
        in_specs=[pl.BlockSpec((1, bm, W), lambda s, i, c_ref: (s, c_ref[0] * nb + i, 0)),
                  pl.BlockSpec((1, bm, W), lambda s, i, c_ref: (s, i, 0))],
        out_specs=pl.BlockSpec((1, bm, W), lambda s, i, c_ref: (s, i, 0)))
    return pl.pallas_call(
        body, name=name, grid_spec=grid_spec, out_shape=jax.ShapeDtypeStruct((n, Rh, W), out_dtype),
        compiler_params=_cp("parallel", "parallel"),
    )(c_arr, g, landed)


def _scatter_to_chips(ps, name):
    n = len(ps)

    def body(*refs):
        p_refs, land_refs = refs[:n], refs[n:2 * n]
        send_sems, recv_sems = refs[2 * n:]
        x, y, c = _me()
        s_me = 2 * x + y
        chips = _other_chips(x, y)
        sends = []
        for a in range(n):
            for j, (cx, cy) in enumerate(chips):
                cp = _remote(p_refs[a].at[2 * cx + cy], land_refs[a].at[s_me], send_sems.at[3 * a + j],
                             recv_sems.at[3 * a + j], (cx, cy, c))
                cp.start()
                sends.append(cp)
        for a in range(n):
            for j, (cx, cy) in enumerate(chips):
                slot = land_refs[a].at[2 * cx + cy]
                _remote(slot, slot, send_sems.at[3 * a + j], recv_sems.at[3 * a + j], (cx, cy, c)).wait_recv()
        for cp in sends:
            cp.wait_send()

    lands = pl.pallas_call(
        body, name=name, in_specs=[_ANY] * n, out_specs=[_ANY] * n,
        out_shape=[jax.ShapeDtypeStruct(p.shape, p.dtype) for p in ps],
        scratch_shapes=[pltpu.SemaphoreType.DMA((3 * n,)), pltpu.SemaphoreType.DMA((3 * n,))],
    )(*ps)
    s_me = 2 * lax.axis_index("x") + lax.axis_index("y")
    return [lax.dynamic_update_slice(l, lax.dynamic_slice(p, (s_me, 0, 0), (1,) + p.shape[1:]), (s_me, 0, 0))
            for l, p in zip(lands, ps)]


def _sum_slots(l, name):
    n, Rh, W = l.shape
    bm = _divisor_block(Rh, 256)
    c_arr = lax.axis_index("c").astype(jnp.int32).reshape((1,))

    def body(c_ref, l_ref, o_ref):
        o_ref[0] = ((l_ref[0].astype(F32) + l_ref[1].astype(F32)) + l_ref[2].astype(F32)) + l_ref[3].astype(F32)

    grid_spec = pltpu.PrefetchScalarGridSpec(
        num_scalar_prefetch=1, grid=(Rh // bm,),
        in_specs=[pl.BlockSpec((n, bm, W), lambda i, c_ref: (0, i, 0))],
        out_specs=pl.BlockSpec((1, bm, W), lambda i, c_ref: (c_ref[0], i, 0)))
    return pl.pallas_call(
        body, name=name, grid_spec=grid_spec,
        out_shape=jax.ShapeDtypeStruct((2, Rh, W), F32), compiler_params=_cp("parallel"),
    )(c_arr, l)


def _pair_gather(fs, name):
    n = len(fs)

    def body(*refs):
        f_refs, out_refs = refs[:n], refs[n:2 * n]
        send_sems, recv_sems = refs[2 * n:]
        x, y, c = _me()
        sends = []
        for a in range(n):
            cp = _remote(f_refs[a].at[c], out_refs[a].at[c], send_sems.at[a], recv_sems.at[a], (x, y, 1 - c))
            cp.start()
            sends.append(cp)
        for a in range(n):
            sib = out_refs[a].at[1 - c]
            _remote(sib, sib, send_sems.at[a], recv_sems.at[a], (x, y, 1 - c)).wait_recv()
        for cp in sends:
            cp.wait_send()

    return pl.pallas_call(
        body, name=name, in_specs=[_ANY] * n, out_specs=[_ANY] * n,
        out_shape=[jax.ShapeDtypeStruct(f.shape, f.dtype) for f in fs],
        input_output_aliases={a: a for a in range(n)},
        scratch_shapes=[pltpu.SemaphoreType.DMA((n,)), pltpu.SemaphoreType.DMA((n,))],
    )(*fs)


def _pack(arrays, rows_multiple, dtype):
    flat = jnp.concatenate([a.reshape(-1).astype(dtype) for a in arrays])
    n = flat.shape[0]
    rows = -(-n // FLAT_W)
    rows = -(-rows // rows_multiple) * rows_multiple
    return jnp.pad(flat, (0, rows * FLAT_W - n)).reshape(rows, FLAT_W)


def _unpack(flat, shapes):
    flat = flat.reshape(-1)
    out, off = [], 0
    for shp in shapes:
        n = math.prod(shp)
        out.append(flat[off:off + n].reshape(shp))
        off += n
    return out


def _pad_cols(a, width):
    return jnp.pad(a, [(0, 0)] * (a.ndim - 1) + [(0, width - a.shape[-1])])


def _rows8(a):
    return jnp.pad(a, ((0, SUBLANES - a.shape[0]), (0, 0)))


def _block_diag_dense(w):
    eye = jnp.eye(RG_BLOCKS, dtype=w.dtype)
    dense = jnp.einsum("ncd,nm->ncmd", w, eye).reshape(D_RNN, D_RNN)
    return jnp.pad(dense, ((0, D_RNN_PAD - D_RNN), (0, D_RNN_PAD - D_RNN)))


def _block_diag_extract(dense):
    d = dense[:D_RNN, :D_RNN].reshape(RG_BLOCKS, RG_BLOCK_W, RG_BLOCKS, RG_BLOCK_W)
    return jnp.stack([d[n, :, n, :] for n in range(RG_BLOCKS)])


def _ffn_fwd(x, l, P, tag):
    n = _rmsnorm_fwd(x, P["ffn_pre_g"][l], f"{tag}_prenorm")
    u_pre = _mm(n, P["ffn_w_up"][l], "nn", F32, f"{tag}_up", bn=2 * D_FF // 4)
    act = _ffn_act_fwd(u_pre, P["ffn_conv_w"][l], P["ffn_conv_b"][l], f"{tag}_act")
    f = _mm(act, P["ffn_w_down"][l], "nn", F32, f"{tag}_down", bk=D_FF)
    return f, (n, u_pre, act)


def _ffn_bwd(df, x, saved, l, P, tag):
    n, u_pre, act = saved
    g_down = _mm(act, df, "tn", F32, f"{tag}_dwdown", bm=D_FF // 2, bk=1024)
    d_act = _mm(df, P["ffn_w_down"][l], "nt", F32, f"{tag}_dact", bn=D_FF // 2)
    (du_g, du_v), stats = _ffn_act_bwd(u_pre, P["ffn_conv_w"][l], P["ffn_conv_b"][l], d_act, f"{tag}_actbwd")
    g_up_g = _mm(n, du_g, "tn", F32, f"{tag}_dwup_gate", bn=D_FF // 2, bk=1024)
    g_up_v = _mm(n, du_v, "tn", F32, f"{tag}_dwup_value", bn=D_FF // 2, bk=1024)
    dn = _mm(du_g, P["ffn_w_up_gate"][l], "nt", F32, f"{tag}_dn_gate", bk=D_FF // 2)
    dn = _mm(du_v, P["ffn_w_up_value"][l], "nt", F32, f"{tag}_dn_value", bk=D_FF // 2, add=dn)
    return dn, dict(ffn_w_down=g_down, ffn_w_up_gate=g_up_g, ffn_w_up_value=g_up_v, ffn_conv_w=stats[0:3], ffn_conv_b=stats[3])


def _local_step(x, target, P):
    G = {}
    n0 = _rmsnorm_fwd(x, P["mix_pre_g"][0], "l0_prenorm")
    qkv = _mm(n0, P["attn_w_qkv"], "nn", BF16, "l0_qkv")
    o = _sb_attn_fwd(qkv, "l0_attn")
    a0 = _mm(o, P["attn_w_o"], "nn", F32, "l0_wo")
    x1 = _postnorm_residual(x, a0, P["mix_post_g"][0], "l0_postnorm")
    f0, ffn0 = _ffn_fwd(x1, 0, P, "f0")
    x2 = _postnorm_residual(x1, f0, P["ffn_post_g"][0], "f0_postnorm")
    n1 = _rmsnorm_fwd(x2, P["mix_pre_g"][1], "l1_prenorm")
    gate_b = _mm(n1, P["rg_w_in_gate"], "nn", F32, "l1_in_gate", bn=D_RNN_PAD)
    rec_b = _mm(n1, P["rg_w_in_rec"], "nn", F32, "l1_in_rec", bn=D_RNN_PAD)
    rec = _rg_conv_fwd(rec_b, P["rg_conv_w"], P["rg_conv_b"], "l1_conv")
    rax = _mm(rec, P["rg_w_ax"], "nn", F32, "l1_gates_mm", bn=D_RNN_PAD, bk=D_RNN_PAD)
    a, u = _rg_gates_fwd(rax, rec, P["rg_vecs"], "l1_gates")
    h = _linear_scan(a, u, "l1_scan")
    gh = _rg_out_fwd(gate_b, h, "l1_gate_out")
    y1 = _mm(gh, P["rg_w_out"], "nn", F32, "l1_wout", bk=D_RNN_PAD)
    x3 = _postnorm_residual(x2, y1, P["mix_post_g"][1], "l1_postnorm")
    f1, ffn1 = _ffn_fwd(x3, 1, P, "f1")
    dx, loss = _postnorm_loss(x3, f1, P["ffn_post_g"][1], target, "loss")

    df, G["ffn_post_g1"] = _rmsnorm_bwd(dx, f1, P["ffn_post_g"][1], "f1_postnorm_bwd", out_dtype=BF16)
    dn, g = _ffn_bwd(df, x3, ffn1, 1, P, "f1")
    G.update({k + "1": v for k, v in g.items()})
    dx, G["ffn_pre_g1"] = _rmsnorm_bwd(dn, x3, P["ffn_pre_g"][1], "f1_prenorm_bwd", res=dx)
    dy, G["mix_post_g1"] = _rmsnorm_bwd(dx, y1, P["mix_post_g"][1], "l1_postnorm_bwd", out_dtype=BF16)
    G["rg_w_out"] = _mm(gh, dy, "tn", F32, "l1_dwout", bm=D_RNN_PAD, bk=1024)
    d_gh = _mm(dy, P["rg_w_out"], "nt", F32, "l1_dgh", bn=D_RNN_PAD)
    d_gate, d_h = _rg_out_bwd(d_gh, gate_b, h, "l1_gate_out_bwd")
    dhs = _linear_scan(a, d_h, "l1_scan_bwd", reverse=True, shift_coef=True)
    d_ra, d_rx, d_rec_direct, st = _rg_gates_bwd(dhs, h, rax, rec, P["rg_vecs"], "l1_gates_bwd")
    G["rg_b_a"], G["rg_b_x"], G["rg_lambda"] = st[0], st[1], st[2]
    d_rax = jnp.concatenate([d_ra, d_rx], axis=1)
    G["rg_w_ax"] = _mm(rec, d_rax, "tn", F32, "l1_dwax", bm=D_RNN_PAD, bn=D_RNN_PAD, bk=1024)
    d_rec = _mm(d_rax, P["rg_w_ax"], "nt", F32, "l1_drec", bm=512, bn=D_RNN_PAD, bk=2 * D_RNN_PAD, add=d_rec_direct)
    st = _rg_conv_bwd_stats(d_rec, rec_b, "l1_conv_stats")
    G["rg_conv_w"], G["rg_conv_b"] = st[0:4], st[4]
    d_rec_b = _conv_transpose(d_rec, P["rg_conv_w"], 4, "l1_convT", D_RNN_PAD)
    G["rg_w_in_gate"] = _mm(n1, d_gate, "tn", F32, "l1_dwin_gate", bn=D_RNN_PAD, bk=1024)
    G["rg_w_in_rec"] = _mm(n1, d_rec_b, "tn", F32, "l1_dwin_rec", bn=D_RNN_PAD, bk=1024)
    dn = _mm(d_gate, P["rg_w_in_gate"], "nt", F32, "l1_dn_gate", bk=D_RNN_PAD)
    dn = _mm(d_rec_b, P["rg_w_in_rec"], "nt", F32, "l1_dn_rec", bk=D_RNN_PAD, add=dn)
    dx, G["mix_pre_g1"] = _rmsnorm_bwd(dn, x2, P["mix_pre_g"][1], "l1_prenorm_bwd", res=dx)
    df, G["ffn_post_g0"] = _rmsnorm_bwd(dx, f0, P["ffn_post_g"][0], "f0_postnorm_bwd", out_dtype=BF16)
    dn, g = _ffn_bwd(df, x1, ffn0, 0, P, "f0")
    G.update({k + "0": v for k, v in g.items()})
    dx, G["ffn_pre_g0"] = _rmsnorm_bwd(dn, x1, P["ffn_pre_g"][0], "f0_prenorm_bwd", res=dx)
    da, G["mix_post_g0"] = _rmsnorm_bwd(dx, a0, P["mix_post_g"][0], "l0_postnorm_bwd", out_dtype=BF16)
    G["attn_w_o"] = _mm(o, da, "tn", F32, "l0_dwo", bk=1024)
    d_o = _mm(da, P["attn_w_o"], "nt", F32, "l0_do")
    dq, dk, dv = _sb_attn_bwd(qkv, o, d_o, "l0_attn_bwd")
    d_qkv = jnp.concatenate([dq, dk, dv], axis=1)
    G["attn_w_qkv"] = _mm(n0, d_qkv, "tn", F32, "l0_dwqkv", bk=1024)
    dn = _mm(d_qkv, P["attn_w_qkv"], "nt", F32, "l0_dn")
    dx, G["mix_pre_g0"] = _rmsnorm_bwd(dn, x, P["mix_pre_g"][0], "l0_prenorm_bwd", res=dx)
    return loss[0, 0], dx, G


_BIG = [("attn_w_qkv", 2), ("attn_w_o", 1), ("rg_w_in", 2), ("rg_w_out", 1), ("ffn_w_up", 2), ("ffn_w_down", 1)]
_SMALL_SHARDED = [("rg_conv_w", 2), ("rg_conv_b", 1), ("rg_b_a", 1), ("rg_b_x", 1), ("rg_lambda", 1), ("ffn_conv_w", 2)]
_REPLICATED = ["rg_w_a", "rg_w_x", "ffn_conv_b", "mix_pre_g", "mix_post_g", "ffn_pre_g", "ffn_post_g"]
_ORDER = ["attn_w_qkv", "attn_w_o", "rg_w_in", "rg_conv_w", "rg_conv_b", "rg_w_a", "rg_b_a", "rg_w_x", "rg_b_x",
          "rg_lambda", "rg_w_out", "ffn_w_up", "ffn_conv_w", "ffn_conv_b", "ffn_w_down", "mix_pre_g", "mix_post_g",
          "ffn_pre_g", "ffn_post_g"]


def _join(gathered, shapes, axes):
    per_chip = [_unpack(gathered[s], shapes) for s in range(N_CHIPS)]
    return [jnp.concatenate([per_chip[s][i] for s in range(N_CHIPS)], axis=ax) for i, ax in enumerate(axes)]


def _split(full, axis):
    return jnp.split(full, N_CHIPS, axis=axis)


RG_OUT_SHARD_ROWS = D_RNN // N_CHIPS
RG_OUT_SHARD_ROWS_PAD = 352
D_FF_SHARD = D_FF // N_CHIPS


def _two_d(a):
    return a.reshape(-1, a.shape[-1])


def _gather_params(W):
    big = {n: _two_d(W[n]).astype(BF16) for n, _ in _BIG}
    big["rg_w_out"] = jnp.pad(big["rg_w_out"], ((0, RG_OUT_SHARD_ROWS_PAD - RG_OUT_SHARD_ROWS), (0, 0)))
    small_shapes = [W[n].shape for n, _ in _SMALL_SHARDED]
    small_flat = _pack([W[n] for n, _ in _SMALL_SHARDED], 32, F32)
    got = _all_gather_chips([big[n] for n, _ in _BIG] + [small_flat], "gather_weights")
    g = dict(zip([n for n, _ in _BIG], got[:-1]))
    full = {}
    full["attn_w_qkv"] = jnp.swapaxes(g["attn_w_qkv"], 0, 1).reshape(1, D_MODEL, 3 * D_MODEL)
    full["attn_w_o"] = g["attn_w_o"].reshape(1, D_MODEL, D_MODEL)
    full["rg_w_in"] = jnp.swapaxes(g["rg_w_in"], 0, 1).reshape(1, D_MODEL, 2 * D_RNN)
    full["rg_w_out"] = g["rg_w_out"][:, :RG_OUT_SHARD_ROWS].reshape(1, D_RNN, D_MODEL)
    up = g["ffn_w_up"].reshape(N_CHIPS, 2, D_MODEL, 2 * D_FF // N_CHIPS)
    full["ffn_w_up"] = jnp.transpose(up, (1, 2, 0, 3)).reshape(2, D_MODEL, 2 * D_FF)
    down = g["ffn_w_down"].reshape(N_CHIPS, 2, D_FF_SHARD, D_MODEL)
    full["ffn_w_down"] = jnp.swapaxes(down, 0, 1).reshape(2, D_FF, D_MODEL)
    full.update(zip([n for n, _ in _SMALL_SHARDED], _join(got[-1], small_shapes, [ax for _, ax in _SMALL_SHARDED])))
    for n in _REPLICATED:
        full[n] = W[n]
    return full


def _layout_params(full):
    P = {}
    P["attn_w_qkv"] = full["attn_w_qkv"][0]
    P["attn_w_o"] = full["attn_w_o"][0]
    w_in = full["rg_w_in"][0]
    P["rg_w_in_gate"] = _pad_cols(w_in[:, :D_RNN], D_RNN_PAD)
    P["rg_w_in_rec"] = _pad_cols(w_in[:, D_RNN:], D_RNN_PAD)
    P["rg_w_out"] = jnp.pad(full["rg_w_out"][0], ((0, D_RNN_PAD - D_RNN), (0, 0)))
    P["rg_conv_w"] = _rows8(_pad_cols(full["rg_conv_w"][0], D_RNN_PAD))
    P["rg_conv_b"] = _pad_cols(full["rg_conv_b"], D_RNN_PAD)
    P["rg_vecs"] = _rows8(_pad_cols(jnp.concatenate([full["rg_b_a"], full["rg_b_x"], full["rg_lambda"]], axis=0), D_RNN_PAD))
    P["rg_w_ax"] = jnp.concatenate([_block_diag_dense(full["rg_w_a"][0]), _block_diag_dense(full["rg_w_x"][0])], axis=1).astype(BF16)
    P["ffn_w_up"] = [full["ffn_w_up"][l] for l in range(2)]
    P["ffn_w_up_gate"] = [full["ffn_w_up"][l][:, :D_FF] for l in range(2)]
    P["ffn_w_up_value"] = [full["ffn_w_up"][l][:, D_FF:] for l in range(2)]
    P["ffn_conv_w"] = [_rows8(full["ffn_conv_w"][l]) for l in range(2)]
    P["ffn_conv_b"] = [full["ffn_conv_b"][l:l + 1] for l in range(2)]
    P["ffn_w_down"] = [full["ffn_w_down"][l] for l in range(2)]
    for n in ("mix_pre_g", "mix_post_g", "ffn_pre_g", "ffn_post_g"):
        P[n] = [full[n][l:l + 1] for l in range(2)]
    return P


def _assemble_grads(G):
    out = {}
    out["attn_w_qkv"] = G["attn_w_qkv"][None]
    out["attn_w_o"] = G["attn_w_o"][None]
    out["rg_w_in"] = jnp.concatenate([G["rg_w_in_gate"][:, :D_RNN], G["rg_w_in_rec"][:, :D_RNN]], axis=1)[None]
    out["rg_conv_w"] = G["rg_conv_w"][:, :D_RNN][None]
    out["rg_conv_b"] = G["rg_conv_b"][:D_RNN][None]
    out["rg_w_a"] = _block_diag_extract(G["rg_w_ax"][:, :D_RNN_PAD])[None]
    out["rg_w_x"] = _block_diag_extract(G["rg_w_ax"][:, D_RNN_PAD:])[None]
    out["rg_b_a"] = G["rg_b_a"][:D_RNN][None]
    out["rg_b_x"] = G["rg_b_x"][:D_RNN][None]
    out["rg_lambda"] = G["rg_lambda"][:D_RNN][None]
    out["rg_w_out"] = G["rg_w_out"][:D_RNN][None]
    out["ffn_w_up"] = jnp.stack([jnp.concatenate([G[f"ffn_w_up_gate{l}"], G[f"ffn_w_up_value{l}"]], axis=1) for l in range(2)])
    out["ffn_conv_w"] = jnp.stack([G[f"ffn_conv_w{l}"] for l in range(2)])
    out["ffn_conv_b"] = jnp.stack([G[f"ffn_conv_b{l}"] for l in range(2)])
    out["ffn_w_down"] = jnp.stack([G[f"ffn_w_down{l}"] for l in range(2)])
    for n in ("mix_pre_g", "mix_post_g", "ffn_pre_g", "ffn_post_g"):
        out[n] = jnp.concatenate([G[n + "0"], G[n + "1"]], axis=0)
    return out


def _big_grad_stacks(G):
    st = {}
    st["attn_w_qkv"] = jnp.swapaxes(G["attn_w_qkv"].reshape(D_MODEL, N_CHIPS, -1), 0, 1)
    st["attn_w_o"] = G["attn_w_o"].reshape(N_CHIPS, -1, D_MODEL)
    w_in = jnp.concatenate([G["rg_w_in_gate"][:, :D_RNN], G["rg_w_in_rec"][:, :D_RNN]], axis=1)
    st["rg_w_in"] = jnp.swapaxes(w_in.reshape(D_MODEL, N_CHIPS, -1), 0, 1)
    st["rg_w_out"] = G["rg_w_out"][:D_RNN].reshape(N_CHIPS, RG_OUT_SHARD_ROWS, D_MODEL)
    half = D_FF // 2
    pieces = [[G[f"ffn_w_up_{part}{l}"][:, h * half:(h + 1) * half] for l in range(2)]
              for part in ("gate", "value") for h in range(2)]
    st["ffn_w_up"] = jnp.stack([jnp.concatenate(p, axis=0) for p in pieces])
    down = jnp.stack([G[f"ffn_w_down{l}"] for l in range(2)]).reshape(2, N_CHIPS, D_FF_SHARD, D_MODEL)
    st["ffn_w_down"] = jnp.swapaxes(down, 0, 1).reshape(N_CHIPS, 2 * D_FF_SHARD, D_MODEL)
    return st


def _reduce_grads(G, W):
    big_names = [n for n, _ in _BIG]
    stacks = _big_grad_stacks(G)
    small_grads = _assemble_grads(G)
    rep_flat = jnp.concatenate([small_grads[n].reshape(-1) for n in _REPLICATED])
    n_rep = rep_flat.shape[0]
    piece = -(-n_rep // (N_CHIPS * FLAT_W)) * FLAT_W
    rep_flat = jnp.pad(rep_flat, (0, N_CHIPS * piece - n_rep)).reshape(N_CHIPS, piece)
    small_blocks = []
    for s in range(N_CHIPS):
        parts = [_split(small_grads[n], ax)[s] for n, ax in _SMALL_SHARDED] + [rep_flat[s]]
        small_blocks.append(_pack(parts, 32, F32))
    gs = [stacks[n] for n in big_names] + [jnp.stack(small_blocks)]
    tags = big_names + ["small"]
    landed = _pair_send_other_halves(gs, "rs_pair_send")
    chip_sums = [_add_own_half(g, l, f"rs_pair_add_{t}", F32 if t == "small" else BF16) for g, l, t in zip(gs, landed, tags)]
    slots = _scatter_to_chips(chip_sums, "rs_scatter")
    halves = [_sum_slots(s, f"rs_sum_{t}") for s, t in zip(slots, tags)]
    mine = _pair_gather(halves, "rs_pair_gather")
    out = {n: m.reshape(W[n].shape) for n, m in zip(big_names, mine[:-1])}
    shapes = [W[n].shape for n, _ in _SMALL_SHARDED] + [(piece,)]
    parts = _unpack(mine[-1], shapes)
    out.update(zip([n for n, _ in _SMALL_SHARDED], parts[:-1]))
    rep_rows = piece // FLAT_W
    rep_pad = -(-rep_rows // 32) * 32
    rep_piece = jnp.pad(parts[-1].reshape(rep_rows, FLAT_W), ((0, rep_pad - rep_rows), (0, 0)))
    rep_all = _all_gather_chips([rep_piece], "gather_replicated_grads")[0][:, :rep_rows].reshape(-1)
    out.update(zip(_REPLICATED, _unpack(rep_all, [W[n].shape for n in _REPLICATED])))
    return out


def _update(W, M, V, grads):
    delta, new_m, new_v = {}, {}, {}
    small = [n for n in _ORDER if n not in dict(_BIG)]
    for n, _ in _BIG:
        shp = W[n].shape
        two_d = (-1, shp[-1])
        d, m, v = _adamw(W[n].reshape(two_d), grads[n].reshape(two_d), M[n].reshape(two_d), V[n].reshape(two_d), f"adamw_{n}")
        delta[n], new_m[n], new_v[n] = d.reshape(shp), m.reshape(shp), v.reshape(shp)
    packed = [_pack([src[n] for n in small], 8, F32) for src in (W, grads, M, V)]
    outs = _adamw(*packed, "adamw_small")
    shapes = [W[n].shape for n in small]
    for tgt, flat in zip((delta, new_m, new_v), outs):
        tgt.update(zip(small, _unpack(flat, shapes)))
    return delta, new_m, new_v


def kernel(x, attn_w_qkv, attn_w_o, rg_w_in, rg_conv_w, rg_conv_b, rg_w_a, rg_b_a, rg_w_x, rg_b_x, rg_lambda, rg_w_out, ffn_w_up, ffn_conv_w, ffn_conv_b, ffn_w_down, mix_pre_g, mix_post_g, ffn_pre_g, ffn_post_g, loss_target, m_attn_w_qkv, m_attn_w_o, m_rg_w_in, m_rg_conv_w, m_rg_conv_b, m_rg_w_a, m_rg_b_a, m_rg_w_x, m_rg_b_x, m_rg_lambda, m_rg_w_out, m_ffn_w_up, m_ffn_conv_w, m_ffn_conv_b, m_ffn_w_down, m_mix_pre_g, m_mix_post_g, m_ffn_pre_g, m_ffn_post_g, v_attn_w_qkv, v_attn_w_o, v_rg_w_in, v_rg_conv_w, v_rg_conv_b, v_rg_w_a, v_rg_b_a, v_rg_w_x, v_rg_b_x, v_rg_lambda, v_rg_w_out, v_ffn_w_up, v_ffn_conv_w, v_ffn_conv_b, v_ffn_w_down, v_mix_pre_g, v_mix_post_g, v_ffn_pre_g, v_ffn_post_g):
    W = dict(attn_w_qkv=attn_w_qkv, attn_w_o=attn_w_o, rg_w_in=rg_w_in, rg_conv_w=rg_conv_w, rg_conv_b=rg_conv_b,
             rg_w_a=rg_w_a, rg_b_a=rg_b_a, rg_w_x=rg_w_x, rg_b_x=rg_b_x, rg_lambda=rg_lambda, rg_w_out=rg_w_out,
             ffn_w_up=ffn_w_up, ffn_conv_w=ffn_conv_w, ffn_conv_b=ffn_conv_b, ffn_w_down=ffn_w_down,
             mix_pre_g=mix_pre_g, mix_post_g=mix_post_g, ffn_pre_g=ffn_pre_g, ffn_post_g=ffn_post_g)
    M = dict(attn_w_qkv=m_attn_w_qkv, attn_w_o=m_attn_w_o, rg_w_in=m_rg_w_in, rg_conv_w=m_rg_conv_w, rg_conv_b=m_rg_conv_b,
             rg_w_a=m_rg_w_a, rg_b_a=m_rg_b_a, rg_w_x=m_rg_w_x, rg_b_x=m_rg_b_x, rg_lambda=m_rg_lambda, rg_w_out=m_rg_w_out,
             ffn_w_up=m_ffn_w_up, ffn_conv_w=m_ffn_conv_w, ffn_conv_b=m_ffn_conv_b, ffn_w_down=m_ffn_w_down,
             mix_pre_g=m_mix_pre_g, mix_post_g=m_mix_post_g, ffn_pre_g=m_ffn_pre_g, ffn_post_g=m_ffn_post_g)
    V = dict(attn_w_qkv=v_attn_w_qkv, attn_w_o=v_attn_w_o, rg_w_in=v_rg_w_in, rg_conv_w=v_rg_conv_w, rg_conv_b=v_rg_conv_b,
             rg_w_a=v_rg_w_a, rg_b_a=v_rg_b_a, rg_w_x=v_rg_w_x, rg_b_x=v_rg_b_x, rg_lambda=v_rg_lambda, rg_w_out=v_rg_w_out,
             ffn_w_up=v_ffn_w_up, ffn_conv_w=v_ffn_conv_w, ffn_conv_b=v_ffn_conv_b, ffn_w_down=v_ffn_w_down,
             mix_pre_g=v_mix_pre_g, mix_post_g=v_mix_post_g, ffn_pre_g=v_ffn_pre_g, ffn_post_g=v_ffn_post_g)
    P = _layout_params(_gather_params(W))
    loss_part, grad_x, G = _local_step(x[0], loss_target[0], P)
    loss = lax.psum(loss_part, ("x", "y", "c"))
    grads = _reduce_grads(G, W)
    delta, new_m, new_v = _update(W, M, V, grads)
    return (loss, grad_x[None], *[grads[n] for n in _ORDER], *[delta[n] for n in _ORDER],
            *[new_m[n] for n in _ORDER], *[new_v[n] for n in _ORDER])
```

```python
import functools
import math

import jax
import jax.numpy as jnp
from jax import lax
from jax.experimental import pallas as pl
from jax.experimental.pallas import tpu as pltpu

D_MODEL = 1024
N_HEADS = 16
HEAD_DIM = 64
D_RNN = 1344
D_RNN_PAD = 1408
RG_BLOCKS = 16
RG_BLOCK_W = 84
D_FF = 2816
FFN_TILE = 256
NORM_EPS = 1e-6
RG_C = 8.0

ADAM_LR = 0.001
ADAM_B1 = 0.9
ADAM_B2 = 0.999
ADAM_EPS = 1e-08
ADAM_WD = 0.01
ADAM_STEP = 10

LANES = 128
SUBLANES = 8
VMEM_LIMIT = 56 * 1024 * 1024
FLAT_W = 1024
N_CHIPS = 4

F32 = jnp.float32
BF16 = jnp.bfloat16
MESH = pl.DeviceIdType.MESH


def _cp(*sem):
    return pltpu.CompilerParams(dimension_semantics=tuple(sem), vmem_limit_bytes=VMEM_LIMIT)


def _mm(a, b, mode, out_dtype, name, bm=1024, bn=1024, bk=1024, add=None):
    if mode == "nn":
        (M, K), (K2, N) = a.shape, b.shape
    elif mode == "nt":
        (M, K), (N, K2) = a.shape, b.shape
    else:
        (K, M), (K2, N) = a.shape, b.shape
    assert K == K2, (a.shape, b.shape, mode)
    bm, bn, bk = min(bm, M), min(bn, N), min(bk, K)
    assert M % bm == 0 and N % bn == 0 and K % bk == 0, (M, N, K, bm, bn, bk)
    nk = K // bk
    dims = {"nn": (((1,), (0,)), ((), ())), "nt": (((1,), (1,)), ((), ())), "tn": (((0,), (0,)), ((), ()))}[mode]
    has_add = add is not None

    def body(*refs):
        a_ref, b_ref = refs[0], refs[1]
        add_ref = refs[2] if has_add else None
        o_ref = refs[2 + has_add]
        p = lax.dot_general(a_ref[...].astype(BF16), b_ref[...].astype(BF16), dims, preferred_element_type=F32)
        if nk == 1:
            if has_add:
                p = p + add_ref[...]
            o_ref[...] = p.astype(o_ref.dtype)
        else:
            acc = refs[3 + has_add]
            k = pl.program_id(2)

            @pl.when(k == 0)
            def _():
                acc[...] = p

            @pl.when(k > 0)
            def _():
                acc[...] += p

            @pl.when(k == nk - 1)
            def _():
                r = acc[...]
                if has_add:
                    r = r + add_ref[...]
                o_ref[...] = r.astype(o_ref.dtype)

    if mode == "tn":
        a_spec = pl.BlockSpec((bk, bm), lambda i, j, k: (k, i))
    else:
        a_spec = pl.BlockSpec((bm, bk), lambda i, j, k: (i, k))
    if mode == "nt":
        b_spec = pl.BlockSpec((bn, bk), lambda i, j, k: (j, k))
    else:
        b_spec = pl.BlockSpec((bk, bn), lambda i, j, k: (k, j))
    o_spec = pl.BlockSpec((bm, bn), lambda i, j, k: (i, j))
    in_specs = [a_spec, b_spec] + ([o_spec] if has_add else [])
    args = (a, b) + ((add,) if has_add else ())
    return pl.pallas_call(
        body, name=name, grid=(M // bm, N // bn, nk),
        in_specs=in_specs, out_specs=o_spec,
        out_shape=jax.ShapeDtypeStruct((M, N), out_dtype),
        scratch_shapes=[pltpu.VMEM((bm, bn), F32)] if nk > 1 else [],
        compiler_params=_cp("parallel", "parallel", "arbitrary"),
    )(*args)


def _row_block(T, bm=512):
    bm = min(bm, T)
    assert T % bm == 0
    return bm


def _rmsnorm_fwd(x, g, name):
    T, D = x.shape
    bm = _row_block(T)

    def body(x_ref, g_ref, o_ref):
        xv = x_ref[...]
        inv = lax.rsqrt(jnp.mean(xv * xv, axis=-1, keepdims=True) + NORM_EPS)
        o_ref[...] = (xv * inv * g_ref[...]).astype(o_ref.dtype)

    return pl.pallas_call(
        body, name=name, grid=(T // bm,),
        in_specs=[pl.BlockSpec((bm, D), lambda i: (i, 0)), pl.BlockSpec((1, D), lambda i: (0, 0))],
        out_specs=pl.BlockSpec((bm, D), lambda i: (i, 0)),
        out_shape=jax.ShapeDtypeStruct((T, D), BF16),
        compiler_params=_cp("parallel"),
    )(x, g)


def _postnorm_residual(x, h, g, name):
    T, D = x.shape
    bm = _row_block(T)

    def body(x_ref, h_ref, g_ref, o_ref):
        hv = h_ref[...]
        inv = lax.rsqrt(jnp.mean(hv * hv, axis=-1, keepdims=True) + NORM_EPS)
        o_ref[...] = x_ref[...] + hv * inv * g_ref[...]

    return pl.pallas_call(
        body, name=name, grid=(T // bm,),
        in_specs=[pl.BlockSpec((bm, D), lambda i: (i, 0)), pl.BlockSpec((bm, D), lambda i: (i, 0)),
                  pl.BlockSpec((1, D), lambda i: (0, 0))],
        out_specs=pl.BlockSpec((bm, D), lambda i: (i, 0)),
        out_shape=jax.ShapeDtypeStruct((T, D), F32),
        compiler_params=_cp("parallel"),
    )(x, h, g)


def _postnorm_loss(x, h, g, target, name):
    T, D = x.shape
    bm = _row_block(T)

    def body(x_ref, h_ref, g_ref, t_ref, dy_ref, loss_ref):
        hv = h_ref[...]
        inv = lax.rsqrt(jnp.mean(hv * hv, axis=-1, keepdims=True) + NORM_EPS)
        err = x_ref[...] + hv * inv * g_ref[...] - t_ref[...]
        dy_ref[...] = err * (1.0 / D)
        part = jnp.sum(jnp.sum(err * err, axis=-1, keepdims=True), axis=0, keepdims=True) * (0.5 / D)

        @pl.when(pl.program_id(0) == 0)
        def _():
            loss_ref[...] = jnp.zeros_like(loss_ref)

        loss_ref[...] += jnp.broadcast_to(part, loss_ref.shape)

    row = pl.BlockSpec((bm, D), lambda i: (i, 0))
    return pl.pallas_call(
        body, name=name, grid=(T // bm,),
        in_specs=[row, row, pl.BlockSpec((1, D), lambda i: (0, 0)), row],
        out_specs=[row, pl.BlockSpec((1, LANES), lambda i: (0, 0))],
        out_shape=[jax.ShapeDtypeStruct((T, D), F32), jax.ShapeDtypeStruct((1, LANES), F32)],
        compiler_params=_cp("arbitrary"),
    )(x, h, g, target)


def _rmsnorm_bwd(dy, xin, g, name, res=None, out_dtype=F32):
    T, D = xin.shape
    bm = _row_block(T)
    has_res = res is not None

    def body(*refs):
        dy_ref, x_ref, g_ref = refs[:3]
        res_ref = refs[3] if has_res else None
        dx_ref, dg_ref = refs[3 + has_res], refs[4 + has_res]
        xv = x_ref[...]
        dyv = dy_ref[...].astype(F32)
        inv = lax.rsqrt(jnp.mean(xv * xv, axis=-1, keepdims=True) + NORM_EPS)
        xhat = xv * inv
        dyg = dyv * g_ref[...]
        c = jnp.mean(dyg * xhat, axis=-1, keepdims=True)
        dx = inv * (dyg - xhat * c)
        if has_res:
            dx = dx + res_ref[...]
        dx_ref[...] = dx.astype(dx_ref.dtype)

        @pl.when(pl.program_id(0) == 0)
        def _():
            dg_ref[...] = jnp.zeros_like(dg_ref)

        dg_ref[...] += jnp.sum(dyv * xhat, axis=0, keepdims=True)

    row = pl.BlockSpec((bm, D), lambda i: (i, 0))
    vec = pl.BlockSpec((1, D), lambda i: (0, 0))
    args = (dy, xin, g) + ((res,) if has_res else ())
    return pl.pallas_call(
        body, name=name, grid=(T // bm,),
        in_specs=[row, row, vec] + ([row] if has_res else []),
        out_specs=[row, vec],
        out_shape=[jax.ShapeDtypeStruct((T, D), out_dtype), jax.ShapeDtypeStruct((1, D), F32)],
        compiler_params=_cp("arbitrary"),
    )(*args)


_GELU_C = math.sqrt(2.0 / math.pi)


def _gelu(x):
    return 0.5 * x * (1.0 + jnp.tanh(_GELU_C * (x + 0.044715 * x * x * x)))


def _gelu_and_grad(x):
    t = jnp.tanh(_GELU_C * (x + 0.044715 * x * x * x))
    val = 0.5 * x * (1.0 + t)
    grad = 0.5 * (1.0 + t) + 0.5 * x * (1.0 - t * t) * _GELU_C * (1.0 + 3.0 * 0.044715 * x * x)
    return val, grad


def _sigmoid(x):
    return 1.0 / (1.0 + jnp.exp(-x))


def _shift_down(x, halo_prev, k):
    xe = jnp.concatenate([halo_prev, x], axis=0)
    return pltpu.roll(xe, k, 0)[SUBLANES:]


def _shift_up(x, halo_next, k):
    xe = jnp.concatenate([x, halo_next], axis=0)
    return pltpu.roll(xe, xe.shape[0] - k, 0)[:x.shape[0]]


def _halo_specs(T, bm, bc):
    r = bm // SUBLANES
    last = T // SUBLANES - 1
    prev = pl.BlockSpec((SUBLANES, bc), lambda j, i: (jnp.maximum(i * r - 1, 0), j))
    nxt = pl.BlockSpec((SUBLANES, bc), lambda j, i: (jnp.minimum((i + 1) * r, last), j))
    return prev, nxt


def _causal_conv(x, halo, w_ref, b_ref, width):
    out = x * w_ref[width - 1:width, :] + b_ref[...]
    for k in range(width - 1):
        out = out + _shift_down(x, halo, width - 1 - k) * w_ref[k:k + 1, :]
    return out


def _gate_value_specs(shape, row_map, nj):
    return [pl.BlockSpec(shape, lambda j, i: (row_map(i), j)), pl.BlockSpec(shape, lambda j, i: (row_map(i), j + nj))]


def _side_by_side(g_ref, v_ref):
    return jnp.concatenate([g_ref[...], v_ref[...]], axis=1)


def _ffn_act_fwd(u_pre, conv_w, conv_b, name):
    T, C2 = u_pre.shape
    bm = _row_block(T, 1024)
    nj = C2 // (2 * FFN_TILE)
    r = bm // SUBLANES

    def body(ug_ref, uv_ref, hg_ref, hv_ref, wg_ref, wv_ref, bg_ref, bv_ref, o_ref):
        i = pl.program_id(1)
        halo = jnp.where(i > 0, _side_by_side(hg_ref, hv_ref), 0.0)
        x = _side_by_side(ug_ref, uv_ref)
        w = _side_by_side(wg_ref, wv_ref)
        u = (_shift_down(x, halo, 2) * w[0:1, :] + _shift_down(x, halo, 1) * w[1:2, :] + x * w[2:3, :]
             + _side_by_side(bg_ref, bv_ref))
        o_ref[...] = (_gelu(u[:, :FFN_TILE]) * u[:, FFN_TILE:]).astype(o_ref.dtype)

    in_specs = (_gate_value_specs((bm, FFN_TILE), lambda i: i, nj)
                + _gate_value_specs((SUBLANES, FFN_TILE), lambda i: jnp.maximum(i * r - 1, 0), nj)
                + _gate_value_specs((SUBLANES, FFN_TILE), lambda i: 0, nj)
                + _gate_value_specs((1, FFN_TILE), lambda i: 0, nj))
    return pl.pallas_call(
        body, name=name, grid=(nj, T // bm), in_specs=in_specs,
        out_specs=pl.BlockSpec((bm, FFN_TILE), lambda j, i: (i, j)),
        out_shape=jax.ShapeDtypeStruct((T, C2 // 2), BF16),
        compiler_params=_cp("parallel", "parallel"),
    )(u_pre, u_pre, u_pre, u_pre, conv_w, conv_w, conv_b, conv_b)


def _ffn_act_bwd(u_pre, conv_w, conv_b, d_act, name):
    T, C2 = u_pre.shape
    bm = _row_block(T, 1024)
    bc = 2 * FFN_TILE
    nj = C2 // bc
    nt = T // bm
    r = bm // SUBLANES

    def body(ug_ref, uv_ref, hg_ref, hv_ref, wg_ref, wv_ref, bg_ref, bv_ref, da_ref, dxg_ref, dxv_ref, stg_ref, stv_ref, later):
        i = pl.program_id(1)

        @pl.when(i == 0)
        def _():
            stg_ref[...] = jnp.zeros_like(stg_ref)
            stv_ref[...] = jnp.zeros_like(stv_ref)
            later[...] = jnp.zeros_like(later)

        halo = jnp.where(i < nt - 1, _side_by_side(hg_ref, hv_ref), 0.0)
        x = _side_by_side(ug_ref, uv_ref)
        w = _side_by_side(wg_ref, wv_ref)
        xm1 = _shift_down(x, halo, 1)
        xm2 = _shift_down(x, halo, 2)
        u = xm2 * w[0:1, :] + xm1 * w[1:2, :] + x * w[2:3, :] + _side_by_side(bg_ref, bv_ref)
        gv, gg = _gelu_and_grad(u[:, :FFN_TILE])
        da = da_ref[...]
        du = jnp.concatenate([da * u[:, FFN_TILE:] * gg, da * gv], axis=1)
        nxt = later[...]
        dx = du * w[2:3, :] + _shift_up(du, nxt, 1) * w[1:2, :] + _shift_up(du, nxt, 2) * w[0:1, :]
        dxg_ref[...] = dx[:, :FFN_TILE].astype(dxg_ref.dtype)
        dxv_ref[...] = dx[:, FFN_TILE:].astype(dxv_ref.dtype)
        later[...] = du[:SUBLANES, :]
        rows = [jnp.sum(du * xm2, axis=0, keepdims=True), jnp.sum(du * xm1, axis=0, keepdims=True),
                jnp.sum(du * x, axis=0, keepdims=True), jnp.sum(du, axis=0, keepdims=True),
                jnp.zeros((4, bc), F32)]
        st = jnp.concatenate(rows, axis=0)
        stg_ref[...] += st[:, :FFN_TILE]
        stv_ref[...] += st[:, FFN_TILE:]

    rev = lambda i: nt - 1 - i
    in_specs = (_gate_value_specs((bm, FFN_TILE), rev, nj)
                + _gate_value_specs((SUBLANES, FFN_TILE), lambda i: jnp.maximum((nt - 1 - i) * r - 1, 0), nj)
                + _gate_value_specs((SUBLANES, FFN_TILE), lambda i: 0, nj)
                + _gate_value_specs((1, FFN_TILE), lambda i: 0, nj)
                + [pl.BlockSpec((bm, FFN_TILE), lambda j, i: (nt - 1 - i, j))])
    rows_spec = pl.BlockSpec((bm, FFN_TILE), lambda j, i: (nt - 1 - i, j))
    stat_spec = pl.BlockSpec((SUBLANES, FFN_TILE), lambda j, i: (0, j))
    dxg, dxv, stg, stv = pl.pallas_call(
        body, name=name, grid=(nj, nt), in_specs=in_specs,
        out_specs=[rows_spec, rows_spec, stat_spec, stat_spec],
        out_shape=[jax.ShapeDtypeStruct((T, C2 // 2), BF16)] * 2 + [jax.ShapeDtypeStruct((SUBLANES, C2 // 2), F32)] * 2,
        scratch_shapes=[pltpu.VMEM((SUBLANES, bc), F32)],
        compiler_params=_cp("parallel", "arbitrary"),
    )(u_pre, u_pre, u_pre, u_pre, conv_w, conv_w, conv_b, conv_b, d_act)
    return (dxg, dxv), jnp.concatenate([stg, stv], axis=1)


def _conv_transpose(dy, conv_w, width, name, bc):
    T, C = dy.shape
    bm = _row_block(T, 512)
    _, nxt = _halo_specs(T, bm, bc)
    nt = T // bm

    def body(y_ref, h_ref, w_ref, o_ref):
        i = pl.program_id(1)
        halo = jnp.where(i < nt - 1, h_ref[...], 0.0)
        y = y_ref[...]
        out = y * w_ref[width - 1:width, :]
        for k in range(width - 1):
            out = out + _shift_up(y, halo, width - 1 - k) * w_ref[k:k + 1, :]
        o_ref[...] = out.astype(o_ref.dtype)

    return pl.pallas_call(
        body, name=name, grid=(C // bc, nt),
        in_specs=[pl.BlockSpec((bm, bc), lambda j, i: (i, j)), nxt, pl.BlockSpec((SUBLANES, bc), lambda j, i: (0, j))],
        out_specs=pl.BlockSpec((bm, bc), lambda j, i: (i, j)),
        out_shape=jax.ShapeDtypeStruct((T, C), BF16),
        compiler_params=_cp("parallel", "parallel"),
    )(dy, dy, conv_w)


ATT_BK = 128


def _split_dot(x, uu):
    hi = x.astype(BF16)
    lo = (x - hi.astype(F32)).astype(BF16)
    return jnp.dot(jnp.concatenate([hi, lo], axis=1), uu, preferred_element_type=F32)


_SIGN_BIT = -2 ** 31


def _log_sigmoid(z):
    neg_abs = pltpu.bitcast(pltpu.bitcast(z, jnp.int32) | jnp.int32(_SIGN_BIT), F32)
    return jnp.minimum(z, 0.0) - jnp.log(1.0 + jnp.exp(neg_abs))


ATT_TQ = 512
ATT_KS = 512
ATT_SUB = ATT_KS // ATT_BK


def _suffix_consts():
    r = lax.broadcasted_iota(jnp.int32, (ATT_BK, 2 * ATT_BK), 0)
    c = lax.broadcasted_iota(jnp.int32, (ATT_BK, 2 * ATT_BK), 1)
    m = jnp.where((r > c) | (c >= ATT_BK), 1.0, 0.0).astype(BF16)
    return jnp.concatenate([m, m], axis=0)


def _suffix_sums(x, uu):
    ext = _split_dot(x, uu)
    return ext[:, :ATT_BK], ext[:, ATT_BK:]


def _masked_heads(x_f32, first):
    return (jnp.where(first, x_f32, 0.0).astype(BF16), jnp.where(first, 0.0, x_f32).astype(BF16))


def _unmasked_blocks(block, nfull, ks, T, carry):
    if 2 * ks > T:
        return lax.fori_loop(0, nfull, lambda i, c: block(nfull - 1 - i, ks, c, False), carry)
    carry = lax.fori_loop(0, jnp.bitwise_and(nfull, 1), lambda i, c: block(nfull - 1, ks, c, False), carry)
    pairs = nfull // 2
    return lax.fori_loop(0, pairs, lambda i, c: block(2 * (pairs - 1 - i), 2 * ks, c, False), carry)


def _sb_attn_fwd(qkv, name):
    T = qkv.shape[0]
    tq, ks = min(ATT_TQ, T), min(ATT_KS, T)
    assert T % tq == 0 and T % ks == 0 and ks % tq == 0
    nq = T // tq
    nsub = ks // ATT_BK
    nhp = N_HEADS // 2
    nt = (((1,), (1,)), ((), ()))


    def body(q_ref, k_ref, v_ref, o_ref):
        qi = pl.program_id(1)
        first = lax.broadcasted_iota(jnp.int32, (1, LANES), 1) < HEAD_DIM
        qs = _masked_heads(q_ref[...].astype(F32) * HEAD_DIM ** -0.5, first)
        uu = _suffix_consts()
        tpos = qi * tq + lax.broadcasted_iota(jnp.int32, (tq, ATT_BK), 0)
        col = lax.broadcasted_iota(jnp.int32, (tq, ATT_BK), 1)

        def block(kb, width, carry, masked):
            nsub = width // ATT_BK
            off = pl.multiple_of(kb * ks, ks)
            kblk = k_ref[pl.ds(off, width), :]
            vblk = v_ref[pl.ds(off, width), :]
            out = []
            for h in range(2):
                acc, run = carry[2 * h], carry[2 * h + 1]
                z = lax.dot_general(qs[h], kblk, nt, preferred_element_type=F32)
                lb = _log_sigmoid(z)
                l1m = lb - z
                ws = [None] * nsub
                for j in reversed(range(nsub)):
                    sl = slice(j * ATT_BK, (j + 1) * ATT_BK)
                    l1 = l1m[:, sl]
                    if masked:
                        causal = (off + j * ATT_BK + col) < tpos
                        l1 = jnp.where(causal, l1, 0.0)
                    cs, tot = _suffix_sums(l1, uu)
                    w = jnp.exp(lb[:, sl] + cs + run)
                    ws[j] = jnp.where(causal, w, 0.0) if masked else w
                    run = run + tot
                w = jnp.concatenate(ws, axis=1)
                acc = acc + jnp.dot(w.astype(BF16), vblk, preferred_element_type=F32)
                out += [acc, run]
            return tuple(out)

        nfull = (qi * tq) // ks
        carry = (jnp.zeros((tq, LANES), F32),) * 4
        carry = block(nfull, ks, carry, True)
        carry = _unmasked_blocks(block, nfull, ks, T, carry)
        o_ref[...] = jnp.where(first, carry[0], carry[2])

    return pl.pallas_call(
        body, name=name, grid=(nhp, nq),
        in_specs=[pl.BlockSpec((tq, LANES), lambda p, i: (i, p)),
                  pl.BlockSpec((T, LANES), lambda p, i: (0, nhp + p)),
                  pl.BlockSpec((T, LANES), lambda p, i: (0, 2 * nhp + p))],
        out_specs=pl.BlockSpec((tq, LANES), lambda p, i: (i, p)),
        out_shape=jax.ShapeDtypeStruct((T, D_MODEL), F32),
        compiler_params=_cp("parallel", "parallel"),
    )(qkv, qkv, qkv)


def _sb_attn_bwd(qkv, o, d_o, name):
    T = qkv.shape[0]
    tq, ks = min(ATT_TQ, T), min(ATT_KS, T)
    assert T % tq == 0 and T % ks == 0 and ks % tq == 0
    nq = T // tq
    nks = T // ks
    nsub = ks // ATT_BK
    nhp = N_HEADS // 2
    nt = (((1,), (1,)), ((), ()))

    def body(q_ref, k_ref, v_ref, o_ref, do_ref, dq_ref, dk_ref, dv_ref, dkt_acc, dvt_acc):
        qi = pl.program_id(1)

        @pl.when(qi == 0)
        def _():
            dkt_acc[...] = jnp.zeros_like(dkt_acc)
            dvt_acc[...] = jnp.zeros_like(dvt_acc)

        first = lax.broadcasted_iota(jnp.int32, (1, LANES), 1) < HEAD_DIM
        first_t = lax.broadcasted_iota(jnp.int32, (LANES, 1), 0) < HEAD_DIM
        qf = q_ref[...].astype(F32) * HEAD_DIM ** -0.5
        qs = _masked_heads(qf, first)
        qts = _masked_heads(qf.T, first_t)
        dof = do_ref[...].astype(BF16).astype(F32)
        dos = _masked_heads(dof, first)
        dots = _masked_heads(dof.T, first_t)
        prod = dof * o_ref[...]
        deltas = (jnp.broadcast_to(jnp.sum(jnp.where(first, prod, 0.0), axis=1, keepdims=True), (tq, LANES)),
                  jnp.broadcast_to(jnp.sum(jnp.where(first, 0.0, prod), axis=1, keepdims=True), (tq, LANES)))
        uu = _suffix_consts()
        tpos = qi * tq + lax.broadcasted_iota(jnp.int32, (tq, ATT_BK), 0)
        col = lax.broadcasted_iota(jnp.int32, (tq, ATT_BK), 1)

        def block(kb, width, carry, masked):
            nsub = width // ATT_BK
            off = pl.multiple_of(kb * ks, ks)
            kblk = k_ref[pl.ds(off, width), :]
            vblk = v_ref[pl.ds(off, width), :]
            out = []
            dkt = jnp.zeros((LANES, width), F32)
            dvt = jnp.zeros((LANES, width), F32)
            for h in range(2):
                dq, run, grun = carry[3 * h], carry[3 * h + 1], carry[3 * h + 2]
                z = lax.dot_general(qs[h], kblk, nt, preferred_element_type=F32)
                lb = _log_sigmoid(z)
                l1m = lb - z
                dw = lax.dot_general(dos[h], vblk, nt, preferred_element_type=F32)
                wbs, dzs = [None] * nsub, [None] * nsub
                for j in reversed(range(nsub)):
                    sl = slice(j * ATT_BK, (j + 1) * ATT_BK)
                    l1 = l1m[:, sl]
                    if masked:
                        causal = (off + j * ATT_BK + col) < tpos
                        l1 = jnp.where(causal, l1, 0.0)
                    cs, tot = _suffix_sums(l1, uu)
                    w = jnp.exp(lb[:, sl] + cs + run)
                    if masked:
                        w = jnp.where(causal, w, 0.0)
                    wbs[j] = w.astype(BF16)
                    g = wbs[j].astype(F32) * dw[:, sl]
                    gcs, gtot = _suffix_sums(g, uu)
                    dz = g - jnp.exp(lb[:, sl]) * (deltas[h] - grun - gcs)
                    if masked:
                        dz = jnp.where(causal, dz, 0.0)
                    dzs[j] = dz.astype(BF16)
                    run = run + tot
                    grun = grun + gtot
                wb = jnp.concatenate(wbs, axis=1)
                dzb = jnp.concatenate(dzs, axis=1)
                dq = dq + jnp.dot(dzb, kblk, preferred_element_type=F32)
                dkt = dkt + jnp.dot(qts[h], dzb, preferred_element_type=F32)
                dvt = dvt + jnp.dot(dots[h], wb, preferred_element_type=F32)
                out += [dq, run, grun]
            for part in range(width // ks):
                dkt_acc[kb + part] += dkt[:, part * ks:(part + 1) * ks]
                dvt_acc[kb + part] += dvt[:, part * ks:(part + 1) * ks]
            return tuple(out)

        nfull = (qi * tq) // ks
        carry = (jnp.zeros((tq, LANES), F32),) * 6
        carry = block(nfull, ks, carry, True)
        carry = _unmasked_blocks(block, nfull, ks, T, carry)
        dq_ref[...] = (jnp.where(first, carry[0], carry[3]) * HEAD_DIM ** -0.5).astype(dq_ref.dtype)

        @pl.when(qi == nq - 1)
        def _():
            for kb in range(nks):
                dk_ref[kb * ks:(kb + 1) * ks, :] = dkt_acc[kb].T.astype(dk_ref.dtype)
                dv_ref[kb * ks:(kb + 1) * ks, :] = dvt_acc[kb].T.astype(dv_ref.dtype)

    qspec = pl.BlockSpec((tq, LANES), lambda p, i: (i, p))
    full = pl.BlockSpec((T, LANES), lambda p, i: (0, p))
    return pl.pallas_call(
        body, name=name, grid=(nhp, nq),
        in_specs=[qspec, pl.BlockSpec((T, LANES), lambda p, i: (0, nhp + p)),
                  pl.BlockSpec((T, LANES), lambda p, i: (0, 2 * nhp + p)), qspec, qspec],
        out_specs=[qspec, full, full],
        out_shape=[jax.ShapeDtypeStruct((T, D_MODEL), BF16)] * 3,
        scratch_shapes=[pltpu.VMEM((nks, LANES, ks), F32), pltpu.VMEM((nks, LANES, ks), F32)],
        compiler_params=_cp("parallel", "arbitrary"),
    )(qkv, qkv, qkv, o, d_o)


def _rg_conv_fwd(x, conv_w, conv_b, name):
    T, C = x.shape
    bm = _row_block(T, 512)
    bc = C
    prev, _ = _halo_specs(T, bm, bc)

    def body(x_ref, h_ref, w_ref, b_ref, o_ref):
        halo = jnp.where(pl.program_id(1) > 0, h_ref[...], 0.0)
        o_ref[...] = _causal_conv(x_ref[...], halo, w_ref, b_ref, 4)

    return pl.pallas_call(
        body, name=name, grid=(C // bc, T // bm),
        in_specs=[pl.BlockSpec((bm, bc), lambda j, i: (i, j)), prev,
                  pl.BlockSpec((SUBLANES, bc), lambda j, i: (0, j)), pl.BlockSpec((1, bc), lambda j, i: (0, j))],
        out_specs=pl.BlockSpec((bm, bc), lambda j, i: (i, j)),
        out_shape=jax.ShapeDtypeStruct((T, C), F32),
        compiler_params=_cp("parallel", "parallel"),
    )(x, x, conv_w, conv_b)


def _rg_gate_math(ra, rx, rec, vec_ref):
    r = _sigmoid(ra + vec_ref[0:1, :])
    ig = _sigmoid(rx + vec_ref[1:2, :])
    lam = vec_ref[2:3, :]
    sp = jnp.maximum(-lam, 0.0) + jnp.log(1.0 + jnp.exp(-jnp.abs(lam)))
    log_a = -RG_C * r * sp
    a = jnp.exp(log_a)
    th = jnp.tanh(log_a)
    mult = jnp.sqrt(-2.0 * th / (1.0 - th))
    return r, ig, sp, a, mult


def _rg_gates_fwd(rax, rec, vecs, name):
    T, C = rec.shape
    bm = _row_block(T, 512)

    def body(ra_ref, rx_ref, rec_ref, vec_ref, a_ref, u_ref):
        recv = rec_ref[...]
        _, ig, _, a, mult = _rg_gate_math(ra_ref[...], rx_ref[...], recv, vec_ref)
        a_ref[...] = a
        u_ref[...] = mult * ig * recv

    blk = pl.BlockSpec((bm, C), lambda i: (i, 0))
    return pl.pallas_call(
        body, name=name, grid=(T // bm,),
        in_specs=[blk, pl.BlockSpec((bm, C), lambda i: (i, 1)), blk, pl.BlockSpec((SUBLANES, C), lambda i: (0, 0))],
        out_specs=[blk, blk],
        out_shape=[jax.ShapeDtypeStruct((T, C), F32)] * 2,
        compiler_params=_cp("parallel"),
    )(rax, rax, rec, vecs)


def _linear_scan(a, b, name, reverse=False, shift_coef=False):
    T, C = a.shape
    bm = _row_block(T, 256)
    bc = C
    nt = T // bm
    _, nxt = _halo_specs(T, bm, bc)
    groups = bm // SUBLANES

    def body(*refs):
        if shift_coef:
            a_ref, an_ref, b_ref, h_ref, carry = refs
        else:
            a_ref, b_ref, h_ref, carry = refs
        i = pl.program_id(1)

        @pl.when(i == 0)
        def _():
            carry[...] = jnp.zeros_like(carry)

        coef = a_ref[...]
        if shift_coef:
            coef = _shift_up(coef, an_ref[...], 1)
        val = b_ref[...]
        sub = jnp.bitwise_and(lax.broadcasted_iota(jnp.int32, (bm, bc), 0), SUBLANES - 1)
        for s in (1, 2, 4):
            if reverse:
                keep = sub < SUBLANES - s
                cs, vs = pltpu.roll(coef, bm - s, 0), pltpu.roll(val, bm - s, 0)
            else:
                keep = sub >= s
                cs, vs = pltpu.roll(coef, s, 0), pltpu.roll(val, s, 0)
            val = jnp.where(keep, coef * vs + val, val)
            coef = jnp.where(keep, coef * cs, coef)
        edge = carry[0:1, :]
        pieces = [None] * groups
        for g in (reversed(range(groups)) if reverse else range(groups)):
            rows = slice(g * SUBLANES, (g + 1) * SUBLANES)
            hg = val[rows] + coef[rows] * edge
            pieces[g] = hg
            edge = hg[0:1, :] if reverse else hg[SUBLANES - 1:SUBLANES, :]
        h_ref[...] = jnp.concatenate(pieces, axis=0)
        carry[...] = jnp.broadcast_to(edge, carry.shape)

    if reverse:
        blk = pl.BlockSpec((bm, bc), lambda j, i: (nt - 1 - i, j))
        r = bm // SUBLANES
        last = T // SUBLANES - 1
        nxt = pl.BlockSpec((SUBLANES, bc), lambda j, i: (jnp.minimum((nt - i) * r, last), j))
    else:
        blk = pl.BlockSpec((bm, bc), lambda j, i: (i, j))
    in_specs = [blk] + ([nxt] if shift_coef else []) + [blk]
    args = (a,) + ((a,) if shift_coef else ()) + (b,)
    return pl.pallas_call(
        body, name=name, grid=(C // bc, nt),
        in_specs=in_specs, out_specs=blk,
        out_shape=jax.ShapeDtypeStruct((T, C), F32),
        scratch_shapes=[pltpu.VMEM((SUBLANES, bc), F32)],
        compiler_params=_cp("parallel", "arbitrary"),
    )(*args)


def _rg_out_fwd(gate_branch, h, name):
    T, C = h.shape
    bm = _row_block(T, 512)

    def body(g_ref, h_ref, o_ref):
        o_ref[...] = (_gelu(g_ref[...]) * h_ref[...]).astype(o_ref.dtype)

    blk = pl.BlockSpec((bm, C), lambda i: (i, 0))
    return pl.pallas_call(
        body, name=name, grid=(T // bm,), in_specs=[blk, blk], out_specs=blk,
        out_shape=jax.ShapeDtypeStruct((T, C), BF16), compiler_params=_cp("parallel"),
    )(gate_branch, h)


def _rg_out_bwd(d_gh, gate_branch, h, name):
    T, C = h.shape
    bm = _row_block(T, 512)

    def body(d_ref, g_ref, h_ref, dg_ref, dh_ref):
        gv, gg = _gelu_and_grad(g_ref[...])
        d = d_ref[...]
        dg_ref[...] = (d * h_ref[...] * gg).astype(dg_ref.dtype)
        dh_ref[...] = d * gv

    blk = pl.BlockSpec((bm, C), lambda i: (i, 0))
    return pl.pallas_call(
        body, name=name, grid=(T // bm,), in_specs=[blk, blk, blk], out_specs=[blk, blk],
        out_shape=[jax.ShapeDtypeStruct((T, C), BF16), jax.ShapeDtypeStruct((T, C), F32)],
        compiler_params=_cp("parallel"),
    )(d_gh, gate_branch, h)


def _rg_gates_bwd(dhs, h, rax, rec, vecs, name):
    T, C = rec.shape
    bm = _row_block(T, 1024)
    bc = LANES
    prev, _ = _halo_specs(T, bm, bc)
    nc = C // bc

    def body(d_ref, h_ref, hh_ref, ra_ref, rx_ref, rec_ref, vec_ref, dra_ref, drx_ref, drec_ref, st_ref):
        i = pl.program_id(1)
        halo = jnp.where(i > 0, hh_ref[...], 0.0)
        recv = rec_ref[...]
        du = d_ref[...]
        r, ig, sp, a, mult = _rg_gate_math(ra_ref[...], rx_ref[...], recv, vec_ref)
        da = du * _shift_down(h_ref[...], halo, 1)
        d_i = du * mult * recv
        drec_ref[...] = du * mult * ig
        d_mult = du * ig * recv
        d_log_a = da * a - d_mult * (a * a) / mult
        d_ra = d_log_a * (-RG_C) * sp * r * (1.0 - r)
        d_rx = d_i * ig * (1.0 - ig)
        dra_ref[...] = d_ra.astype(dra_ref.dtype)
        drx_ref[...] = d_rx.astype(drx_ref.dtype)
        lam = vec_ref[2:3, :]
        d_sp = jnp.sum(d_log_a * (-RG_C) * r, axis=0, keepdims=True)
        rows = [jnp.sum(d_ra, axis=0, keepdims=True), jnp.sum(d_rx, axis=0, keepdims=True),
                d_sp * (_sigmoid(lam) - 1.0), jnp.zeros((5, bc), F32)]

        @pl.when(i == 0)
        def _():
            st_ref[...] = jnp.zeros_like(st_ref)

        st_ref[...] += jnp.concatenate(rows, axis=0)

    blk = pl.BlockSpec((bm, bc), lambda j, i: (i, j))
    blk_x = pl.BlockSpec((bm, bc), lambda j, i: (i, nc + j))
    vec = pl.BlockSpec((SUBLANES, bc), lambda j, i: (0, j))
    d_ra, d_rx, d_rec, stats = pl.pallas_call(
        body, name=name, grid=(nc, T // bm),
        in_specs=[blk, blk, prev, blk, blk_x, blk, vec],
        out_specs=[blk, blk, blk, vec],
        out_shape=[jax.ShapeDtypeStruct((T, C), BF16), jax.ShapeDtypeStruct((T, C), BF16),
                   jax.ShapeDtypeStruct((T, C), F32), jax.ShapeDtypeStruct((SUBLANES, C), F32)],
        compiler_params=_cp("parallel", "arbitrary"),
    )(dhs, h, h, rax, rax, rec, vecs)
    return d_ra, d_rx, d_rec, stats


def _rg_conv_bwd_stats(d_rec, x, name):
    T, C = x.shape
    bm = _row_block(T, 1024)
    bc = LANES
    prev, _ = _halo_specs(T, bm, bc)

    def body(d_ref, x_ref, h_ref, st_ref):
        i = pl.program_id(1)
        halo = jnp.where(i > 0, h_ref[...], 0.0)
        d = d_ref[...]
        xv = x_ref[...]
        rows = [jnp.sum(d * _shift_down(xv, halo, 3 - k), axis=0, keepdims=True) for k in range(3)]
        rows += [jnp.sum(d * xv, axis=0, keepdims=True), jnp.sum(d, axis=0, keepdims=True), jnp.zeros((3, bc), F32)]

        @pl.when(i == 0)
        def _():
            st_ref[...] = jnp.zeros_like(st_ref)

        st_ref[...] += jnp.concatenate(rows, axis=0)

    blk = pl.BlockSpec((bm, bc), lambda j, i: (i, j))
    vec = pl.BlockSpec((SUBLANES, bc), lambda j, i: (0, j))
    return pl.pallas_call(
        body, name=name, grid=(C // bc, T // bm),
        in_specs=[blk, blk, prev], out_specs=vec,
        out_shape=jax.ShapeDtypeStruct((SUBLANES, C), F32),
        compiler_params=_cp("parallel", "arbitrary"),
    )(d_rec, x, x)


def _adamw(w, g, m, v, name):
    R, W = w.shape
    bm = R
    for cand in (512, 256, 128, 64, 32, 16, 8):
        if R % cand == 0:
            bm = cand
            break
    bc1 = 1.0 - ADAM_B1 ** ADAM_STEP
    bc2 = 1.0 - ADAM_B2 ** ADAM_STEP

    def body(w_ref, g_ref, m_ref, v_ref, d_ref, nm_ref, nv_ref):
        gv = g_ref[...]
        nm = ADAM_B1 * m_ref[...] + (1.0 - ADAM_B1) * gv
        nv = ADAM_B2 * v_ref[...] + (1.0 - ADAM_B2) * (gv * gv)
        m_hat = nm / bc1
        v_hat = nv / bc2
        d_ref[...] = -ADAM_LR * (m_hat / (jnp.sqrt(v_hat) + ADAM_EPS) + ADAM_WD * w_ref[...])
        nm_ref[...] = nm
        nv_ref[...] = nv

    blk = pl.BlockSpec((bm, W), lambda i: (i, 0))
    return pl.pallas_call(
        body, name=name, grid=(R // bm,), in_specs=[blk] * 4, out_specs=[blk] * 3,
        out_shape=[jax.ShapeDtypeStruct((R, W), F32)] * 3, compiler_params=_cp("parallel"),
    )(w, g, m, v)


_ANY = pl.BlockSpec(memory_space=pl.ANY)


def _me():
    return lax.axis_index("x"), lax.axis_index("y"), lax.axis_index("c")


def _other_chips(x, y):
    return [(1 - x, y), (x, 1 - y), (1 - x, 1 - y)]


def _remote(src, dst, send_sem, recv_sem, to):
    return pltpu.make_async_remote_copy(src_ref=src, dst_ref=dst, send_sem=send_sem, recv_sem=recv_sem,
                                        device_id=to, device_id_type=MESH)


def _half_rows(c, rows_half, align):
    return pl.ds(pl.multiple_of(c * rows_half, align), rows_half)


def _divisor_block(rows, cap=512):
    if rows <= cap:
        return rows
    best = None
    for cand in range(16, cap + 1, 16):
        if rows % cand == 0:
            best = cand
    assert best is not None, rows
    return best


def _all_gather_chips(shards, name):
    n = len(shards)
    halves = [s.shape[0] // 2 for s in shards]
    for s, rh in zip(shards, halves):
        assert 2 * rh == s.shape[0] and rh % 16 == 0, s.shape

    def body(*refs):
        x_refs, out_refs = refs[:n], refs[n:2 * n]
        send_sems, recv_sems = refs[2 * n:]
        x, y, c = _me()
        s_me = 2 * x + y
        chips = _other_chips(x, y)
        sends = []
        for a in range(n):
            half = _half_rows(c, halves[a], 16)
            for j, (cx, cy) in enumerate(chips):
                cp = _remote(x_refs[a].at[half], out_refs[a].at[s_me, half], send_sems.at[6 * a + j],
                             recv_sems.at[6 * a + j], (cx, cy, c))
                cp.start()
                sends.append(cp)
        for a in range(n):
            half = _half_rows(c, halves[a], 16)
            for j, (cx, cy) in enumerate(chips):
                landed = out_refs[a].at[2 * cx + cy, half]
                _remote(landed, landed, send_sems.at[6 * a + j], recv_sems.at[6 * a + j], (cx, cy, c)).wait_recv()
                fwd = _remote(landed, landed, send_sems.at[6 * a + 3 + j], recv_sems.at[6 * a + 3 + j], (x, y, 1 - c))
                fwd.start()
                sends.append(fwd)
        for a in range(n):
            other_half = _half_rows(1 - c, halves[a], 16)
            for j, (cx, cy) in enumerate(chips):
                landed = out_refs[a].at[2 * cx + cy, other_half]
                _remote(landed, landed, send_sems.at[6 * a + 3 + j], recv_sems.at[6 * a + 3 + j], (x, y, 1 - c)).wait_recv()
        for cp in sends:
            cp.wait_send()

    outs = pl.pallas_call(
        body, name=name, in_specs=[_ANY] * n, out_specs=[_ANY] * n,
        out_shape=[jax.ShapeDtypeStruct((N_CHIPS,) + s.shape, s.dtype) for s in shards],
        scratch_shapes=[pltpu.SemaphoreType.DMA((6 * n,)), pltpu.SemaphoreType.DMA((6 * n,))],
    )(*shards)
    s_me = 2 * lax.axis_index("x") + lax.axis_index("y")
    return [lax.dynamic_update_slice(o, s[None], (s_me, 0, 0)) for o, s in zip(outs, shards)]


def _pair_send_other_halves(gs, name):
    n = len(gs)
    halves = [g.shape[1] // 2 for g in gs]
    for g, rh in zip(gs, halves):
        assert 2 * rh == g.shape[1] and rh % 8 == 0, g.shape

    def body(*refs):
        g_refs, land_refs = refs[:n], refs[n:2 * n]
        send_sems, recv_sems = refs[2 * n:]
        x, y, c = _me()
        copies = []
        for a in range(n):
            src = g_refs[a].at[:, _half_rows(1 - c, halves[a], 8), :]
            cp = _remote(src, land_refs[a], send_sems.at[a], recv_sems.at[a], (x, y, 1 - c))
            cp.start()
            copies.append(cp)
        for cp in copies:
            cp.wait()

    return pl.pallas_call(
        body, name=name, in_specs=[_ANY] * n, out_specs=[_ANY] * n,
        out_shape=[jax.ShapeDtypeStruct((g.shape[0], rh, g.shape[2]), g.dtype) for g, rh in zip(gs, halves)],
        scratch_shapes=[pltpu.SemaphoreType.DMA((n,)), pltpu.SemaphoreType.DMA((n,))],
    )(*gs)


def _add_own_half(g, landed, name, out_dtype):
    n, R, W = g.shape
    Rh = R // 2
    bm = _divisor_block(Rh)
    nb = Rh // bm
    c_arr = lax.axis_index("c").astype(jnp.int32).reshape((1,))

    def body(c_ref, g_ref, l_ref, o_ref):
        o_ref[...] = (g_ref[...] + l_ref[...]).astype(o_ref.dtype)

    grid_spec = pltpu.PrefetchScalarGridSpec(
        num_scalar_prefetch=1, grid=(n, nb),
        in_specs=[pl.BlockSpec((1, bm, W), lambda s, i, c_ref: (s, c_ref[0] * nb + i, 0)),
                  pl.BlockSpec((1, bm, W), lambda s, i, c_ref: (s, i, 0))],
        out_specs=pl.BlockSpec((1, bm, W), lambda s, i, c_ref: (s, i, 0)))
    return pl.pallas_call(
        body, name=name, grid_spec=grid_spec, out_shape=jax.ShapeDtypeStruct((n, Rh, W), out_dtype),
        compiler_params=_cp("parallel", "parallel"),
    )(c_arr, g, landed)


def _scatter_to_chips(ps, name):
    n = len(ps)

    def body(*refs):
        p_refs, land_refs = refs[:n], refs[n:2 * n]
        send_sems, recv_sems = refs[2 * n:]
        x, y, c = _me()
        s_me = 2 * x + y
        chips = _other_chips(x, y)
        sends = []
        for a in range(n):
            for j, (cx, cy) in enumerate(chips):
                cp = _remote(p_refs[a].at[2 * cx + cy], land_refs[a].at[s_me], send_sems.at[3 * a + j],
                             recv_sems.at[3 * a + j], (cx, cy, c))
                cp.start()
                sends.append(cp)
        for a in range(n):
            for j, (cx, cy) in enumerate(chips):
                slot = land_refs[a].at[2 * cx + cy]
                _remote(slot, slot, send_sems.at[3 * a + j], recv_sems.at[3 * a + j], (cx, cy, c)).wait_recv()
        for cp in sends:
            cp.wait_send()

    lands = pl.pallas_call(
        body, name=name, in_specs=[_ANY] * n, out_specs=[_ANY] * n,
        out_shape=[jax.ShapeDtypeStruct(p.shape, p.dtype) for p in ps],
        scratch_shapes=[pltpu.SemaphoreType.DMA((3 * n,)), pltpu.SemaphoreType.DMA((3 * n,))],
    )(*ps)
    s_me = 2 * lax.axis_index("x") + lax.axis_index("y")
    return [lax.dynamic_update_slice(l, lax.dynamic_slice(p, (s_me, 0, 0), (1,) + p.shape[1:]), (s_me, 0, 0))
            for l, p in zip(lands, ps)]


def _sum_slots(l, name):
    n, Rh, W = l.shape
    bm = _divisor_block(Rh, 256)
    c_arr = lax.axis_index("c").astype(jnp.int32).reshape((1,))

    def body(c_ref, l_ref, o_ref):
        o_ref[0] = ((l_ref[0].astype(F32) + l_ref[1].astype(F32)) + l_ref[2].astype(F32)) + l_ref[3].astype(F32)

    grid_spec = pltpu.PrefetchScalarGridSpec(
        num_scalar_prefetch=1, grid=(Rh // bm,),
        in_specs=[pl.BlockSpec((n, bm, W), lambda i, c_ref: (0, i, 0))],
        out_specs=pl.BlockSpec((1, bm, W), lambda i, c_ref: (c_ref[0], i, 0)))
    return pl.pallas_call(
        body, name=name, grid_spec=grid_spec,
        out_shape=jax.ShapeDtypeStruct((2, Rh, W), F32), compiler_params=_cp("parallel"),
    )(c_arr, l)


def _pair_gather(fs, name):
    n = len(fs)

    def body(*refs):
        f_refs, out_refs = refs[:n], refs[n:2 * n]
        send_sems, recv_sems = refs[2 * n:]
        x, y, c = _me()
        sends = []
        for a in range(n):
            cp = _remote(f_refs[a].at[c], out_refs[a].at[c], send_sems.at[a], recv_sems.at[a], (x, y, 1 - c))
            cp.start()
            sends.append(cp)
        for a in range(n):
            sib = out_refs[a].at[1 - c]
            _remote(sib, sib, send_sems.at[a], recv_sems.at[a], (x, y, 1 - c)).wait_recv()
        for cp in sends:
            cp.wait_send()

    return pl.pallas_call(
        body, name=name, in_specs=[_ANY] * n, out_specs=[_ANY] * n,
        out_shape=[jax.ShapeDtypeStruct(f.shape, f.dtype) for f in fs],
        input_output_aliases={a: a for a in range(n)},
        scratch_shapes=[pltpu.SemaphoreType.DMA((n,)), pltpu.SemaphoreType.DMA((n,))],
    )(*fs)


def _pack(arrays, rows_multiple, dtype):
    flat = jnp.concatenate([a.reshape(-1).astype(dtype) for a in arrays])
    n = flat.shape[0]
    rows = -(-n // FLAT_W)
    rows = -(-rows // rows_multiple) * rows_multiple
    return jnp.pad(flat, (0, rows * FLAT_W - n)).reshape(rows, FLAT_W)


def _unpack(flat, shapes):
    flat = flat.reshape(-1)
    out, off = [], 0
    for shp in shapes:
        n = math.prod(shp)
        out.append(flat[off:off + n].reshape(shp))
        off += n
    return out


def _pad_cols(a, width):
    return jnp.pad(a, [(0, 0)] * (a.ndim - 1) + [(0, width - a.shape[-1])])


def _rows8(a):
    return jnp.pad(a, ((0, SUBLANES - a.shape[0]), (0, 0)))


def _block_diag_dense(w):
    eye = jnp.eye(RG_BLOCKS, dtype=w.dtype)
    dense = jnp.einsum("ncd,nm->ncmd", w, eye).reshape(D_RNN, D_RNN)
    return jnp.pad(dense, ((0, D_RNN_PAD - D_RNN), (0, D_RNN_PAD - D_RNN)))


def _block_diag_extract(dense):
    d = dense[:D_RNN, :D_RNN].reshape(RG_BLOCKS, RG_BLOCK_W, RG_BLOCKS, RG_BLOCK_W)
    return jnp.stack([d[n, :, n, :] for n in range(RG_BLOCKS)])


def _ffn_fwd(x, l, P, tag):
    n = _rmsnorm_fwd(x, P["ffn_pre_g"][l], f"{tag}_prenorm")
    u_pre = _mm(n, P["ffn_w_up"][l], "nn", F32, f"{tag}_up", bn=2 * D_FF // 4)
    act = _ffn_act_fwd(u_pre, P["ffn_conv_w"][l], P["ffn_conv_b"][l], f"{tag}_act")
    f = _mm(act, P["ffn_w_down"][l], "nn", F32, f"{tag}_down", bk=D_FF)
    return f, (n, u_pre, act)


def _ffn_bwd(df, x, saved, l, P, tag):
    n, u_pre, act = saved
    g_down = _mm(act, df, "tn", F32, f"{tag}_dwdown", bm=D_FF // 2, bk=1024)
    d_act = _mm(df, P["ffn_w_down"][l], "nt", F32, f"{tag}_dact", bn=D_FF // 2)
    (du_g, du_v), stats = _ffn_act_bwd(u_pre, P["ffn_conv_w"][l], P["ffn_conv_b"][l], d_act, f"{tag}_actbwd")
    g_up_g = _mm(n, du_g, "tn", F32, f"{tag}_dwup_gate", bn=D_FF // 2, bk=1024)
    g_up_v = _mm(n, du_v, "tn", F32, f"{tag}_dwup_value", bn=D_FF // 2, bk=1024)
    dn = _mm(du_g, P["ffn_w_up_gate"][l], "nt", F32, f"{tag}_dn_gate", bk=D_FF // 2)
    dn = _mm(du_v, P["ffn_w_up_value"][l], "nt", F32, f"{tag}_dn_value", bk=D_FF // 2, add=dn)
    return dn, dict(ffn_w_down=g_down, ffn_w_up_gate=g_up_g, ffn_w_up_value=g_up_v, ffn_conv_w=stats[0:3], ffn_conv_b=stats[3])


def _local_step(x, target, P):
    G = {}
    n0 = _rmsnorm_fwd(x, P["mix_pre_g"][0], "l0_prenorm")
    qkv = _mm(n0, P["attn_w_qkv"], "nn", BF16, "l0_qkv")
    o = _sb_attn_fwd(qkv, "l0_attn")
    a0 = _mm(o, P["attn_w_o"], "nn", F32, "l0_wo")
    x1 = _postnorm_residual(x, a0, P["mix_post_g"][0], "l0_postnorm")
    f0, ffn0 = _ffn_fwd(x1, 0, P, "f0")
    x2 = _postnorm_residual(x1, f0, P["ffn_post_g"][0], "f0_postnorm")
    n1 = _rmsnorm_fwd(x2, P["mix_pre_g"][1], "l1_prenorm")
    gate_b = _mm(n1, P["rg_w_in_gate"], "nn", F32, "l1_in_gate", bn=D_RNN_PAD)
    rec_b = _mm(n1, P["rg_w_in_rec"], "nn", F32, "l1_in_rec", bn=D_RNN_PAD)
    rec = _rg_conv_fwd(rec_b, P["rg_conv_w"], P["rg_conv_b"], "l1_conv")
    rax = _mm(rec, P["rg_w_ax"], "nn", F32, "l1_gates_mm", bn=D_RNN_PAD, bk=D_RNN_PAD)
    a, u = _rg_gates_fwd(rax, rec, P["rg_vecs"], "l1_gates")
    h = _linear_scan(a, u, "l1_scan")
    gh = _rg_out_fwd(gate_b, h, "l1_gate_out")
    y1 = _mm(gh, P["rg_w_out"], "nn", F32, "l1_wout", bk=D_RNN_PAD)
    x3 = _postnorm_residual(x2, y1, P["mix_post_g"][1], "l1_postnorm")
    f1, ffn1 = _ffn_fwd(x3, 1, P, "f1")
    dx, loss = _postnorm_loss(x3, f1, P["ffn_post_g"][1], target, "loss")

    df, G["ffn_post_g1"] = _rmsnorm_bwd(dx, f1, P["ffn_post_g"][1], "f1_postnorm_bwd", out_dtype=BF16)
    dn, g = _ffn_bwd(df, x3, ffn1, 1, P, "f1")
    G.update({k + "1": v for k, v in g.items()})
    dx, G["ffn_pre_g1"] = _rmsnorm_bwd(dn, x3, P["ffn_pre_g"][1], "f1_prenorm_bwd", res=dx)
    dy, G["mix_post_g1"] = _rmsnorm_bwd(dx, y1, P["mix_post_g"][1], "l1_postnorm_bwd", out_dtype=BF16)
    G["rg_w_out"] = _mm(gh, dy, "tn", F32, "l1_dwout", bm=D_RNN_PAD, bk=1024)
    d_gh = _mm(dy, P["rg_w_out"], "nt", F32, "l1_dgh", bn=D_RNN_PAD)
    d_gate, d_h = _rg_out_bwd(d_gh, gate_b, h, "l1_gate_out_bwd")
    dhs = _linear_scan(a, d_h, "l1_scan_bwd", reverse=True, shift_coef=True)
    d_ra, d_rx, d_rec_direct, st = _rg_gates_bwd(dhs, h, rax, rec, P["rg_vecs"], "l1_gates_bwd")
    G["rg_b_a"], G["rg_b_x"], G["rg_lambda"] = st[0], st[1], st[2]
    d_rax = jnp.concatenate([d_ra, d_rx], axis=1)
    G["rg_w_ax"] = _mm(rec, d_rax, "tn", F32, "l1_dwax", bm=D_RNN_PAD, bn=D_RNN_PAD, bk=1024)
    d_rec = _mm(d_rax, P["rg_w_ax"], "nt", F32, "l1_drec", bm=512, bn=D_RNN_PAD, bk=2 * D_RNN_PAD, add=d_rec_direct)
    st = _rg_conv_bwd_stats(d_rec, rec_b, "l1_conv_stats")
    G["rg_conv_w"], G["rg_conv_b"] = st[0:4], st[4]
    d_rec_b = _conv_transpose(d_rec, P["rg_conv_w"], 4, "l1_convT", D_RNN_PAD)
    G["rg_w_in_gate"] = _mm(n1, d_gate, "tn", F32, "l1_dwin_gate", bn=D_RNN_PAD, bk=1024)
    G["rg_w_in_rec"] = _mm(n1, d_rec_b, "tn", F32, "l1_dwin_rec", bn=D_RNN_PAD, bk=1024)
    dn = _mm(d_gate, P["rg_w_in_gate"], "nt", F32, "l1_dn_gate", bk=D_RNN_PAD)
    dn = _mm(d_rec_b, P["rg_w_in_rec"], "nt", F32, "l1_dn_rec", bk=D_RNN_PAD, add=dn)
    dx, G["mix_pre_g1"] = _rmsnorm_bwd(dn, x2, P["mix_pre_g"][1], "l1_prenorm_bwd", res=dx)
    df, G["ffn_post_g0"] = _rmsnorm_bwd(dx, f0, P["ffn_post_g"][0], "f0_postnorm_bwd", out_dtype=BF16)
    dn, g = _ffn_bwd(df, x1, ffn0, 0, P, "f0")
    G.update({k + "0": v for k, v in g.items()})
    dx, G["ffn_pre_g0"] = _rmsnorm_bwd(dn, x1, P["ffn_pre_g"][0], "f0_prenorm_bwd", res=dx)
    da, G["mix_post_g0"] = _rmsnorm_bwd(dx, a0, P["mix_post_g"][0], "l0_postnorm_bwd", out_dtype=BF16)
    G["attn_w_o"] = _mm(o, da, "tn", F32, "l0_dwo", bk=1024)
    d_o = _mm(da, P["attn_w_o"], "nt", F32, "l0_do")
    dq, dk, dv = _sb_attn_bwd(qkv, o, d_o, "l0_attn_bwd")
    d_qkv = jnp.concatenate([dq, dk, dv], axis=1)
    G["attn_w_qkv"] = _mm(n0, d_qkv, "tn", F32, "l0_dwqkv", bk=1024)
    dn = _mm(d_qkv, P["attn_w_qkv"], "nt", F32, "l0_dn")
    dx, G["mix_pre_g0"] = _rmsnorm_bwd(dn, x, P["mix_pre_g"][0], "l0_prenorm_bwd", res=dx)
    return loss[0, 0], dx, G


_BIG = [("attn_w_qkv", 2), ("attn_w_o", 1), ("rg_w_in", 2), ("rg_w_out", 1), ("ffn_w_up", 2), ("ffn_w_down", 1)]
_SMALL_SHARDED = [("rg_conv_w", 2), ("rg_conv_b", 1), ("rg_b_a", 1), ("rg_b_x", 1), ("rg_lambda", 1), ("ffn_conv_w", 2)]
_REPLICATED = ["rg_w_a", "rg_w_x", "ffn_conv_b", "mix_pre_g", "mix_post_g", "ffn_pre_g", "ffn_post_g"]
_ORDER = ["attn_w_qkv", "attn_w_o", "rg_w_in", "rg_conv_w", "rg_conv_b", "rg_w_a", "rg_b_a", "rg_w_x", "rg_b_x",
          "rg_lambda", "rg_w_out", "ffn_w_up", "ffn_conv_w", "ffn_conv_b", "ffn_w_down", "mix_pre_g", "mix_post_g",
          "ffn_pre_g", "ffn_post_g"]


def _join(gathered, shapes, axes):
    per_chip = [_unpack(gathered[s], shapes) for s in range(N_CHIPS)]
    return [jnp.concatenate([per_chip[s][i] for s in range(N_CHIPS)], axis=ax) for i, ax in enumerate(axes)]


def _split(full, axis):
    return jnp.split(full, N_CHIPS, axis=axis)


RG_OUT_SHARD_ROWS = D_RNN // N_CHIPS
RG_OUT_SHARD_ROWS_PAD = 352
D_FF_SHARD = D_FF // N_CHIPS


def _two_d(a):
    return a.reshape(-1, a.shape[-1])


def _gather_params(W):
    big = {n: _two_d(W[n]).astype(BF16) for n, _ in _BIG}
    big["rg_w_out"] = jnp.pad(big["rg_w_out"], ((0, RG_OUT_SHARD_ROWS_PAD - RG_OUT_SHARD_ROWS), (0, 0)))
    small_shapes = [W[n].shape for n, _ in _SMALL_SHARDED]
    small_flat = _pack([W[n] for n, _ in _SMALL_SHARDED], 32, F32)
    got = _all_gather_chips([big[n] for n, _ in _BIG] + [small_flat], "gather_weights")
    g = dict(zip([n for n, _ in _BIG], got[:-1]))
    full = {}
    full["attn_w_qkv"] = jnp.swapaxes(g["attn_w_qkv"], 0, 1).reshape(1, D_MODEL, 3 * D_MODEL)
    full["attn_w_o"] = g["attn_w_o"].reshape(1, D_MODEL, D_MODEL)
    full["rg_w_in"] = jnp.swapaxes(g["rg_w_in"], 0, 1).reshape(1, D_MODEL, 2 * D_RNN)
    full["rg_w_out"] = g["rg_w_out"][:, :RG_OUT_SHARD_ROWS].reshape(1, D_RNN, D_MODEL)
    up = g["ffn_w_up"].reshape(N_CHIPS, 2, D_MODEL, 2 * D_FF // N_CHIPS)
    full["ffn_w_up"] = jnp.transpose(up, (1, 2, 0, 3)).reshape(2, D_MODEL, 2 * D_FF)
    down = g["ffn_w_down"].reshape(N_CHIPS, 2, D_FF_SHARD, D_MODEL)
    full["ffn_w_down"] = jnp.swapaxes(down, 0, 1).reshape(2, D_FF, D_MODEL)
    full.update(zip([n for n, _ in _SMALL_SHARDED], _join(got[-1], small_shapes, [ax for _, ax in _SMALL_SHARDED])))
    for n in _REPLICATED:
        full[n] = W[n]
    return full


def _layout_params(full):
    P = {}
    P["attn_w_qkv"] = full["attn_w_qkv"][0]
    P["attn_w_o"] = full["attn_w_o"][0]
    w_in = full["rg_w_in"][0]
    P["rg_w_in_gate"] = _pad_cols(w_in[:, :D_RNN], D_RNN_PAD)
    P["rg_w_in_rec"] = _pad_cols(w_in[:, D_RNN:], D_RNN_PAD)
    P["rg_w_out"] = jnp.pad(full["rg_w_out"][0], ((0, D_RNN_PAD - D_RNN), (0, 0)))
    P["rg_conv_w"] = _rows8(_pad_cols(full["rg_conv_w"][0], D_RNN_PAD))
    P["rg_conv_b"] = _pad_cols(full["rg_conv_b"], D_RNN_PAD)
    P["rg_vecs"] = _rows8(_pad_cols(jnp.concatenate([full["rg_b_a"], full["rg_b_x"], full["rg_lambda"]], axis=0), D_RNN_PAD))
    P["rg_w_ax"] = jnp.concatenate([_block_diag_dense(full["rg_w_a"][0]), _block_diag_dense(full["rg_w_x"][0])], axis=1).astype(BF16)
    P["ffn_w_up"] = [full["ffn_w_up"][l] for l in range(2)]
    P["ffn_w_up_gate"] = [full["ffn_w_up"][l][:, :D_FF] for l in range(2)]
    P["ffn_w_up_value"] = [full["ffn_w_up"][l][:, D_FF:] for l in range(2)]
    P["ffn_conv_w"] = [_rows8(full["ffn_conv_w"][l]) for l in range(2)]
    P["ffn_conv_b"] = [full["ffn_conv_b"][l:l + 1] for l in range(2)]
    P["ffn_w_down"] = [full["ffn_w_down"][l] for l in range(2)]
    for n in ("mix_pre_g", "mix_post_g", "ffn_pre_g", "ffn_post_g"):
        P[n] = [full[n][l:l + 1] for l in range(2)]
    return P


def _assemble_grads(G):
    out = {}
    out["attn_w_qkv"] = G["attn_w_qkv"][None]
    out["attn_w_o"] = G["attn_w_o"][None]
    out["rg_w_in"] = jnp.concatenate([G["rg_w_in_gate"][:, :D_RNN], G["rg_w_in_rec"][:, :D_RNN]], axis=1)[None]
    out["rg_conv_w"] = G["rg_conv_w"][:, :D_RNN][None]
    out["rg_conv_b"] = G["rg_conv_b"][:D_RNN][None]
    out["rg_w_a"] = _block_diag_extract(G["rg_w_ax"][:, :D_RNN_PAD])[None]
    out["rg_w_x"] = _block_diag_extract(G["rg_w_ax"][:, D_RNN_PAD:])[None]
    out["rg_b_a"] = G["rg_b_a"][:D_RNN][None]
    out["rg_b_x"] = G["rg_b_x"][:D_RNN][None]
    out["rg_lambda"] = G["rg_lambda"][:D_RNN][None]
    out["rg_w_out"] = G["rg_w_out"][:D_RNN][None]
    out["ffn_w_up"] = jnp.stack([jnp.concatenate([G[f"ffn_w_up_gate{l}"], G[f"ffn_w_up_value{l}"]], axis=1) for l in range(2)])
    out["ffn_conv_w"] = jnp.stack([G[f"ffn_conv_w{l}"] for l in range(2)])
    out["ffn_conv_b"] = jnp.stack([G[f"ffn_conv_b{l}"] for l in range(2)])
    out["ffn_w_down"] = jnp.stack([G[f"ffn_w_down{l}"] for l in range(2)])
    for n in ("mix_pre_g", "mix_post_g", "ffn_pre_g", "ffn_post_g"):
        out[n] = jnp.concatenate([G[n + "0"], G[n + "1"]], axis=0)
    return out


def _big_grad_stacks(G):
    st = {}
    st["attn_w_qkv"] = jnp.swapaxes(G["attn_w_qkv"].reshape(D_MODEL, N_CHIPS, -1), 0, 1)
    st["attn_w_o"] = G["attn_w_o"].reshape(N_CHIPS, -1, D_MODEL)
    w_in = jnp.concatenate([G["rg_w_in_gate"][:, :D_RNN], G["rg_w_in_rec"][:, :D_RNN]], axis=1)
    st["rg_w_in"] = jnp.swapaxes(w_in.reshape(D_MODEL, N_CHIPS, -1), 0, 1)
    st["rg_w_out"] = G["rg_w_out"][:D_RNN].reshape(N_CHIPS, RG_OUT_SHARD_ROWS, D_MODEL)
    half = D_FF // 2
    pieces = [[G[f"ffn_w_up_{part}{l}"][:, h * half:(h + 1) * half] for l in range(2)]
              for part in ("gate", "value") for h in range(2)]
    st["ffn_w_up"] = jnp.stack([jnp.concatenate(p, axis=0) for p in pieces])
    down = jnp.stack([G[f"ffn_w_down{l}"] for l in range(2)]).reshape(2, N_CHIPS, D_FF_SHARD, D_MODEL)
    st["ffn_w_down"] = jnp.swapaxes(down, 0, 1).reshape(N_CHIPS, 2 * D_FF_SHARD, D_MODEL)
    return st


def _reduce_grads(G, W):
    big_names = [n for n, _ in _BIG]
    stacks = _big_grad_stacks(G)
    small_grads = _assemble_grads(G)
    rep_flat = jnp.concatenate([small_grads[n].reshape(-1) for n in _REPLICATED])
    n_rep = rep_flat.shape[0]
    piece = -(-n_rep // (N_CHIPS * FLAT_W)) * FLAT_W
    rep_flat = jnp.pad(rep_flat, (0, N_CHIPS * piece - n_rep)).reshape(N_CHIPS, piece)
    small_blocks = []
    for s in range(N_CHIPS):
        parts = [_split(small_grads[n], ax)[s] for n, ax in _SMALL_SHARDED] + [rep_flat[s]]
        small_blocks.append(_pack(parts, 32, F32))
    gs = [stacks[n] for n in big_names] + [jnp.stack(small_blocks)]
    tags = big_names + ["small"]
    landed = _pair_send_other_halves(gs, "rs_pair_send")
    chip_sums = [_add_own_half(g, l, f"rs_pair_add_{t}", F32 if t == "small" else BF16) for g, l, t in zip(gs, landed, tags)]
    slots = _scatter_to_chips(chip_sums, "rs_scatter")
    halves = [_sum_slots(s, f"rs_sum_{t}") for s, t in zip(slots, tags)]
    mine = _pair_gather(halves, "rs_pair_gather")
    out = {n: m.reshape(W[n].shape) for n, m in zip(big_names, mine[:-1])}
    shapes = [W[n].shape for n, _ in _SMALL_SHARDED] + [(piece,)]
    parts = _unpack(mine[-1], shapes)
    out.update(zip([n for n, _ in _SMALL_SHARDED], parts[:-1]))
    rep_rows = piece // FLAT_W
    rep_pad = -(-rep_rows // 32) * 32
    rep_piece = jnp.pad(parts[-1].reshape(rep_rows, FLAT_W), ((0, rep_pad - rep_rows), (0, 0)))
    rep_all = _all_gather_chips([rep_piece], "gather_replicated_grads")[0][:, :rep_rows].reshape(-1)
    out.update(zip(_REPLICATED, _unpack(rep_all, [W[n].shape for n in _REPLICATED])))
    return out


def _update(W, M, V, grads):
    delta, new_m, new_v = {}, {}, {}
    small = [n for n in _ORDER if n not in dict(_BIG)]
    for n, _ in _BIG:
        shp = W[n].shape
        two_d = (-1, shp[-1])
        d, m, v = _adamw(W[n].reshape(two_d), grads[n].reshape(two_d), M[n].reshape(two_d), V[n].reshape(two_d), f"adamw_{n}")
        delta[n], new_m[n], new_v[n] = d.reshape(shp), m.reshape(shp), v.reshape(shp)
    packed = [_pack([src[n] for n in small], 8, F32) for src in (W, grads, M, V)]
    outs = _adamw(*packed, "adamw_small")
    shapes = [W[n].shape for n in small]
    for tgt, flat in zip((delta, new_m, new_v), outs):
        tgt.update(zip(small, _unpack(flat, shapes)))
    return delta, new_m, new_v


def kernel(x, attn_w_qkv, attn_w_o, rg_w_in, rg_conv_w, rg_conv_b, rg_w_a, rg_b_a, rg_w_x, rg_b_x, rg_lambda, rg_w_out, ffn_w_up, ffn_conv_w, ffn_conv_b, ffn_w_down, mix_pre_g, mix_post_g, ffn_pre_g, ffn_post_g, loss_target, m_attn_w_qkv, m_attn_w_o, m_rg_w_in, m_rg_conv_w, m_rg_conv_b, m_rg_w_a, m_rg_b_a, m_rg_w_x, m_rg_b_x, m_rg_lambda, m_rg_w_out, m_ffn_w_up, m_ffn_conv_w, m_ffn_conv_b, m_ffn_w_down, m_mix_pre_g, m_mix_post_g, m_ffn_pre_g, m_ffn_post_g, v_attn_w_qkv, v_attn_w_o, v_rg_w_in, v_rg_conv_w, v_rg_conv_b, v_rg_w_a, v_rg_b_a, v_rg_w_x, v_rg_b_x, v_rg_lambda, v_rg_w_out, v_ffn_w_up, v_ffn_conv_w, v_ffn_conv_b, v_ffn_w_down, v_mix_pre_g, v_mix_post_g, v_ffn_pre_g, v_ffn_post_g):
    W = dict(attn_w_qkv=attn_w_qkv, attn_w_o=attn_w_o, rg_w_in=rg_w_in, rg_conv_w=rg_conv_w, rg_conv_b=rg_conv_b,
             rg_w_a=rg_w_a, rg_b_a=rg_b_a, rg_w_x=rg_w_x, rg_b_x=rg_b_x, rg_lambda=rg_lambda, rg_w_out=rg_w_out,
             ffn_w_up=ffn_w_up, ffn_conv_w=ffn_conv_w, ffn_conv_b=ffn_conv_b, ffn_w_down=ffn_w_down,
             mix_pre_g=mix_pre_g, mix_post_g=mix_post_g, ffn_pre_g=ffn_pre_g, ffn_post_g=ffn_post_g)
    M = dict(attn_w_qkv=m_attn_w_qkv, attn_w_o=m_attn_w_o, rg_w_in=m_rg_w_in, rg_conv_w=m_rg_conv_w, rg_conv_b=m_rg_conv_b,
             rg_w_a=m_rg_w_a, rg_b_a=m_rg_b_a, rg_w_x=m_rg_w_x, rg_b_x=m_rg_b_x, rg_lambda=m_rg_lambda, rg_w_out=m_rg_w_out,
             ffn_w_up=m_ffn_w_up, ffn_conv_w=m_ffn_conv_w, ffn_conv_b=m_ffn_conv_b, ffn_w_down=m_ffn_w_down,
             mix_pre_g=m_mix_pre_g, mix_post_g=m_mix_post_g, ffn_pre_g=m_ffn_pre_g, ffn_post_g=m_ffn_post_g)
    V = dict(attn_w_qkv=v_attn_w_qkv, attn_w_o=v_attn_w_o, rg_w_in=v_rg_w_in, rg_conv_w=v_rg_conv_w, rg_conv_b=v_rg_conv_b,
             rg_w_a=v_rg_w_a, rg_b_a=v_rg_b_a, rg_w_x=v_rg_w_x, rg_b_x=v_rg_b_x, rg_lambda=v_rg_lambda, rg_w_out=v_rg_w_out,
             ffn_w_up=v_ffn_w_up, ffn_conv_w=v_ffn_conv_w, ffn_conv_b=v_ffn_conv_b, ffn_w_down=v_ffn_w_down,
             mix_pre_g=v_mix_pre_g, mix_post_g=v_mix_post_g, ffn_pre_g=v_ffn_pre_g, ffn_post_g=v_ffn_post_g)
    P = _layout_params(_gather_params(W))
    loss_part, grad_x, G = _local_step(x[0], loss_target[0], P)
    loss = lax.psum(loss_part, ("x", "y", "c"))
    grads = _reduce_grads(G, W)
    delta, new_m, new_v = _update(W, M, V, grads)
    return (loss, grad_x[None], *[grads[n] for n in _ORDER], *[delta[n] for n in _ORDER],
            *[new_m[n] for n in _ORDER], *[new_v[n] for n in _ORDER])
```

```python
import functools
import math

import jax
import jax.numpy as jnp
from jax import lax
from jax.experimental import pallas as pl
from jax.experimental.pallas import tpu as pltpu

D_MODEL = 1024
N_HEADS = 16
HEAD_DIM = 64
D_RNN = 1344
D_RNN_PAD = 1408
RG_BLOCKS = 16
RG_BLOCK_W = 84
D_FF = 2816
FFN_TILE = 256
NORM_EPS = 1e-6
RG_C = 8.0

ADAM_LR = 0.001
ADAM_B1 = 0.9
ADAM_B2 = 0.999
ADAM_EPS = 1e-08
ADAM_WD = 0.01
ADAM_STEP = 10

LANES = 128
SUBLANES = 8
VMEM_LIMIT = 56 * 1024 * 1024
FLAT_W = 1024
N_CHIPS = 4

F32 = jnp.float32
BF16 = jnp.bfloat16
MESH = pl.DeviceIdType.MESH


def _cp(*sem):
    return pltpu.CompilerParams(dimension_semantics=tuple(sem), vmem_limit_bytes=VMEM_LIMIT)


def _mm(a, b, mode, out_dtype, name, bm=1024, bn=1024, bk=1024, add=None):
    if mode == "nn":
        (M, K), (K2, N) = a.shape, b.shape
    elif mode == "nt":
        (M, K), (N, K2) = a.shape, b.shape
    else:
        (K, M), (K2, N) = a.shape, b.shape
    assert K == K2, (a.shape, b.shape, mode)
    bm, bn, bk = min(bm, M), min(bn, N), min(bk, K)
    assert M % bm == 0 and N % bn == 0 and K % bk == 0, (M, N, K, bm, bn, bk)
    nk = K // bk
    dims = {"nn": (((1,), (0,)), ((), ())), "nt": (((1,), (1,)), ((), ())), "tn": (((0,), (0,)), ((), ()))}[mode]
    has_add = add is not None

    def body(*refs):
        a_ref, b_ref = refs[0], refs[1]
        add_ref = refs[2] if has_add else None
        o_ref = refs[2 + has_add]
        p = lax.dot_general(a_ref[...].astype(BF16), b_ref[...].astype(BF16), dims, preferred_element_type=F32)
        if nk == 1:
            if has_add:
                p = p + add_ref[...]
            o_ref[...] = p.astype(o_ref.dtype)
        else:
            acc = refs[3 + has_add]
            k = pl.program_id(2)

            @pl.when(k == 0)
            def _():
                acc[...] = p

            @pl.when(k > 0)
            def _():
                acc[...] += p

            @pl.when(k == nk - 1)
            def _():
                r = acc[...]
                if has_add:
                    r = r + add_ref[...]
                o_ref[...] = r.astype(o_ref.dtype)

    if mode == "tn":
        a_spec = pl.BlockSpec((bk, bm), lambda i, j, k: (k, i))
    else:
        a_spec = pl.BlockSpec((bm, bk), lambda i, j, k: (i, k))
    if mode == "nt":
        b_spec = pl.BlockSpec((bn, bk), lambda i, j, k: (j, k))
    else:
        b_spec = pl.BlockSpec((bk, bn), lambda i, j, k: (k, j))
    o_spec = pl.BlockSpec((bm, bn), lambda i, j, k: (i, j))
    in_specs = [a_spec, b_spec] + ([o_spec] if has_add else [])
    args = (a, b) + ((add,) if has_add else ())
    return pl.pallas_call(
        body, name=name, grid=(M // bm, N // bn, nk),
        in_specs=in_specs, out_specs=o_spec,
        out_shape=jax.ShapeDtypeStruct((M, N), out_dtype),
        scratch_shapes=[pltpu.VMEM((bm, bn), F32)] if nk > 1 else [],
        compiler_params=_cp("parallel", "parallel", "arbitrary"),
    )(*args)


def _row_block(T, bm=512):
    bm = min(bm, T)
    assert T % bm == 0
    return bm


def _rmsnorm_fwd(x, g, name):
    T, D = x.shape
    bm = _row_block(T)

    def body(x_ref, g_ref, o_ref):
        xv = x_ref[...]
        inv = lax.rsqrt(jnp.mean(xv * xv, axis=-1, keepdims=True) + NORM_EPS)
        o_ref[...] = (xv * inv * g_ref[...]).astype(o_ref.dtype)

    return pl.pallas_call(
        body, name=name, grid=(T // bm,),
        in_specs=[pl.BlockSpec((bm, D), lambda i: (i, 0)), pl.BlockSpec((1, D), lambda i: (0, 0))],
        out_specs=pl.BlockSpec((bm, D), lambda i: (i, 0)),
        out_shape=jax.ShapeDtypeStruct((T, D), BF16),
        compiler_params=_cp("parallel"),
    )(x, g)


def _postnorm_residual(x, h, g, name):
    T, D = x.shape
    bm = _row_block(T)

    def body(x_ref, h_ref, g_ref, o_ref):
        hv = h_ref[...]
        inv = lax.rsqrt(jnp.mean(hv * hv, axis=-1, keepdims=True) + NORM_EPS)
        o_ref[...] = x_ref[...] + hv * inv * g_ref[...]

    return pl.pallas_call(
        body, name=name, grid=(T // bm,),
        in_specs=[pl.BlockSpec((bm, D), lambda i: (i, 0)), pl.BlockSpec((bm, D), lambda i: (i, 0)),
                  pl.BlockSpec((1, D), lambda i: (0, 0))],
        out_specs=pl.BlockSpec((bm, D), lambda i: (i, 0)),
        out_shape=jax.ShapeDtypeStruct((T, D), F32),
        compiler_params=_cp("parallel"),
    )(x, h, g)


def _postnorm_loss(x, h, g, target, name):
    T, D = x.shape
    bm = _row_block(T)

    def body(x_ref, h_ref, g_ref, t_ref, dy_ref, loss_ref):
        hv = h_ref[...]
        inv = lax.rsqrt(jnp.mean(hv * hv, axis=-1, keepdims=True) + NORM_EPS)
        err = x_ref[...] + hv * inv * g_ref[...] - t_ref[...]
        dy_ref[...] = err * (1.0 / D)
        part = jnp.sum(jnp.sum(err * err, axis=-1, keepdims=True), axis=0, keepdims=True) * (0.5 / D)

        @pl.when(pl.program_id(0) == 0)
        def _():
            loss_ref[...] = jnp.zeros_like(loss_ref)

        loss_ref[...] += jnp.broadcast_to(part, loss_ref.shape)

    row = pl.BlockSpec((bm, D), lambda i: (i, 0))
    return pl.pallas_call(
        body, name=name, grid=(T // bm,),
        in_specs=[row, row, pl.BlockSpec((1, D), lambda i: (0, 0)), row],
        out_specs=[row, pl.BlockSpec((1, LANES), lambda i: (0, 0))],
        out_shape=[jax.ShapeDtypeStruct((T, D), F32), jax.ShapeDtypeStruct((1, LANES), F32)],
        compiler_params=_cp("arbitrary"),
    )(x, h, g, target)


def _rmsnorm_bwd(dy, xin, g, name, res=None, out_dtype=F32):
    T, D = xin.shape
    bm = _row_block(T)
    has_res = res is not None

    def body(*refs):
        dy_ref, x_ref, g_ref = refs[:3]
        res_ref = refs[3] if has_res else None
        dx_ref, dg_ref = refs[3 + has_res], refs[4 + has_res]
        xv = x_ref[...]
        dyv = dy_ref[...].astype(F32)
        inv = lax.rsqrt(jnp.mean(xv * xv, axis=-1, keepdims=True) + NORM_EPS)
        xhat = xv * inv
        dyg = dyv * g_ref[...]
        c = jnp.mean(dyg * xhat, axis=-1, keepdims=True)
        dx = inv * (dyg - xhat * c)
        if has_res:
            dx = dx + res_ref[...]
        dx_ref[...] = dx.astype(dx_ref.dtype)

        @pl.when(pl.program_id(0) == 0)
        def _():
            dg_ref[...] = jnp.zeros_like(dg_ref)

        dg_ref[...] += jnp.sum(dyv * xhat, axis=0, keepdims=True)

    row = pl.BlockSpec((bm, D), lambda i: (i, 0))
    vec = pl.BlockSpec((1, D), lambda i: (0, 0))
    args = (dy, xin, g) + ((res,) if has_res else ())
    return pl.pallas_call(
        body, name=name, grid=(T // bm,),
        in_specs=[row, row, vec] + ([row] if has_res else []),
        out_specs=[row, vec],
        out_shape=[jax.ShapeDtypeStruct((T, D), out_dtype), jax.ShapeDtypeStruct((1, D), F32)],
        compiler_params=_cp("arbitrary"),
    )(*args)


_GELU_C = math.sqrt(2.0 / math.pi)


def _gelu(x):
    return 0.5 * x * (1.0 + jnp.tanh(_GELU_C * (x + 0.044715 * x * x * x)))


def _gelu_and_grad(x):
    t = jnp.tanh(_GELU_C * (x + 0.044715 * x * x * x))
    val = 0.5 * x * (1.0 + t)
    grad = 0.5 * (1.0 + t) + 0.5 * x * (1.0 - t * t) * _GELU_C * (1.0 + 3.0 * 0.044715 * x * x)
    return val, grad


def _sigmoid(x):
    return 1.0 / (1.0 + jnp.exp(-x))


def _shift_down(x, halo_prev, k):
    xe = jnp.concatenate([halo_prev, x], axis=0)
    return pltpu.roll(xe, k, 0)[SUBLANES:]


def _shift_up(x, halo_next, k):
    xe = jnp.concatenate([x, halo_next], axis=0)
    return pltpu.roll(xe, xe.shape[0] - k, 0)[:x.shape[0]]


def _halo_specs(T, bm, bc):
    r = bm // SUBLANES
    last = T // SUBLANES - 1
    prev = pl.BlockSpec((SUBLANES, bc), lambda j, i: (jnp.maximum(i * r - 1, 0), j))
    nxt = pl.BlockSpec((SUBLANES, bc), lambda j, i: (jnp.minimum((i + 1) * r, last), j))
    return prev, nxt


def _causal_conv(x, halo, w_ref, b_ref, width):
    out = x * w_ref[width - 1:width, :] + b_ref[...]
    for k in range(width - 1):
        out = out + _shift_down(x, halo, width - 1 - k) * w_ref[k:k + 1, :]
    return out


def _gate_value_specs(shape, row_map, nj):
    return [pl.BlockSpec(shape, lambda j, i: (row_map(i), j)), pl.BlockSpec(shape, lambda j, i: (row_map(i), j + nj))]


def _side_by_side(g_ref, v_ref):
    return jnp.concatenate([g_ref[...], v_ref[...]], axis=1)


def _ffn_act_fwd(u_pre, conv_w, conv_b, name):
    T, C2 = u_pre.shape
    bm = _row_block(T, 1024)
    nj = C2 // (2 * FFN_TILE)
    r = bm // SUBLANES

    def body(ug_ref, uv_ref, hg_ref, hv_ref, wg_ref, wv_ref, bg_ref, bv_ref, o_ref):
        i = pl.program_id(1)
        halo = jnp.where(i > 0, _side_by_side(hg_ref, hv_ref), 0.0)
        x = _side_by_side(ug_ref, uv_ref)
        w = _side_by_side(wg_ref, wv_ref)
        u = (_shift_down(x, halo, 2) * w[0:1, :] + _shift_down(x, halo, 1) * w[1:2, :] + x * w[2:3, :]
             + _side_by_side(bg_ref, bv_ref))
        o_ref[...] = (_gelu(u[:, :FFN_TILE]) * u[:, FFN_TILE:]).astype(o_ref.dtype)

    in_specs = (_gate_value_specs((bm, FFN_TILE), lambda i: i, nj)
                + _gate_value_specs((SUBLANES, FFN_TILE), lambda i: jnp.maximum(i * r - 1, 0), nj)
                + _gate_value_specs((SUBLANES, FFN_TILE), lambda i: 0, nj)
                + _gate_value_specs((1, FFN_TILE), lambda i: 0, nj))
    return pl.pallas_call(
        body, name=name, grid=(nj, T // bm), in_specs=in_specs,
        out_specs=pl.BlockSpec((bm, FFN_TILE), lambda j, i: (i, j)),
        out_shape=jax.ShapeDtypeStruct((T, C2 // 2), BF16),
        compiler_params=_cp("parallel", "parallel"),
    )(u_pre, u_pre, u_pre, u_pre, conv_w, conv_w, conv_b, conv_b)


def _ffn_act_bwd(u_pre, conv_w, conv_b, d_act, name):
    T, C2 = u_pre.shape
    bm = _row_block(T, 1024)
    bc = 2 * FFN_TILE
    nj = C2 // bc
    nt = T // bm
    r = bm // SUBLANES

    def body(ug_ref, uv_ref, hg_ref, hv_ref, wg_ref, wv_ref, bg_ref, bv_ref, da_ref, dxg_ref, dxv_ref, stg_ref, stv_ref, later):
        i = pl.program_id(1)

        @pl.when(i == 0)
        def _():
            stg_ref[...] = jnp.zeros_like(stg_ref)
            stv_ref[...] = jnp.zeros_like(stv_ref)
            later[...] = jnp.zeros_like(later)

        halo = jnp.where(i < nt - 1, _side_by_side(hg_ref, hv_ref), 0.0)
        x = _side_by_side(ug_ref, uv_ref)
        w = _side_by_side(wg_ref, wv_ref)
        xm1 = _shift_down(x, halo, 1)
        xm2 = _shift_down(x, halo, 2)
        u = xm2 * w[0:1, :] + xm1 * w[1:2, :] + x * w[2:3, :] + _side_by_side(bg_ref, bv_ref)
        gv, gg = _gelu_and_grad(u[:, :FFN_TILE])
        da = da_ref[...]
        du = jnp.concatenate([da * u[:, FFN_TILE:] * gg, da * gv], axis=1)
        nxt = later[...]
        dx = du * w[2:3, :] + _shift_up(du, nxt, 1) * w[1:2, :] + _shift_up(du, nxt, 2) * w[0:1, :]
        dxg_ref[...] = dx[:, :FFN_TILE].astype(dxg_ref.dtype)
        dxv_ref[...] = dx[:, FFN_TILE:].astype(dxv_ref.dtype)
        later[...] = du[:SUBLANES, :]
        rows = [jnp.sum(du * xm2, axis=0, keepdims=True), jnp.sum(du * xm1, axis=0, keepdims=True),
                jnp.sum(du * x, axis=0, keepdims=True), jnp.sum(du, axis=0, keepdims=True),
                jnp.zeros((4, bc), F32)]
        st = jnp.concatenate(rows, axis=0)
        stg_ref[...] += st[:, :FFN_TILE]
        stv_ref[...] += st[:, FFN_TILE:]

    rev = lambda i: nt - 1 - i
    in_specs = (_gate_value_specs((bm, FFN_TILE), rev, nj)
                + _gate_value_specs((SUBLANES, FFN_TILE), lambda i: jnp.maximum((nt - 1 - i) * r - 1, 0), nj)
                + _gate_value_specs((SUBLANES, FFN_TILE), lambda i: 0, nj)
                + _gate_value_specs((1, FFN_TILE), lambda i: 0, nj)
                + [pl.BlockSpec((bm, FFN_TILE), lambda j, i: (nt - 1 - i, j))])
    rows_spec = pl.BlockSpec((bm, FFN_TILE), lambda j, i: (nt - 1 - i, j))
    stat_spec = pl.BlockSpec((SUBLANES, FFN_TILE), lambda j, i: (0, j))
    dxg, dxv, stg, stv = pl.pallas_call(
        body, name=name, grid=(nj, nt), in_specs=in_specs,
        out_specs=[rows_spec, rows_spec, stat_spec, stat_spec],
        out_shape=[jax.ShapeDtypeStruct((T, C2 // 2), BF16)] * 2 + [jax.ShapeDtypeStruct((SUBLANES, C2 // 2), F32)] * 2,
        scratch_shapes=[pltpu.VMEM((SUBLANES, bc), F32)],
        compiler_params=_cp("parallel", "arbitrary"),
    )(u_pre, u_pre, u_pre, u_pre, conv_w, conv_w, conv_b, conv_b, d_act)
    return (dxg, dxv), jnp.concatenate([stg, stv], axis=1)


def _conv_transpose(dy, conv_w, width, name, bc):
    T, C = dy.shape
    bm = _row_block(T, 512)
    _, nxt = _halo_specs(T, bm, bc)
    nt = T // bm

    def body(y_ref, h_ref, w_ref, o_ref):
        i = pl.program_id(1)
        halo = jnp.where(i < nt - 1, h_ref[...], 0.0)
        y = y_ref[...]
        out = y * w_ref[width - 1:width, :]
        for k in range(width - 1):
            out = out + _shift_up(y, halo, width - 1 - k) * w_ref[k:k + 1, :]
        o_ref[...] = out.astype(o_ref.dtype)

    return pl.pallas_call(
        body, name=name, grid=(C // bc, nt),
        in_specs=[pl.BlockSpec((bm, bc), lambda j, i: (i, j)), nxt, pl.BlockSpec((SUBLANES, bc), lambda j, i: (0, j))],
        out_specs=pl.BlockSpec((bm, bc), lambda j, i: (i, j)),
        out_shape=jax.ShapeDtypeStruct((T, C), BF16),
        compiler_params=_cp("parallel", "parallel"),
    )(dy, dy, conv_w)


ATT_BK = 128


def _split_dot(x, uu):
    hi = x.astype(BF16)
    lo = (x - hi.astype(F32)).astype(BF16)
    return jnp.dot(jnp.concatenate([hi, lo], axis=1), uu, preferred_element_type=F32)


_SIGN_BIT = -2 ** 31


def _log_sigmoid(z):
    neg_abs = pltpu.bitcast(pltpu.bitcast(z, jnp.int32) | jnp.int32(_SIGN_BIT), F32)
    return jnp.minimum(z, 0.0) - jnp.log(1.0 + jnp.exp(neg_abs))


ATT_TQ = 512
ATT_KS = 512
ATT_SUB = ATT_KS // ATT_BK


def _suffix_consts():
    r = lax.broadcasted_iota(jnp.int32, (ATT_BK, 2 * ATT_BK), 0)
    c = lax.broadcasted_iota(jnp.int32, (ATT_BK, 2 * ATT_BK), 1)
    m = jnp.where((r > c) | (c >= ATT_BK), 1.0, 0.0).astype(BF16)
    return jnp.concatenate([m, m], axis=0)


def _suffix_sums(x, uu):
    ext = _split_dot(x, uu)
    return ext[:, :ATT_BK], ext[:, ATT_BK:]


def _masked_heads(x_f32, first):
    return (jnp.where(first, x_f32, 0.0).astype(BF16), jnp.where(first, 0.0, x_f32).astype(BF16))


def _unmasked_blocks(block, nfull, ks, T, carry):
    if 2 * ks > T:
        return lax.fori_loop(0, nfull, lambda i, c: block(nfull - 1 - i, ks, c, False), carry)
    carry = lax.fori_loop(0, jnp.bitwise_and(nfull, 1), lambda i, c: block(nfull - 1, ks, c, False), carry)
    pairs = nfull // 2
    return lax.fori_loop(0, pairs, lambda i, c: block(2 * (pairs - 1 - i), 2 * ks, c, False), carry)


def _sb_attn_fwd(qkv, name):
    T = qkv.shape[0]
    tq, ks = min(ATT_TQ, T), min(ATT_KS, T)
    assert T % tq == 0 and T % ks == 0 and ks % tq == 0
    nq = T // tq
    nsub = ks // ATT_BK
    nhp = N_HEADS // 2
    nt = (((1,), (1,)), ((), ()))


    def body(q_ref, k_ref, v_ref, o_ref):
        qi = pl.program_id(1)
        first = lax.broadcasted_iota(jnp.int32, (1, LANES), 1) < HEAD_DIM
        qs = _masked_heads(q_ref[...].astype(F32) * HEAD_DIM ** -0.5, first)
        uu = _suffix_consts()
        tpos = qi * tq + lax.broadcasted_iota(jnp.int32, (tq, ATT_BK), 0)
        col = lax.broadcasted_iota(jnp.int32, (tq, ATT_BK), 1)

        def block(kb, width, carry, masked):
            nsub = width // ATT_BK
            off = pl.multiple_of(kb * ks, ks)
            kblk = k_ref[pl.ds(off, width), :]
            vblk = v_ref[pl.ds(off, width), :]
            out = []
            for h in range(2):
                acc, run = carry[2 * h], carry[2 * h + 1]
                z = lax.dot_general(qs[h], kblk, nt, preferred_element_type=F32)
                lb = _log_sigmoid(z)
                l1m = lb - z
                ws = [None] * nsub
                for j in reversed(range(nsub)):
                    sl = slice(j * ATT_BK, (j + 1) * ATT_BK)
                    l1 = l1m[:, sl]
                    if masked:
                        causal = (off + j * ATT_BK + col) < tpos
                        l1 = jnp.where(causal, l1, 0.0)
                    cs, tot = _suffix_sums(l1, uu)
                    w = jnp.exp(lb[:, sl] + cs + run)
                    ws[j] = jnp.where(causal, w, 0.0) if masked else w
                    run = run + tot
                w = jnp.concatenate(ws, axis=1)
                acc = acc + jnp.dot(w.astype(BF16), vblk, preferred_element_type=F32)
                out += [acc, run]
            return tuple(out)

        nfull = (qi * tq) // ks
        carry = (jnp.zeros((tq, LANES), F32),) * 4
        carry = block(nfull, ks, carry, True)
        carry = _unmasked_blocks(block, nfull, ks, T, carry)
        o_ref[...] = jnp.where(first, carry[0], carry[2])

    return pl.pallas_call(
        body, name=name, grid=(nhp, nq),
        in_specs=[pl.BlockSpec((tq, LANES), lambda p, i: (i, p)),
                  pl.BlockSpec((T, LANES), lambda p, i: (0, nhp + p)),
                  pl.BlockSpec((T, LANES), lambda p, i: (0, 2 * nhp + p))],
        out_specs=pl.BlockSpec((tq, LANES), lambda p, i: (i, p)),
        out_shape=jax.ShapeDtypeStruct((T, D_MODEL), F32),
        compiler_params=_cp("parallel", "parallel"),
    )(qkv, qkv, qkv)


def _sb_attn_bwd(qkv, o, d_o, name):
    T = qkv.shape[0]
    tq, ks = min(ATT_TQ, T), min(ATT_KS, T)
    assert T % tq == 0 and T % ks == 0 and ks % tq == 0
    nq = T // tq
    nks = T // ks
    nsub = ks // ATT_BK
    nhp = N_HEADS // 2
    nt = (((1,), (1,)), ((), ()))

    def body(q_ref, k_ref, v_ref, o_ref, do_ref, dq_ref, dk_ref, dv_ref, dkt_acc, dvt_acc):
        qi = pl.program_id(1)

        @pl.when(qi == 0)
        def _():
            dkt_acc[...] = jnp.zeros_like(dkt_acc)
            dvt_acc[...] = jnp.zeros_like(dvt_acc)

        first = lax.broadcasted_iota(jnp.int32, (1, LANES), 1) < HEAD_DIM
        first_t = lax.broadcasted_iota(jnp.int32, (LANES, 1), 0) < HEAD_DIM
        qf = q_ref[...].astype(F32) * HEAD_DIM ** -0.5
        qs = _masked_heads(qf, first)
        qts = _masked_heads(qf.T, first_t)
        dof = do_ref[...].astype(BF16).astype(F32)
        dos = _masked_heads(dof, first)
        dots = _masked_heads(dof.T, first_t)
        prod = dof * o_ref[...]
        deltas = (jnp.broadcast_to(jnp.sum(jnp.where(first, prod, 0.0), axis=1, keepdims=True), (tq, LANES)),
                  jnp.broadcast_to(jnp.sum(jnp.where(first, 0.0, prod), axis=1, keepdims=True), (tq, LANES)))
        uu = _suffix_consts()
        tpos = qi * tq + lax.broadcasted_iota(jnp.int32, (tq, ATT_BK), 0)
        col = lax.broadcasted_iota(jnp.int32, (tq, ATT_BK), 1)

        def block(kb, width, carry, masked):
            nsub = width // ATT_BK
            off = pl.multiple_of(kb * ks, ks)
            kblk = k_ref[pl.ds(off, width), :]
            vblk = v_ref[pl.ds(off, width), :]
            out = []
            dkt = jnp.zeros((LANES, width), F32)
            dvt = jnp.zeros((LANES, width), F32)
            for h in range(2):
                dq, run, grun = carry[3 * h], carry[3 * h + 1], carry[3 * h + 2]
                z = lax.dot_general(qs[h], kblk, nt, preferred_element_type=F32)
                lb = _log_sigmoid(z)
                l1m = lb - z
                dw = lax.dot_general(dos[h], vblk, nt, preferred_element_type=F32)
                wbs, dzs = [None] * nsub, [None] * nsub
                for j in reversed(range(nsub)):
                    sl = slice(j * ATT_BK, (j + 1) * ATT_BK)
                    l1 = l1m[:, sl]
                    if masked:
                        causal = (off + j * ATT_BK + col) < tpos
                        l1 = jnp.where(causal, l1, 0.0)
                    cs, tot = _suffix_sums(l1, uu)
                    w = jnp.exp(lb[:, sl] + cs + run)
                    if masked:
                        w = jnp.where(causal, w, 0.0)
                    wbs[j] = w.astype(BF16)
                    g = wbs[j].astype(F32) * dw[:, sl]
                    gcs, gtot = _suffix_sums(g, uu)
                    dz = g - jnp.exp(lb[:, sl]) * (deltas[h] - grun - gcs)
                    if masked:
                        dz = jnp.where(causal, dz, 0.0)
                    dzs[j] = dz.astype(BF16)
                    run = run + tot
                    grun = grun + gtot
                wb = jnp.concatenate(wbs, axis=1)
                dzb = jnp.concatenate(dzs, axis=1)
                dq = dq + jnp.dot(dzb, kblk, preferred_element_type=F32)
                dkt = dkt + jnp.dot(qts[h], dzb, preferred_element_type=F32)
                dvt = dvt + jnp.dot(dots[h], wb, preferred_element_type=F32)
                out += [dq, run, grun]
            for part in range(width // ks):
                dkt_acc[kb + part] += dkt[:, part * ks:(part + 1) * ks]
                dvt_acc[kb + part] += dvt[:, part * ks:(part + 1) * ks]
            return tuple(out)

        nfull = (qi * tq) // ks
        carry = (jnp.zeros((tq, LANES), F32),) * 6
        carry = block(nfull, ks, carry, True)
        carry = _unmasked_blocks(block, nfull, ks, T, carry)
        dq_ref[...] = (jnp.where(first, carry[0], carry[3]) * HEAD_DIM ** -0.5).astype(dq_ref.dtype)

        @pl.when(qi == nq - 1)
        def _():
            for kb in range(nks):
                dk_ref[kb * ks:(kb + 1) * ks, :] = dkt_acc[kb].T.astype(dk_ref.dtype)
                dv_ref[kb * ks:(kb + 1) * ks, :] = dvt_acc[kb].T.astype(dv_ref.dtype)

    qspec = pl.BlockSpec((tq, LANES), lambda p, i: (i, p))
    full = pl.BlockSpec((T, LANES), lambda p, i: (0, p))
    return pl.pallas_call(
        body, name=name, grid=(nhp, nq),
        in_specs=[qspec, pl.BlockSpec((T, LANES), lambda p, i: (0, nhp + p)),
                  pl.BlockSpec((T, LANES), lambda p, i: (0, 2 * nhp + p)), qspec, qspec],
        out_specs=[qspec, full, full],
        out_shape=[jax.ShapeDtypeStruct((T, D_MODEL), BF16)] * 3,
        scratch_shapes=[pltpu.VMEM((nks, LANES, ks), F32), pltpu.VMEM((nks, LANES, ks), F32)],
        compiler_params=_cp("parallel", "arbitrary"),
    )(qkv, qkv, qkv, o, d_o)


def _rg_conv_fwd(x, conv_w, conv_b, name):
    T, C = x.shape
    bm = _row_block(T, 512)
    bc = C
    prev, _ = _halo_specs(T, bm, bc)

    def body(x_ref, h_ref, w_ref, b_ref, o_ref):
        halo = jnp.where(pl.program_id(1) > 0, h_ref[...], 0.0)
        o_ref[...] = _causal_conv(x_ref[...], halo, w_ref, b_ref, 4)

    return pl.pallas_call(
        body, name=name, grid=(C // bc, T // bm),
        in_specs=[pl.BlockSpec((bm, bc), lambda j, i: (i, j)), prev,
                  pl.BlockSpec((SUBLANES, bc), lambda j, i: (0, j)), pl.BlockSpec((1, bc), lambda j, i: (0, j))],
        out_specs=pl.BlockSpec((bm, bc), lambda j, i: (i, j)),
        out_shape=jax.ShapeDtypeStruct((T, C), F32),
        compiler_params=_cp("parallel", "parallel"),
    )(x, x, conv_w, conv_b)


def _rg_gate_math(ra, rx, rec, vec_ref):
    r = _sigmoid(ra + vec_ref[0:1, :])
    ig = _sigmoid(rx + vec_ref[1:2, :])
    lam = vec_ref[2:3, :]
    sp = jnp.maximum(-lam, 0.0) + jnp.log(1.0 + jnp.exp(-jnp.abs(lam)))
    log_a = -RG_C * r * sp
    a = jnp.exp(log_a)
    th = jnp.tanh(log_a)
    mult = jnp.sqrt(-2.0 * th / (1.0 - th))
    return r, ig, sp, a, mult


def _rg_gates_fwd(rax, rec, vecs, name):
    T, C = rec.shape
    bm = _row_block(T, 512)

    def body(ra_ref, rx_ref, rec_ref, vec_ref, a_ref, u_ref):
        recv = rec_ref[...]
        _, ig, _, a, mult = _rg_gate_math(ra_ref[...], rx_ref[...], recv, vec_ref)
        a_ref[...] = a
        u_ref[...] = mult * ig * recv

    blk = pl.BlockSpec((bm, C), lambda i: (i, 0))
    return pl.pallas_call(
        body, name=name, grid=(T // bm,),
        in_specs=[blk, pl.BlockSpec((bm, C), lambda i: (i, 1)), blk, pl.BlockSpec((SUBLANES, C), lambda i: (0, 0))],
        out_specs=[blk, blk],
        out_shape=[jax.ShapeDtypeStruct((T, C), F32)] * 2,
        compiler_params=_cp("parallel"),
    )(rax, rax, rec, vecs)


def _linear_scan(a, b, name, reverse=False, shift_coef=False):
    T, C = a.shape
    bm = _row_block(T, 256)
    bc = C
    nt = T // bm
    _, nxt = _halo_specs(T, bm, bc)
    groups = bm // SUBLANES

    def body(*refs):
        if shift_coef:
            a_ref, an_ref, b_ref, h_ref, carry = refs
        else:
            a_ref, b_ref, h_ref, carry = refs
        i = pl.program_id(1)

        @pl.when(i == 0)
        def _():
            carry[...] = jnp.zeros_like(carry)

        coef = a_ref[...]
        if shift_coef:
            coef = _shift_up(coef, an_ref[...], 1)
        val = b_ref[...]
        sub = jnp.bitwise_and(lax.broadcasted_iota(jnp.int32, (bm, bc), 0), SUBLANES - 1)
        for s in (1, 2, 4):
            if reverse:
                keep = sub < SUBLANES - s
                cs, vs = pltpu.roll(coef, bm - s, 0), pltpu.roll(val, bm - s, 0)
            else:
                keep = sub >= s
                cs, vs = pltpu.roll(coef, s, 0), pltpu.roll(val, s, 0)
            val = jnp.where(keep, coef * vs + val, val)
            coef = jnp.where(keep, coef * cs, coef)
        edge = carry[0:1, :]
        pieces = [None] * groups
        for g in (reversed(range(groups)) if reverse else range(groups)):
            rows = slice(g * SUBLANES, (g + 1) * SUBLANES)
            hg = val[rows] + coef[rows] * edge
            pieces[g] = hg
            edge = hg[0:1, :] if reverse else hg[SUBLANES - 1:SUBLANES, :]
        h_ref[...] = jnp.concatenate(pieces, axis=0)
        carry[...] = jnp.broadcast_to(edge, carry.shape)

    if reverse:
        blk = pl.BlockSpec((bm, bc), lambda j, i: (nt - 1 - i, j))
        r = bm // SUBLANES
        last = T // SUBLANES - 1
        nxt = pl.BlockSpec((SUBLANES, bc), lambda j, i: (jnp.minimum((nt - i) * r, last), j))
    else:
        blk = pl.BlockSpec((bm, bc), lambda j, i: (i, j))
    in_specs = [blk] + ([nxt] if shift_coef else []) + [blk]
    args = (a,) + ((a,) if shift_coef else ()) + (b,)
    return pl.pallas_call(
        body, name=name, grid=(C // bc, nt),
        in_specs=in_specs, out_specs=blk,
        out_shape=jax.ShapeDtypeStruct((T, C), F32),
        scratch_shapes=[pltpu.VMEM((SUBLANES, bc), F32)],
        compiler_params=_cp("parallel", "arbitrary"),
    )(*args)


def _rg_out_fwd(gate_branch, h, name):
    T, C = h.shape
    bm = _row_block(T, 512)

    def body(g_ref, h_ref, o_ref):
        o_ref[...] = (_gelu(g_ref[...]) * h_ref[...]).astype(o_ref.dtype)

    blk = pl.BlockSpec((bm, C), lambda i: (i, 0))
    return pl.pallas_call(
        body, name=name, grid=(T // bm,), in_specs=[blk, blk], out_specs=blk,
        out_shape=jax.ShapeDtypeStruct((T, C), BF16), compiler_params=_cp("parallel"),
    )(gate_branch, h)


def _rg_out_bwd(d_gh, gate_branch, h, name):
    T, C = h.shape
    bm = _row_block(T, 512)

    def body(d_ref, g_ref, h_ref, dg_ref, dh_ref):
        gv, gg = _gelu_and_grad(g_ref[...])
        d = d_ref[...]
        dg_ref[...] = (d * h_ref[...] * gg).astype(dg_ref.dtype)
        dh_ref[...] = d * gv

    blk = pl.BlockSpec((bm, C), lambda i: (i, 0))
    return pl.pallas_call(
        body, name=name, grid=(T // bm,), in_specs=[blk, blk, blk], out_specs=[blk, blk],
        out_shape=[jax.ShapeDtypeStruct((T, C), BF16), jax.ShapeDtypeStruct((T, C), F32)],
        compiler_params=_cp("parallel"),
    )(d_gh, gate_branch, h)


def _rg_gates_bwd(dhs, h, rax, rec, vecs, name):
    T, C = rec.shape
    bm = _row_block(T, 1024)
    bc = LANES
    prev, _ = _halo_specs(T, bm, bc)
    nc = C // bc

    def body(d_ref, h_ref, hh_ref, ra_ref, rx_ref, rec_ref, vec_ref, dra_ref, drx_ref, drec_ref, st_ref):
        i = pl.program_id(1)
        halo = jnp.where(i > 0, hh_ref[...], 0.0)
        recv = rec_ref[...]
        du = d_ref[...]
        r, ig, sp, a, mult = _rg_gate_math(ra_ref[...], rx_ref[...], recv, vec_ref)
        da = du * _shift_down(h_ref[...], halo, 1)
        d_i = du * mult * recv
        drec_ref[...] = du * mult * ig
        d_mult = du * ig * recv
        d_log_a = da * a - d_mult * (a * a) / mult
        d_ra = d_log_a * (-RG_C) * sp * r * (1.0 - r)
        d_rx = d_i * ig * (1.0 - ig)
        dra_ref[...] = d_ra.astype(dra_ref.dtype)
        drx_ref[...] = d_rx.astype(drx_ref.dtype)
        lam = vec_ref[2:3, :]
        d_sp = jnp.sum(d_log_a * (-RG_C) * r, axis=0, keepdims=True)
        rows = [jnp.sum(d_ra, axis=0, keepdims=True), jnp.sum(d_rx, axis=0, keepdims=True),
                d_sp * (_sigmoid(lam) - 1.0), jnp.zeros((5, bc), F32)]

        @pl.when(i == 0)
        def _():
            st_ref[...] = jnp.zeros_like(st_ref)

        st_ref[...] += jnp.concatenate(rows, axis=0)

    blk = pl.BlockSpec((bm, bc), lambda j, i: (i, j))
    blk_x = pl.BlockSpec((bm, bc), lambda j, i: (i, nc + j))
    vec = pl.BlockSpec((SUBLANES, bc), lambda j, i: (0, j))
    d_ra, d_rx, d_rec, stats = pl.pallas_call(
        body, name=name, grid=(nc, T // bm),
        in_specs=[blk, blk, prev, blk, blk_x, blk, vec],
        out_specs=[blk, blk, blk, vec],
        out_shape=[jax.ShapeDtypeStruct((T, C), BF16), jax.ShapeDtypeStruct((T, C), BF16),
                   jax.ShapeDtypeStruct((T, C), F32), jax.ShapeDtypeStruct((SUBLANES, C), F32)],
        compiler_params=_cp("parallel", "arbitrary"),
    )(dhs, h, h, rax, rax, rec, vecs)
    return d_ra, d_rx, d_rec, stats


def _rg_conv_bwd_stats(d_rec, x, name):
    T, C = x.shape
    bm = _row_block(T, 1024)
    bc = LANES
    prev, _ = _halo_specs(T, bm, bc)

    def body(d_ref, x_ref, h_ref, st_ref):
        i = pl.program_id(1)
        halo = jnp.where(i > 0, h_ref[...], 0.0)
        d = d_ref[...]
        xv = x_ref[...]
        rows = [jnp.sum(d * _shift_down(xv, halo, 3 - k), axis=0, keepdims=True) for k in range(3)]
        rows += [jnp.sum(d * xv, axis=0, keepdims=True), jnp.sum(d, axis=0, keepdims=True), jnp.zeros((3, bc), F32)]

        @pl.when(i == 0)
        def _():
            st_ref[...] = jnp.zeros_like(st_ref)

        st_ref[...] += jnp.concatenate(rows, axis=0)

    blk = pl.BlockSpec((bm, bc), lambda j, i: (i, j))
    vec = pl.BlockSpec((SUBLANES, bc), lambda j, i: (0, j))
    return pl.pallas_call(
        body, name=name, grid=(C // bc, T // bm),
        in_specs=[blk, blk, prev], out_specs=vec,
        out_shape=jax.ShapeDtypeStruct((SUBLANES, C), F32),
        compiler_params=_cp("parallel", "arbitrary"),
    )(d_rec, x, x)


def _adamw(w, g, m, v, name):
    R, W = w.shape
    bm = R
    for cand in (512, 256, 128, 64, 32, 16, 8):
        if R % cand == 0:
            bm = cand
            break
    bc1 = 1.0 - ADAM_B1 ** ADAM_STEP
    bc2 = 1.0 - ADAM_B2 ** ADAM_STEP

    def body(w_ref, g_ref, m_ref, v_ref, d_ref, nm_ref, nv_ref):
        gv = g_ref[...]
        nm = ADAM_B1 * m_ref[...] + (1.0 - ADAM_B1) * gv
        nv = ADAM_B2 * v_ref[...] + (1.0 - ADAM_B2) * (gv * gv)
        m_hat = nm / bc1
        v_hat = nv / bc2
        d_ref[...] = -ADAM_LR * (m_hat / (jnp.sqrt(v_hat) + ADAM_EPS) + ADAM_WD * w_ref[...])
        nm_ref[...] = nm
        nv_ref[...] = nv

    blk = pl.BlockSpec((bm, W), lambda i: (i, 0))
    return pl.pallas_call(
        body, name=name, grid=(R // bm,), in_specs=[blk] * 4, out_specs=[blk] * 3,
        out_shape=[jax.ShapeDtypeStruct((R, W), F32)] * 3, compiler_params=_cp("parallel"),
    )(w, g, m, v)


_ANY = pl.BlockSpec(memory_space=pl.ANY)


def _me():
    return lax.axis_index("x"), lax.axis_index("y"), lax.axis_index("c")


def _other_chips(x, y):
    return [(1 - x, y), (x, 1 - y), (1 - x, 1 - y)]


def _remote(src, dst, send_sem, recv_sem, to):
    return pltpu.make_async_remote_copy(src_ref=src, dst_ref=dst, send_sem=send_sem, recv_sem=recv_sem,
                                        device_id=to, device_id_type=MESH)


def _half_rows(c, rows_half, align):
    return pl.ds(pl.multiple_of(c * rows_half, align), rows_half)


def _divisor_block(rows, cap=512):
    if rows <= cap:
        return rows
    best = None
    for cand in range(16, cap + 1, 16):
        if rows % cand == 0:
            best = cand
    assert best is not None, rows
    return best


def _all_gather_chips(shards, name):
    n = len(shards)
    halves = [s.shape[0] // 2 for s in shards]
    for s, rh in zip(shards, halves):
        assert 2 * rh == s.shape[0] and rh % 16 == 0, s.shape

    def body(*refs):
        x_refs, out_refs = refs[:n], refs[n:2 * n]
        send_sems, recv_sems = refs[2 * n:]
        x, y, c = _me()
        s_me = 2 * x + y
        chips = _other_chips(x, y)
        sends = []
        for a in range(n):
            half = _half_rows(c, halves[a], 16)
            for j, (cx, cy) in enumerate(chips):
                cp = _remote(x_refs[a].at[half], out_refs[a].at[s_me, half], send_sems.at[6 * a + j],
                             recv_sems.at[6 * a + j], (cx, cy, c))
                cp.start()
                sends.append(cp)
        for a in range(n):
            half = _half_rows(c, halves[a], 16)
            for j, (cx, cy) in enumerate(chips):
                landed = out_refs[a].at[2 * cx + cy, half]
                _remote(landed, landed, send_sems.at[6 * a + j], recv_sems.at[6 * a + j], (cx, cy, c)).wait_recv()
                fwd = _remote(landed, landed, send_sems.at[6 * a + 3 + j], recv_sems.at[6 * a + 3 + j], (x, y, 1 - c))
                fwd.start()
                sends.append(fwd)
        for a in range(n):
            other_half = _half_rows(1 - c, halves[a], 16)
            for j, (cx, cy) in enumerate(chips):
                landed = out_refs[a].at[2 * cx + cy, other_half]
                _remote(landed, landed, send_sems.at[6 * a + 3 + j], recv_sems.at[6 * a + 3 + j], (x, y, 1 - c)).wait_recv()
        for cp in sends:
            cp.wait_send()

    outs = pl.pallas_call(
        body, name=name, in_specs=[_ANY] * n, out_specs=[_ANY] * n,
        out_shape=[jax.ShapeDtypeStruct((N_CHIPS,) + s.shape, s.dtype) for s in shards],
        scratch_shapes=[pltpu.SemaphoreType.DMA((6 * n,)), pltpu.SemaphoreType.DMA((6 * n,))],
    )(*shards)
    s_me = 2 * lax.axis_index("x") + lax.axis_index("y")
    return [lax.dynamic_update_slice(o, s[None], (s_me, 0, 0)) for o, s in zip(outs, shards)]


def _pair_send_other_halves(gs, name):
    n = len(gs)
    halves = [g.shape[1] // 2 for g in gs]
    for g, rh in zip(gs, halves):
        assert 2 * rh == g.shape[1] and rh % 8 == 0, g.shape

    def body(*refs):
        g_refs, land_refs = refs[:n], refs[n:2 * n]
        send_sems, recv_sems = refs[2 * n:]
        x, y, c = _me()
        copies = []
        for a in range(n):
            src = g_refs[a].at[:, _half_rows(1 - c, halves[a], 8), :]
            cp = _remote(src, land_refs[a], send_sems.at[a], recv_sems.at[a], (x, y, 1 - c))
            cp.start()
            copies.append(cp)
        for cp in copies:
            cp.wait()

    return pl.pallas_call(
        body, name=name, in_specs=[_ANY] * n, out_specs=[_ANY] * n,
        out_shape=[jax.ShapeDtypeStruct((g.shape[0], rh, g.shape[2]), g.dtype) for g, rh in zip(gs, halves)],
        scratch_shapes=[pltpu.SemaphoreType.DMA((n,)), pltpu.SemaphoreType.DMA((n,))],
    )(*gs)


def _add_own_half(g, landed, name, out_dtype):
    n, R, W = g.shape
    Rh = R // 2
    bm = _divisor_block(Rh)
    nb = Rh // bm
    c_arr = lax.axis_index("c").astype(jnp.int32).reshape((1,))

    def body(c_ref, g_ref, l_ref, o_ref):
        o_ref[...] = (g_ref[...] + l_ref[...]).astype(o_ref.dtype)

    grid_spec = pltpu.PrefetchScalarGridSpec(
        num_scalar_prefetch=1, grid=(n, nb),
        in_specs=[pl.BlockSpec((1, bm, W), lambda s, i, c_ref: (s, c_ref[0] * nb + i, 0)),
                  pl.BlockSpec((1, bm, W), lambda s, i, c_ref: (s, i, 0))],
        out_specs=pl.BlockSpec((1, bm, W), lambda s, i, c_ref: (s, i, 0)))
    return pl.pallas_call(
        body, name=name, grid_spec=grid_spec, out_shape=jax.ShapeDtypeStruct((n, Rh, W), out_dtype),
        compiler_params=_cp("parallel", "parallel"),
    )(c_arr, g, landed)


def _scatter_to_chips(ps, name):
    n = len(ps)

    def body(*refs):
        p_refs, land_refs = refs[:n], refs[n:2 * n]
        send_sems, recv_sems = refs[2 * n:]
        x, y, c = _me()
        s_me = 2 * x + y
        chips = _other_chips(x, y)
        sends = []
        for a in range(n):
            for j, (cx, cy) in enumerate(chips):
                cp = _remote(p_refs[a].at[2 * cx + cy], land_refs[a].at[s_me], send_sems.at[3 * a + j],
                             recv_sems.at[3 * a + j], (cx, cy, c))
                cp.start()
                sends.append(cp)
        for a in range(n):
            for j, (cx, cy) in enumerate(chips):
                slot = land_refs[a].at[2 * cx + cy]
                _remote(slot, slot, send_sems.at[3 * a + j], recv_sems.at[3 * a + j], (cx, cy, c)).wait_recv()
        for cp in sends:
            cp.wait_send()

    lands = pl.pallas_call(
        body, name=name, in_specs=[_ANY] * n, out_specs=[_ANY] * n,
        out_shape=[jax.ShapeDtypeStruct(p.shape, p.dtype) for p in ps],
        scratch_shapes=[pltpu.SemaphoreType.DMA((3 * n,)), pltpu.SemaphoreType.DMA((3 * n,))],
    )(*ps)
    s_me = 2 * lax.axis_index("x") + lax.axis_index("y")
    return [lax.dynamic_update_slice(l, lax.dynamic_slice(p, (s_me, 0, 0), (1,) + p.shape[1:]), (s_me, 0, 0))
            for l, p in zip(lands, ps)]


def _sum_slots(l, name):
    n, Rh, W = l.shape
    bm = _divisor_block(Rh, 256)
    c_arr = lax.axis_index("c").astype(jnp.int32).reshape((1,))

    def body(c_ref, l_ref, o_ref):
        o_ref[0] = ((l_ref[0].astype(F32) + l_ref[1].astype(F32)) + l_ref[2].astype(F32)) + l_ref[3].astype(F32)

    grid_spec = pltpu.PrefetchScalarGridSpec(
        num_scalar_prefetch=1, grid=(Rh // bm,),
        in_specs=[pl.BlockSpec((n, bm, W), lambda i, c_ref: (0, i, 0))],
        out_specs=pl.BlockSpec((1, bm, W), lambda i, c_ref: (c_ref[0], i, 0)))
    return pl.pallas_call(
        body, name=name, grid_spec=grid_spec,
        out_shape=jax.ShapeDtypeStruct((2, Rh, W), F32), compiler_params=_cp("parallel"),
    )(c_arr, l)


def _pair_gather(fs, name):
    n = len(fs)

    def body(*refs):
        f_refs, out_refs = refs[:n], refs[n:2 * n]
        send_sems, recv_sems = refs[2 * n:]
        x, y, c = _me()
        sends = []
        for a in range(n):
            cp = _remote(f_refs[a].at[c], out_refs[a].at[c], send_sems.at[a], recv_sems.at[a], (x, y, 1 - c))
            cp.start()
            sends.append(cp)
        for a in range(n):
            sib = out_refs[a].at[1 - c]
            _remote(sib, sib, send_sems.at[a], recv_sems.at[a], (x, y, 1 - c)).wait_recv()
        for cp in sends:
            cp.wait_send()

    return pl.pallas_call(
        body, name=name, in_specs=[_ANY] * n, out_specs=[_ANY] * n,
        out_shape=[jax.ShapeDtypeStruct(f.shape, f.dtype) for f in fs],
        input_output_aliases={a: a for a in range(n)},
        scratch_shapes=[pltpu.SemaphoreType.DMA((n,)), pltpu.SemaphoreType.DMA((n,))],
    )(*fs)


def _pack(arrays, rows_multiple, dtype):
    flat = jnp.concatenate([a.reshape(-1).astype(dtype) for a in arrays])
    n = flat.shape[0]
    rows = -(-n // FLAT_W)
    rows = -(-rows // rows_multiple) * rows_multiple
    return jnp.pad(flat, (0, rows * FLAT_W - n)).reshape(rows, FLAT_W)


def _unpack(flat, shapes):
    flat = flat.reshape(-1)
    out, off = [], 0
    for shp in shapes:
        n = math.prod(shp)
        out.append(flat[off:off + n].reshape(shp))
        off += n
    return out


def _pad_cols(a, width):
    return jnp.pad(a, [(0, 0)] * (a.ndim - 1) + [(0, width - a.shape[-1])])


def _rows8(a):
    return jnp.pad(a, ((0, SUBLANES - a.shape[0]), (0, 0)))


def _block_diag_dense(w):
    eye = jnp.eye(RG_BLOCKS, dtype=w.dtype)
    dense = jnp.einsum("ncd,nm->ncmd", w, eye).reshape(D_RNN, D_RNN)
    return jnp.pad(dense, ((0, D_RNN_PAD - D_RNN), (0, D_RNN_PAD - D_RNN)))


def _block_diag_extract(dense):
    d = dense[:D_RNN, :D_RNN].reshape(RG_BLOCKS, RG_BLOCK_W, RG_BLOCKS, RG_BLOCK_W)
    return jnp.stack([d[n, :, n, :] for n in range(RG_BLOCKS)])


def _ffn_fwd(x, l, P, tag):
    n = _rmsnorm_fwd(x, P["ffn_pre_g"][l], f"{tag}_prenorm")
    u_pre = _mm(n, P["ffn_w_up"][l], "nn", F32, f"{tag}_up", bn=2 * D_FF // 4)
    act = _ffn_act_fwd(u_pre, P["ffn_conv_w"][l], P["ffn_conv_b"][l], f"{tag}_act")
    f = _mm(act, P["ffn_w_down"][l], "nn", F32, f"{tag}_down", bk=D_FF)
    return f, (n, u_pre, act)


def _ffn_bwd(df, x, saved, l, P, tag):
    n, u_pre, act = saved
    g_down = _mm(act, df, "tn", F32, f"{tag}_dwdown", bm=D_FF // 2, bk=1024)
    d_act = _mm(df, P["ffn_w_down"][l], "nt", F32, f"{tag}_dact", bn=D_FF // 2)
    (du_g, du_v), stats = _ffn_act_bwd(u_pre, P["ffn_conv_w"][l], P["ffn_conv_b"][l], d_act, f"{tag}_actbwd")
    g_up_g = _mm(n, du_g, "tn", F32, f"{tag}_dwup_gate", bn=D_FF // 2, bk=1024)
    g_up_v = _mm(n, du_v, "tn", F32, f"{tag}_dwup_value", bn=D_FF // 2, bk=1024)
    dn = _mm(du_g, P["ffn_w_up_gate"][l], "nt", F32, f"{tag}_dn_gate", bk=D_FF // 2)
    dn = _mm(du_v, P["ffn_w_up_value"][l], "nt", F32, f"{tag}_dn_value", bk=D_FF // 2, add=dn)
    return dn, dict(ffn_w_down=g_down, ffn_w_up_gate=g_up_g, ffn_w_up_value=g_up_v, ffn_conv_w=stats[0:3], ffn_conv_b=stats[3])


def _local_step(x, target, P):
    G = {}
    n0 = _rmsnorm_fwd(x, P["mix_pre_g"][0], "l0_prenorm")
    qkv = _mm(n0, P["attn_w_qkv"], "nn", BF16, "l0_qkv")
    o = _sb_attn_fwd(qkv, "l0_attn")
    a0 = _mm(o, P["attn_w_o"], "nn", F32, "l0_wo")
    x1 = _postnorm_residual(x, a0, P["mix_post_g"][0], "l0_postnorm")
    f0, ffn0 = _ffn_fwd(x1, 0, P, "f0")
    x2 = _postnorm_residual(x1, f0, P["ffn_post_g"][0], "f0_postnorm")
    n1 = _rmsnorm_fwd(x2, P["mix_pre_g"][1], "l1_prenorm")
    gate_b = _mm(n1, P["rg_w_in_gate"], "nn", F32, "l1_in_gate", bn=D_RNN_PAD)
    rec_b = _mm(n1, P["rg_w_in_rec"], "nn", F32, "l1_in_rec", bn=D_RNN_PAD)
    rec = _rg_conv_fwd(rec_b, P["rg_conv_w"], P["rg_conv_b"], "l1_conv")
    rax = _mm(rec, P["rg_w_ax"], "nn", F32, "l1_gates_mm", bn=D_RNN_PAD, bk=D_RNN_PAD)
    a, u = _rg_gates_fwd(rax, rec, P["rg_vecs"], "l1_gates")
    h = _linear_scan(a, u, "l1_scan")
    gh = _rg_out_fwd(gate_b, h, "l1_gate_out")
    y1 = _mm(gh, P["rg_w_out"], "nn", F32, "l1_wout", bk=D_RNN_PAD)
    x3 = _postnorm_residual(x2, y1, P["mix_post_g"][1], "l1_postnorm")
    f1, ffn1 = _ffn_fwd(x3, 1, P, "f1")
    dx, loss = _postnorm_loss(x3, f1, P["ffn_post_g"][1], target, "loss")

    df, G["ffn_post_g1"] = _rmsnorm_bwd(dx, f1, P["ffn_post_g"][1], "f1_postnorm_bwd", out_dtype=BF16)
    dn, g = _ffn_bwd(df, x3, ffn1, 1, P, "f1")
    G.update({k + "1": v for k, v in g.items()})
    dx, G["ffn_pre_g1"] = _rmsnorm_bwd(dn, x3, P["ffn_pre_g"][1], "f1_prenorm_bwd", res=dx)
    dy, G["mix_post_g1"] = _rmsnorm_bwd(dx, y1, P["mix_post_g"][1], "l1_postnorm_bwd", out_dtype=BF16)
    G["rg_w_out"] = _mm(gh, dy, "tn", F32, "l1_dwout", bm=D_RNN_PAD, bk=1024)
    d_gh = _mm(dy, P["rg_w_out"], "nt", F32, "l1_dgh", bn=D_RNN_PAD)
    d_gate, d_h = _rg_out_bwd(d_gh, gate_b, h, "l1_gate_out_bwd")
    dhs = _linear_scan(a, d_h, "l1_scan_bwd", reverse=True, shift_coef=True)
    d_ra, d_rx, d_rec_direct, st = _rg_gates_bwd(dhs, h, rax, rec, P["rg_vecs"], "l1_gates_bwd")
    G["rg_b_a"], G["rg_b_x"], G["rg_lambda"] = st[0], st[1], st[2]
    d_rax = jnp.concatenate([d_ra, d_rx], axis=1)
    G["rg_w_ax"] = _mm(rec, d_rax, "tn", F32, "l1_dwax", bm=D_RNN_PAD, bn=D_RNN_PAD, bk=1024)
    d_rec = _mm(d_rax, P["rg_w_ax"], "nt", F32, "l1_drec", bm=512, bn=D_RNN_PAD, bk=2 * D_RNN_PAD, add=d_rec_direct)
    st = _rg_conv_bwd_stats(d_rec, rec_b, "l1_conv_stats")
    G["rg_conv_w"], G["rg_conv_b"] = st[0:4], st[4]
    d_rec_b = _conv_transpose(d_rec, P["rg_conv_w"], 4, "l1_convT", D_RNN_PAD)
    G["rg_w_in_gate"] = _mm(n1, d_gate, "tn", F32, "l1_dwin_gate", bn=D_RNN_PAD, bk=1024)
    G["rg_w_in_rec"] = _mm(n1, d_rec_b, "tn", F32, "l1_dwin_rec", bn=D_RNN_PAD, bk=1024)
    dn = _mm(d_gate, P["rg_w_in_gate"], "nt", F32, "l1_dn_gate", bk=D_RNN_PAD)
    dn = _mm(d_rec_b, P["rg_w_in_rec"], "nt", F32, "l1_dn_rec", bk=D_RNN_PAD, add=dn)
    dx, G["mix_pre_g1"] = _rmsnorm_bwd(dn, x2, P["mix_pre_g"][1], "l1_prenorm_bwd", res=dx)
    df, G["ffn_post_g0"] = _rmsnorm_bwd(dx, f0, P["ffn_post_g"][0], "f0_postnorm_bwd", out_dtype=BF16)
    dn, g = _ffn_bwd(df, x1, ffn0, 0, P, "f0")
    G.update({k + "0": v for k, v in g.items()})
    dx, G["ffn_pre_g0"] = _rmsnorm_bwd(dn, x1, P["ffn_pre_g"][0], "f0_prenorm_bwd", res=dx)
    da, G["mix_post_g0"] = _rmsnorm_bwd(dx, a0, P["mix_post_g"][0], "l0_postnorm_bwd", out_dtype=BF16)
    G["attn_w_o"] = _mm(o, da, "tn", F32, "l0_dwo", bk=1024)
    d_o = _mm(da, P["attn_w_o"], "nt", F32, "l0_do")
    dq, dk, dv = _sb_attn_bwd(qkv, o, d_o, "l0_attn_bwd")
    d_qkv = jnp.concatenate([dq, dk, dv], axis=1)
    G["attn_w_qkv"] = _mm(n0, d_qkv, "tn", F32, "l0_dwqkv", bk=1024)
    dn = _mm(d_qkv, P["attn_w_qkv"], "nt", F32, "l0_dn")
    dx, G["mix_pre_g0"] = _rmsnorm_bwd(dn, x, P["mix_pre_g"][0], "l0_prenorm_bwd", res=dx)
    return loss[0, 0], dx, G


_BIG = [("attn_w_qkv", 2), ("attn_w_o", 1), ("rg_w_in", 2), ("rg_w_out", 1), ("ffn_w_up", 2), ("ffn_w_down", 1)]
_SMALL_SHARDED = [("rg_conv_w", 2), ("rg_conv_b", 1), ("rg_b_a", 1), ("rg_b_x", 1), ("rg_lambda", 1), ("ffn_conv_w", 2)]
_REPLICATED = ["rg_w_a", "rg_w_x", "ffn_conv_b", "mix_pre_g", "mix_post_g", "ffn_pre_g", "ffn_post_g"]
_ORDER = ["attn_w_qkv", "attn_w_o", "rg_w_in", "rg_conv_w", "rg_conv_b", "rg_w_a", "rg_b_a", "rg_w_x", "rg_b_x",
          "rg_lambda", "rg_w_out", "ffn_w_up", "ffn_conv_w", "ffn_conv_b", "ffn_w_down", "mix_pre_g", "mix_post_g",
          "ffn_pre_g", "ffn_post_g"]


def _join(gathered, shapes, axes):
    per_chip = [_unpack(gathered[s], shapes) for s in range(N_CHIPS)]
    return [jnp.concatenate([per_chip[s][i] for s in range(N_CHIPS)], axis=ax) for i, ax in enumerate(axes)]


def _split(full, axis):
    return jnp.split(full, N_CHIPS, axis=axis)


RG_OUT_SHARD_ROWS = D_RNN // N_CHIPS
RG_OUT_SHARD_ROWS_PAD = 352
D_FF_SHARD = D_FF // N_CHIPS


def _two_d(a):
    return a.reshape(-1, a.shape[-1])


def _gather_params(W):
    big = {n: _two_d(W[n]).astype(BF16) for n, _ in _BIG}
    big["rg_w_out"] = jnp.pad(big["rg_w_out"], ((0, RG_OUT_SHARD_ROWS_PAD - RG_OUT_SHARD_ROWS), (0, 0)))
    small_shapes = [W[n].shape for n, _ in _SMALL_SHARDED]
    small_flat = _pack([W[n] for n, _ in _SMALL_SHARDED], 32, F32)
    got = _all_gather_chips([big[n] for n, _ in _BIG] + [small_flat], "gather_weights")
    g = dict(zip([n for n, _ in _BIG], got[:-1]))
    full = {}
    full["attn_w_qkv"] = jnp.swapaxes(g["attn_w_qkv"], 0, 1).reshape(1, D_MODEL, 3 * D_MODEL)
    full["attn_w_o"] = g["attn_w_o"].reshape(1, D_MODEL, D_MODEL)
    full["rg_w_in"] = jnp.swapaxes(g["rg_w_in"], 0, 1).reshape(1, D_MODEL, 2 * D_RNN)
    full["rg_w_out"] = g["rg_w_out"][:, :RG_OUT_SHARD_ROWS].reshape(1, D_RNN, D_MODEL)
    up = g["ffn_w_up"].reshape(N_CHIPS, 2, D_MODEL, 2 * D_FF // N_CHIPS)
    full["ffn_w_up"] = jnp.transpose(up, (1, 2, 0, 3)).reshape(2, D_MODEL, 2 * D_FF)
    down = g["ffn_w_down"].reshape(N_CHIPS, 2, D_FF_SHARD, D_MODEL)
    full["ffn_w_down"] = jnp.swapaxes(down, 0, 1).reshape(2, D_FF, D_MODEL)
    full.update(zip([n for n, _ in _SMALL_SHARDED], _join(got[-1], small_shapes, [ax for _, ax in _SMALL_SHARDED])))
    for n in _REPLICATED:
        full[n] = W[n]
    return full


def _layout_params(full):
    P = {}
    P["attn_w_qkv"] = full["attn_w_qkv"][0]
    P["attn_w_o"] = full["attn_w_o"][0]
    w_in = full["rg_w_in"][0]
    P["rg_w_in_gate"] = _pad_cols(w_in[:, :D_RNN], D_RNN_PAD)
    P["rg_w_in_rec"] = _pad_cols(w_in[:, D_RNN:], D_RNN_PAD)
    P["rg_w_out"] = jnp.pad(full["rg_w_out"][0], ((0, D_RNN_PAD - D_RNN), (0, 0)))
    P["rg_conv_w"] = _rows8(_pad_cols(full["rg_conv_w"][0], D_RNN_PAD))
    P["rg_conv_b"] = _pad_cols(full["rg_conv_b"], D_RNN_PAD)
    P["rg_vecs"] = _rows8(_pad_cols(jnp.concatenate([full["rg_b_a"], full["rg_b_x"], full["rg_lambda"]], axis=0), D_RNN_PAD))
    P["rg_w_ax"] = jnp.concatenate([_block_diag_dense(full["rg_w_a"][0]), _block_diag_dense(full["rg_w_x"][0])], axis=1).astype(BF16)
    P["ffn_w_up"] = [full["ffn_w_up"][l] for l in range(2)]
    P["ffn_w_up_gate"] = [full["ffn_w_up"][l][:, :D_FF] for l in range(2)]
    P["ffn_w_up_value"] = [full["ffn_w_up"][l][:, D_FF:] for l in range(2)]
    P["ffn_conv_w"] = [_rows8(full["ffn_conv_w"][l]) for l in range(2)]
    P["ffn_conv_b"] = [full["ffn_conv_b"][l:l + 1] for l in range(2)]
    P["ffn_w_down"] = [full["ffn_w_down"][l] for l in range(2)]
    for n in ("mix_pre_g", "mix_post_g", "ffn_pre_g", "ffn_post_g"):
        P[n] = [full[n][l:l + 1] for l in range(2)]
    return P


def _assemble_grads(G):
    out = {}
    out["attn_w_qkv"] = G["attn_w_qkv"][None]
    out["attn_w_o"] = G["attn_w_o"][None]
    out["rg_w_in"] = jnp.concatenate([G["rg_w_in_gate"][:, :D_RNN], G["rg_w_in_rec"][:, :D_RNN]], axis=1)[None]
    out["rg_conv_w"] = G["rg_conv_w"][:, :D_RNN][None]
    out["rg_conv_b"] = G["rg_conv_b"][:D_RNN][None]
    out["rg_w_a"] = _block_diag_extract(G["rg_w_ax"][:, :D_RNN_PAD])[None]
    out["rg_w_x"] = _block_diag_extract(G["rg_w_ax"][:, D_RNN_PAD:])[None]
    out["rg_b_a"] = G["rg_b_a"][:D_RNN][None]
    out["rg_b_x"] = G["rg_b_x"][:D_RNN][None]
    out["rg_lambda"] = G["rg_lambda"][:D_RNN][None]
    out["rg_w_out"] = G["rg_w_out"][:D_RNN][None]
    out["ffn_w_up"] = jnp.stack([jnp.concatenate([G[f"ffn_w_up_gate{l}"], G[f"ffn_w_up_value{l}"]], axis=1) for l in range(2)])
    out["ffn_conv_w"] = jnp.stack([G[f"ffn_conv_w{l}"] for l in range(2)])
    out["ffn_conv_b"] = jnp.stack([G[f"ffn_conv_b{l}"] for l in range(2)])
    out["ffn_w_down"] = jnp.stack([G[f"ffn_w_down{l}"] for l in range(2)])
    for n in ("mix_pre_g", "mix_post_g", "ffn_pre_g", "ffn_post_g"):
        out[n] = jnp.concatenate([G[n + "0"], G[n + "1"]], axis=0)
    return out


def _big_grad_stacks(G):
    st = {}
    st["attn_w_qkv"] = jnp.swapaxes(G["attn_w_qkv"].reshape(D_MODEL, N_CHIPS, -1), 0, 1)
    st["attn_w_o"] = G["attn_w_o"].reshape(N_CHIPS, -1, D_MODEL)
    w_in = jnp.concatenate([G["rg_w_in_gate"][:, :D_RNN], G["rg_w_in_rec"][:, :D_RNN]], axis=1)
    st["rg_w_in"] = jnp.swapaxes(w_in.reshape(D_MODEL, N_CHIPS, -1), 0, 1)
    st["rg_w_out"] = G["rg_w_out"][:D_RNN].reshape(N_CHIPS, RG_OUT_SHARD_ROWS, D_MODEL)
    half = D_FF // 2
    pieces = [[G[f"ffn_w_up_{part}{l}"][:, h * half:(h + 1) * half] for l in range(2)]
              for part in ("gate", "value") for h in range(2)]
    st["ffn_w_up"] = jnp.stack([jnp.concatenate(p, axis=0) for p in pieces])
    down = jnp.stack([G[f"ffn_w_down{l}"] for l in range(2)]).reshape(2, N_CHIPS, D_FF_SHARD, D_MODEL)
    st["ffn_w_down"] = jnp.swapaxes(down, 0, 1).reshape(N_CHIPS, 2 * D_FF_SHARD, D_MODEL)
    return st


def _reduce_grads(G, W, loss_part):
    big_names = [n for n, _ in _BIG]
    stacks = _big_grad_stacks(G)
    small_grads = _assemble_grads(G)
    rep_flat = jnp.concatenate([small_grads[n].reshape(-1) for n in _REPLICATED] + [loss_part.reshape(1)])
    n_rep = rep_flat.shape[0]
    piece = -(-n_rep // (N_CHIPS * FLAT_W)) * FLAT_W
    rep_flat = jnp.pad(rep_flat, (0, N_CHIPS * piece - n_rep)).reshape(N_CHIPS, piece)
    small_blocks = []
    for s in range(N_CHIPS):
        parts = [_split(small_grads[n], ax)[s] for n, ax in _SMALL_SHARDED] + [rep_flat[s]]
        small_blocks.append(_pack(parts, 32, F32))
    gs = [stacks[n] for n in big_names] + [jnp.stack(small_blocks)]
    tags = big_names + ["small"]
    landed = _pair_send_other_halves(gs, "rs_pair_send")
    chip_sums = [_add_own_half(g, l, f"rs_pair_add_{t}", F32 if t == "small" else BF16) for g, l, t in zip(gs, landed, tags)]
    slots = _scatter_to_chips(chip_sums, "rs_scatter")
    halves = [_sum_slots(s, f"rs_sum_{t}") for s, t in zip(slots, tags)]
    mine = _pair_gather(halves, "rs_pair_gather")
    out = {n: m.reshape(W[n].shape) for n, m in zip(big_names, mine[:-1])}
    shapes = [W[n].shape for n, _ in _SMALL_SHARDED] + [(piece,)]
    parts = _unpack(mine[-1], shapes)
    out.update(zip([n for n, _ in _SMALL_SHARDED], parts[:-1]))
    rep_rows = piece // FLAT_W
    rep_pad = -(-rep_rows // 32) * 32
    rep_piece = jnp.pad(parts[-1].reshape(rep_rows, FLAT_W), ((0, rep_pad - rep_rows), (0, 0)))
    rep_all = _all_gather_chips([rep_piece], "gather_replicated_grads")[0][:, :rep_rows].reshape(-1)
    out.update(zip(_REPLICATED + ["loss"], _unpack(rep_all, [W[n].shape for n in _REPLICATED] + [()])))
    return out


def _update(W, M, V, grads):
    delta, new_m, new_v = {}, {}, {}
    small = [n for n in _ORDER if n not in dict(_BIG)]
    for n, _ in _BIG:
        shp = W[n].shape
        two_d = (-1, shp[-1])
        d, m, v = _adamw(W[n].reshape(two_d), grads[n].reshape(two_d), M[n].reshape(two_d), V[n].reshape(two_d), f"adamw_{n}")
        delta[n], new_m[n], new_v[n] = d.reshape(shp), m.reshape(shp), v.reshape(shp)
    packed = [_pack([src[n] for n in small], 8, F32) for src in (W, grads, M, V)]
    outs = _adamw(*packed, "adamw_small")
    shapes = [W[n].shape for n in small]
    for tgt, flat in zip((delta, new_m, new_v), outs):
        tgt.update(zip(small, _unpack(flat, shapes)))
    return delta, new_m, new_v


def kernel(x, attn_w_qkv, attn_w_o, rg_w_in, rg_conv_w, rg_conv_b, rg_w_a, rg_b_a, rg_w_x, rg_b_x, rg_lambda, rg_w_out, ffn_w_up, ffn_conv_w, ffn_conv_b, ffn_w_down, mix_pre_g, mix_post_g, ffn_pre_g, ffn_post_g, loss_target, m_attn_w_qkv, m_attn_w_o, m_rg_w_in, m_rg_conv_w, m_rg_conv_b, m_rg_w_a, m_rg_b_a, m_rg_w_x, m_rg_b_x, m_rg_lambda, m_rg_w_out, m_ffn_w_up, m_ffn_conv_w, m_ffn_conv_b, m_ffn_w_down, m_mix_pre_g, m_mix_post_g, m_ffn_pre_g, m_ffn_post_g, v_attn_w_qkv, v_attn_w_o, v_rg_w_in, v_rg_conv_w, v_rg_conv_b, v_rg_w_a, v_rg_b_a, v_rg_w_x, v_rg_b_x, v_rg_lambda, v_rg_w_out, v_ffn_w_up, v_ffn_conv_w, v_ffn_conv_b, v_ffn_w_down, v_mix_pre_g, v_mix_post_g, v_ffn_pre_g, v_ffn_post_g):
    W = dict(attn_w_qkv=attn_w_qkv, attn_w_o=attn_w_o, rg_w_in=rg_w_in, rg_conv_w=rg_conv_w, rg_conv_b=rg_conv_b,
             rg_w_a=rg_w_a, rg_b_a=rg_b_a, rg_w_x=rg_w_x, rg_b_x=rg_b_x, rg_lambda=rg_lambda, rg_w_out=rg_w_out,
             ffn_w_up=ffn_w_up, ffn_conv_w=ffn_conv_w, ffn_conv_b=ffn_conv_b, ffn_w_down=ffn_w_down,
             mix_pre_g=mix_pre_g, mix_post_g=mix_post_g, ffn_pre_g=ffn_pre_g, ffn_post_g=ffn_post_g)
    M = dict(attn_w_qkv=m_attn_w_qkv, attn_w_o=m_attn_w_o, rg_w_in=m_rg_w_in, rg_conv_w=m_rg_conv_w, rg_conv_b=m_rg_conv_b,
             rg_w_a=m_rg_w_a, rg_b_a=m_rg_b_a, rg_w_x=m_rg_w_x, rg_b_x=m_rg_b_x, rg_lambda=m_rg_lambda, rg_w_out=m_rg_w_out,
             ffn_w_up=m_ffn_w_up, ffn_conv_w=m_ffn_conv_w, ffn_conv_b=m_ffn_conv_b, ffn_w_down=m_ffn_w_down,
             mix_pre_g=m_mix_pre_g, mix_post_g=m_mix_post_g, ffn_pre_g=m_ffn_pre_g, ffn_post_g=m_ffn_post_g)
    V = dict(attn_w_qkv=v_attn_w_qkv, attn_w_o=v_attn_w_o, rg_w_in=v_rg_w_in, rg_conv_w=v_rg_conv_w, rg_conv_b=v_rg_conv_b,
             rg_w_a=v_rg_w_a, rg_b_a=v_rg_b_a, rg_w_x=v_rg_w_x, rg_b_x=v_rg_b_x, rg_lambda=v_rg_lambda, rg_w_out=v_rg_w_out,
             ffn_w_up=v_ffn_w_up, ffn_conv_w=v_ffn_conv_w, ffn_conv_b=v_ffn_conv_b, ffn_w_down=v_ffn_w_down,
             mix_pre_g=v_mix_pre_g, mix_post_g=v_mix_post_g, ffn_pre_g=v_ffn_pre_g, ffn_post_g=v_ffn_post_g)
    P = _layout_params(_gather_params(W))
    loss_part, grad_x, G = _local_step(x[0], loss_target[0], P)
    grads = _reduce_grads(G, W, loss_part)
    loss = grads["loss"]
    delta, new_m, new_v = _update(W, M, V, grads)
    return (loss, grad_x[None], *[grads[n] for n in _ORDER], *[delta[n] for n in _ORDER],
            *[new_m[n] for n in _ORDER], *[new_v[n] for n in _ORDER])
```
